```python
import jax, jax.numpy as jnp
from jax import lax
import numpy as np

D_MODEL = 4096
BATCH = 4
SEQ = 4096
DEPTH = 1

CHUNK = 64
Q_BLOCK = 128
HEAD_DIM = 128
ROPE_THETA = 10000.0
EPS = 1e-6
N_HEADS_A = 16
N_KV_A = 2
Q_RANK = 1024
N_IDX_HEADS = 32
IDX_DIM = 64
IDX_ROPE_DIM = 32
INDEX_TOPK = 256
N_HEADS_B = 16
D_FF = ((8 * D_MODEL + 3 * 256 - 1) // (3 * 256)) * 256
PLE_DIM = 256

W_A = N_HEADS_A * HEAD_DIM
KV_A_W = N_KV_A * HEAD_DIM
W_B = N_HEADS_B * HEAD_DIM
SPLIT_SIZES = (Q_RANK, KV_A_W, KV_A_W, IDX_DIM, N_IDX_HEADS, W_B, W_B, W_B, N_HEADS_B, D_MODEL, D_MODEL)
IN_WIDTH = sum(SPLIT_SIZES)

kernel_name = "hybrid_dsa_fox_gated_block"


def _split_offsets():
    offs, acc = [], 0
    for s in SPLIT_SIZES[:-1]:
        acc += s
        offs.append(acc)
    return offs


def rms_norm(x, g):
    xf = x.astype(jnp.float32)
    y = xf * lax.rsqrt(jnp.mean(xf * xf, axis=-1, keepdims=True) + EPS)
    return (y * g.astype(jnp.float32)).astype(x.dtype)


def layer_norm(x, g, b):
    xf = x.astype(jnp.float32)
    mu = jnp.mean(xf, axis=-1, keepdims=True)
    var = jnp.mean(jnp.square(xf - mu), axis=-1, keepdims=True)
    y = (xf - mu) * lax.rsqrt(var + EPS) * g.astype(jnp.float32) + b.astype(jnp.float32)
    return y.astype(x.dtype)


def rope(x, pos, rot_dim):
    half = rot_dim // 2
    inv_freq = jnp.power(ROPE_THETA, -jnp.arange(half, dtype=jnp.float32) * (2.0 / rot_dim))
    ang = pos.astype(jnp.float32)[..., None] * inv_freq
    cos = jnp.cos(ang)[:, :, None, :]
    sin = jnp.sin(ang)[:, :, None, :]
    xf = x.astype(jnp.float32)
    x1 = xf[..., :half]
    x2 = xf[..., half:rot_dim]
    out = jnp.concatenate([x1 * cos - x2 * sin, x2 * cos + x1 * sin, xf[..., rot_dim:]], axis=-1)
    return out.astype(x.dtype)


def _to_blocks(a, n_blocks):
    b = a.shape[0]
    return jnp.moveaxis(a.reshape((b, n_blocks, Q_BLOCK) + a.shape[2:]), 1, 0)


def dsa_sparse_attention(q, k, v, q_idx, k_idx, w_idx, top_k):
    B, S, H, Dh = q.shape
    nb = S // Q_BLOCK
    G = H // N_KV_A
    scale = Dh ** -0.5
    key_pos = jnp.arange(S)

    def block(args):
        qb, qib, wb, start = args
        t = start + jnp.arange(Q_BLOCK)
        visible_end = (t // CHUNK + 1) * CHUNK
        s_idx = jnp.einsum('bqhd,bkd->bqhk', qib, k_idx).astype(jnp.float32)
        index = jnp.einsum('bqhk,bqh->bqk', jax.nn.relu(s_idx), wb.astype(jnp.float32))
        admissible = key_pos[None, :] < visible_end[:, None]
        index = jnp.where(admissible[None], index, -jnp.inf)
        _, sel = lax.top_k(index, top_k)
        valid = sel < visible_end[None, :, None]
        k_sel = jax.vmap(lambda kb, ib: kb[ib])(k, sel)
        v_sel = jax.vmap(lambda vb, ib: vb[ib])(v, sel)
        qg = qb.reshape(B, Q_BLOCK, N_KV_A, G, Dh)
        logits = jnp.einsum('bqgrd,bqkgd->bqgrk', qg, k_sel).astype(jnp.float32) * scale
        logits = jnp.where(valid[:, :, None, None, :], logits, -jnp.inf)
        probs = jax.nn.softmax(logits, axis=-1).astype(v.dtype)
        o = jnp.einsum('bqgrk,bqkgd->bqgrd', probs, v_sel)
        return o.reshape(B, Q_BLOCK, H * Dh)

    starts = jnp.arange(nb) * Q_BLOCK
    out = lax.map(block, (_to_blocks(q, nb), _to_blocks(q_idx, nb), _to_blocks(w_idx, nb), starts))
    return jnp.moveaxis(out, 0, 1).reshape(B, S, H * Dh)


def forgetting_attention(q, k, v, log_f):
    B, S, H, Dh = q.shape
    nb = S // Q_BLOCK
    scale = Dh ** -0.5
    key_pos = jnp.arange(S)
    F = jnp.cumsum(log_f, axis=1)
    F_k = jnp.transpose(F, (0, 2, 1))

    def block(args):
        qb, Fq, start = args
        t = start + jnp.arange(Q_BLOCK)
        causal = key_pos[None, :] <= t[:, None]
        decay = jnp.transpose(Fq, (0, 2, 1))[..., None] - F_k[:, :, None, :]
        logits = jnp.einsum('bqhd,bkhd->bhqk', qb, k).astype(jnp.float32) * scale + decay
        logits = jnp.where(causal[None, None], logits, -jnp.inf)
        probs = jax.nn.softmax(logits, axis=-1).astype(v.dtype)
        o = jnp.einsum('bhqk,bkhd->bqhd', probs, v)
        return o.reshape(B, Q_BLOCK, H * Dh)

    starts = jnp.arange(nb) * Q_BLOCK
    out = lax.map(block, (_to_blocks(q, nb), _to_blocks(F, nb), starts))
    return jnp.moveaxis(out, 0, 1).reshape(B, S, H * Dh)


def setup_inputs(seed: int = 0) -> dict:
    key = jax.random.key(seed)
    ks = jax.random.split(key, 28)
    f32 = jnp.float32

    def nrm(k, shape, fan_in):
        return jax.random.normal(k, shape, f32) * (fan_in ** -0.5)

    def gain(k, n):
        return 1.0 + 0.01 * jax.random.normal(k, (DEPTH, n), f32)

    x = jax.random.normal(ks[0], (BATCH, SEQ, D_MODEL), f32)
    p = jax.random.normal(ks[1], (DEPTH, BATCH, SEQ, PLE_DIM), f32)
    offsets = jax.random.randint(ks[2], (BATCH, 1), 0, 1024) * CHUNK
    positions = (offsets + jnp.arange(SEQ)[None, :]).astype(jnp.int32)
    return {
        "x": x,
        "p": p,
        "positions": positions,
        "g_attn": gain(ks[3], D_MODEL),
        "w_in": nrm(ks[4], (DEPTH, D_MODEL, IN_WIDTH), D_MODEL),
        "g_cq": gain(ks[5], Q_RANK),
        "w_uq": nrm(ks[6], (DEPTH, Q_RANK, W_A), Q_RANK),
        "w_uq_idx": nrm(ks[7], (DEPTH, Q_RANK, N_IDX_HEADS * IDX_DIM), Q_RANK),
        "g_kidx": gain(ks[8], IDX_DIM),
        "b_kidx": 0.01 * jax.random.normal(ks[9], (DEPTH, IDX_DIM), f32),
        "g_q_a": gain(ks[10], HEAD_DIM),
        "g_k_a": gain(ks[11], HEAD_DIM),
        "b_forget": jax.random.uniform(ks[12], (DEPTH, N_HEADS_B), f32, 1.0, 4.0),
        "g_q_b": gain(ks[13], HEAD_DIM),
        "g_k_b": gain(ks[14], HEAD_DIM),
        "w_up_a": nrm(ks[15], (DEPTH, W_A, D_MODEL), W_A),
        "w_up_b": nrm(ks[16], (DEPTH, W_B, D_MODEL), W_B),
        "w_o": nrm(ks[17], (DEPTH, D_MODEL, D_MODEL), D_MODEL),
        "g_ffn": gain(ks[18], D_MODEL),
        "w_ffn_gate": nrm(ks[19], (DEPTH, D_MODEL, D_FF), D_MODEL),
        "w_ffn_up": nrm(ks[20], (DEPTH, D_MODEL, D_FF), D_MODEL),
        "w_ffn_down": nrm(ks[21], (DEPTH, D_FF, D_MODEL), D_FF),
        "g_ple": gain(ks[22], D_MODEL),
        "w_ple": nrm(ks[23], (DEPTH, PLE_DIM, D_MODEL), PLE_DIM),
        "w_ple_gate": nrm(ks[24], (DEPTH, D_MODEL, D_MODEL), D_MODEL),
    }


def reference(x, p, positions, g_attn, w_in, g_cq, w_uq, w_uq_idx, g_kidx, b_kidx,
              g_q_a, g_k_a, b_forget, g_q_b, g_k_b, w_up_a, w_up_b, w_o,
              g_ffn, w_ffn_gate, w_ffn_up, w_ffn_down, g_ple, w_ple, w_ple_gate):
    B, S, _ = x.shape
    top_k = min(INDEX_TOPK, S // 4)
    offs = _split_offsets()
    idx_w_scale = (N_IDX_HEADS ** -0.5) * (IDX_DIM ** -0.5)
    for i in range(DEPTH):
        h = rms_norm(x, g_attn[i])
        proj = h @ w_in[i]
        (c_q, k_a, v_a, k_idx, w_idx, q_b, k_b, v_b, f_b, gate_a, gate_b) = jnp.split(proj, offs, axis=-1)

        c_q = rms_norm(c_q, g_cq[i])
        q_a = rms_norm((c_q @ w_uq[i]).reshape(B, S, N_HEADS_A, HEAD_DIM), g_q_a[i])
        q_a = rope(q_a, positions, HEAD_DIM)
        k_a = rope(rms_norm(k_a.reshape(B, S, N_KV_A, HEAD_DIM), g_k_a[i]), positions, HEAD_DIM)
        v_a = v_a.reshape(B, S, N_KV_A, HEAD_DIM)
        q_idx = rope((c_q @ w_uq_idx[i]).reshape(B, S, N_IDX_HEADS, IDX_DIM), positions, IDX_ROPE_DIM)
        k_idx = rope(layer_norm(k_idx, g_kidx[i], b_kidx[i])[:, :, None, :], positions, IDX_ROPE_DIM)[:, :, 0, :]
        o_a = dsa_sparse_attention(q_a, k_a, v_a, q_idx, k_idx, w_idx * idx_w_scale, top_k)

        q_b = rms_norm(q_b.reshape(B, S, N_HEADS_B, HEAD_DIM), g_q_b[i])
        k_b = rms_norm(k_b.reshape(B, S, N_HEADS_B, HEAD_DIM), g_k_b[i])
        v_b = v_b.reshape(B, S, N_HEADS_B, HEAD_DIM)
        log_f = jax.nn.log_sigmoid((f_b + b_forget[i]).astype(jnp.float32))
        o_b = forgetting_attention(q_b, k_b, v_b, log_f)

        merged = jax.nn.sigmoid(gate_a) * (o_a @ w_up_a[i]) + jax.nn.sigmoid(gate_b) * (o_b @ w_up_b[i])
        x = x + merged @ w_o[i]

        h = rms_norm(x, g_ffn[i])
        x = x + (jax.nn.silu(h @ w_ffn_gate[i]) * (h @ w_ffn_up[i])) @ w_ffn_down[i]

        h = rms_norm(x, g_ple[i])
        x = x + jax.nn.sigmoid(h @ w_ple_gate[i]) * (p[i] @ w_ple[i])
    return x
```

```python
import functools

import numpy as np
import jax
import jax.numpy as jnp
from jax import lax
from jax.experimental import pallas as pl
from jax.experimental.pallas import tpu as pltpu

CHUNK = 64
CHUNK_SHIFT = CHUNK.bit_length() - 1
HEAD_DIM = 128
ROPE_THETA = 10000.0
EPS = 1e-6
N_HEADS_A = 16
N_KV_A = 2
Q_RANK = 1024
N_IDX_HEADS = 32
IDX_DIM = 64
IDX_ROPE_DIM = 32
INDEX_TOPK = 256
N_HEADS_B = 16
PLE_DIM = 256

LANES = 128
KV_A_W = N_KV_A * HEAD_DIM
W_A = N_HEADS_A * HEAD_DIM
W_B = N_HEADS_B * HEAD_DIM
GROUP_A = N_HEADS_A // N_KV_A

MISC_W_LANE = IDX_DIM
MISC_F_LANE = IDX_DIM + N_IDX_HEADS

MASK_NEG = -1e30
SCORE_NEG = -3e38
SELECT_ITERS = 32
VMEM_LIMIT = 56 * 1024 * 1024


def _cparams(sem):
    return pltpu.CompilerParams(dimension_semantics=sem, vmem_limit_bytes=VMEM_LIMIT)


def _sigmoid(x):
    return 1.0 / (1.0 + jnp.exp(-x))


def _nt_dot(a, b):
    return lax.dot_general(a, b, (((1,), (1,)), ((), ())), preferred_element_type=jnp.float32)


def _rmsnorm_kernel(x_ref, g_ref, o_ref):
    x = x_ref[...]
    ms = jnp.mean(x * x, axis=-1, keepdims=True)
    o_ref[...] = (x * lax.rsqrt(ms + EPS) * g_ref[...]).astype(o_ref.dtype)


def _rmsnorm(x, g, tm=256):
    t, d = x.shape
    tm = min(tm, t)
    return pl.pallas_call(
        _rmsnorm_kernel,
        grid=(t // tm,),
        in_specs=[pl.BlockSpec((tm, d), lambda i: (i, 0)), pl.BlockSpec((1, d), lambda i: (0, 0))],
        out_specs=pl.BlockSpec((tm, d), lambda i: (i, 0)),
        out_shape=jax.ShapeDtypeStruct((t, d), jnp.bfloat16),
        compiler_params=_cparams(("parallel",)),
        name="rmsnorm",
    )(x, g.reshape(1, d))


def _mm_kernel(a_ref, w_ref, o_ref):
    o_ref[...] = jnp.dot(a_ref[...], w_ref[...], preferred_element_type=jnp.float32).astype(o_ref.dtype)


def _matmul(a, w, out_dtype, tm=1024, tn=512, name="matmul"):
    t, k = a.shape
    n = w.shape[1]
    tm, tn = min(tm, t), min(tn, n)
    return pl.pallas_call(
        _mm_kernel,
        grid=(t // tm, n // tn),
        in_specs=[pl.BlockSpec((tm, k), lambda i, j: (i, 0)), pl.BlockSpec((k, tn), lambda i, j: (0, j))],
        out_specs=pl.BlockSpec((tm, tn), lambda i, j: (i, j)),
        out_shape=jax.ShapeDtypeStruct((t, n), out_dtype),
        compiler_params=_cparams(("parallel", "parallel")),
        name=name,
    )(a, w)


def _mm_res_kernel(a_ref, w_ref, r_ref, o_ref):
    o_ref[...] = r_ref[...] + jnp.dot(a_ref[...], w_ref[...], preferred_element_type=jnp.float32)


def _matmul_residual(a, w, r, tm=1024, tn=512, name="matmul_residual"):
    t, k = a.shape
    n = w.shape[1]
    tm, tn = min(tm, t), min(tn, n)
    return pl.pallas_call(
        _mm_res_kernel,
        grid=(t // tm, n // tn),
        in_specs=[
            pl.BlockSpec((tm, k), lambda i, j: (i, 0)),
            pl.BlockSpec((k, tn), lambda i, j: (0, j)),
            pl.BlockSpec((tm, tn), lambda i, j: (i, j)),
        ],
        out_specs=pl.BlockSpec((tm, tn), lambda i, j: (i, j)),
        out_shape=jax.ShapeDtypeStruct((t, n), jnp.float32),
        compiler_params=_cparams(("parallel", "parallel")),
        name=name,
    )(a, w, r)


def _mm_res_acc_kernel(a_ref, w_ref, r_ref, o_ref, acc_ref):
    kk = pl.program_id(2)

    @pl.when(kk == 0)
    def _():
        acc_ref[...] = r_ref[...]

    acc_ref[...] += jnp.dot(a_ref[...], w_ref[...], preferred_element_type=jnp.float32)

    @pl.when(kk == pl.num_programs(2) - 1)
    def _():
        o_ref[...] = acc_ref[...]


def _matmul_residual_ksplit(a, w, r, tm=1024, tn=1024, tk=1024, name="matmul_residual_ksplit"):
    t, k = a.shape
    n = w.shape[1]
    tm, tn, tk = min(tm, t), min(tn, n), min(tk, k)
    return pl.pallas_call(
        _mm_res_acc_kernel,
        grid=(t // tm, n // tn, k // tk),
        in_specs=[
            pl.BlockSpec((tm, tk), lambda i, j, kk: (i, kk)),
            pl.BlockSpec((tk, tn), lambda i, j, kk: (kk, j)),
            pl.BlockSpec((tm, tn), lambda i, j, kk: (i, j)),
        ],
        out_specs=pl.BlockSpec((tm, tn), lambda i, j, kk: (i, j)),
        out_shape=jax.ShapeDtypeStruct((t, n), jnp.float32),
        scratch_shapes=[pltpu.VMEM((tm, tn), jnp.float32)],
        compiler_params=_cparams(("parallel", "parallel", "arbitrary")),
        name=name,
    )(a, w, r)


def _swiglu_kernel(a_ref, wg_ref, wu_ref, o_ref):
    a = a_ref[...]
    g = jnp.dot(a, wg_ref[...], preferred_element_type=jnp.float32)
    u = jnp.dot(a, wu_ref[...], preferred_element_type=jnp.float32)
    o_ref[...] = (g * _sigmoid(g) * u).astype(o_ref.dtype)


def _swiglu(a, wg, wu, tm=1024, tn=512):
    t, k = a.shape
    n = wg.shape[1]
    tm, tn = min(tm, t), min(tn, n)
    return pl.pallas_call(
        _swiglu_kernel,
        grid=(t // tm, n // tn),
        in_specs=[
            pl.BlockSpec((tm, k), lambda i, j: (i, 0)),
            pl.BlockSpec((k, tn), lambda i, j: (0, j)),
            pl.BlockSpec((k, tn), lambda i, j: (0, j)),
        ],
        out_specs=pl.BlockSpec((tm, tn), lambda i, j: (i, j)),
        out_shape=jax.ShapeDtypeStruct((t, n), jnp.bfloat16),
        compiler_params=_cparams(("parallel", "parallel")),
        name="swiglu",
    )(a, wg, wu)


def _merge_kernel(oa_ref, ob_ref, wa_ref, wb_ref, ga_ref, gb_ref, o_ref):
    a = jnp.dot(oa_ref[...], wa_ref[...], preferred_element_type=jnp.float32)
    b = jnp.dot(ob_ref[...], wb_ref[...], preferred_element_type=jnp.float32)
    ga = _sigmoid(ga_ref[...].astype(jnp.float32))
    gb = _sigmoid(gb_ref[...].astype(jnp.float32))
    o_ref[...] = (ga * a + gb * b).astype(o_ref.dtype)


def _merge(o_a, o_b, w_up_a, w_up_b, proj, ga_col, gb_col, tm=1024, tn=512):
    t, ka = o_a.shape
    kb = o_b.shape[1]
    n = w_up_a.shape[1]
    tm, tn = min(tm, t), min(tn, n)
    ga_blk, gb_blk = ga_col // tn, gb_col // tn
    return pl.pallas_call(
        _merge_kernel,
        grid=(t // tm, n // tn),
        in_specs=[
            pl.BlockSpec((tm, ka), lambda i, j: (i, 0)),
            pl.BlockSpec((tm, kb), lambda i, j: (i, 0)),
            pl.BlockSpec((ka, tn), lambda i, j: (0, j)),
            pl.BlockSpec((kb, tn), lambda i, j: (0, j)),
            pl.BlockSpec((tm, tn), lambda i, j: (i, ga_blk + j)),
            pl.BlockSpec((tm, tn), lambda i, j: (i, gb_blk + j)),
        ],
        out_specs=pl.BlockSpec((tm, tn), lambda i, j: (i, j)),
        out_shape=jax.ShapeDtypeStruct((t, n), jnp.bfloat16),
        compiler_params=_cparams(("parallel", "parallel")),
        name="merge",
    )(o_a, o_b, w_up_a, w_up_b, proj, proj)


def _ple_kernel(h_ref, wg_ref, p_ref, wp_ref, r_ref, o_ref):
    g = jnp.dot(h_ref[...], wg_ref[...], preferred_element_type=jnp.float32)
    e = jnp.dot(p_ref[...], wp_ref[...], preferred_element_type=jnp.float32)
    o_ref[...] = r_ref[...] + _sigmoid(g) * e


def _ple(h, w_gate, p, w_ple, r, tm=1024, tn=512):
    t, k = h.shape
    kp = p.shape[1]
    n = w_gate.shape[1]
    tm, tn = min(tm, t), min(tn, n)
    return pl.pallas_call(
        _ple_kernel,
        grid=(t // tm, n // tn),
        in_specs=[
            pl.BlockSpec((tm, k), lambda i, j: (i, 0)),
            pl.BlockSpec((k, tn), lambda i, j: (0, j)),
            pl.BlockSpec((tm, kp), lambda i, j: (i, 0)),
            pl.BlockSpec((kp, tn), lambda i, j: (0, j)),
            pl.BlockSpec((tm, tn), lambda i, j: (i, j)),
        ],
        out_specs=pl.BlockSpec((tm, tn), lambda i, j: (i, j)),
        out_shape=jax.ShapeDtypeStruct((t, n), jnp.float32),
        compiler_params=_cparams(("parallel", "parallel")),
        name="ple",
    )(h, w_gate, p, w_ple, r)


def _rope_consts():
    half = HEAD_DIM // 2
    inv_full = jnp.power(ROPE_THETA, -jnp.arange(half, dtype=jnp.float32) * (2.0 / HEAD_DIM))
    half_i = IDX_ROPE_DIM // 2
    inv_idx = jnp.power(ROPE_THETA, -jnp.arange(half_i, dtype=jnp.float32) * (2.0 / IDX_ROPE_DIM))
    zeros = jnp.zeros((LANES - IDX_ROPE_DIM,), jnp.float32)
    sign = np.concatenate([-np.ones(half, np.float32), np.ones(half, np.float32)])
    mask_a = np.zeros(LANES, np.float32)
    mask_a[:half_i] = -1.0
    mask_b = np.zeros(LANES, np.float32)
    mask_b[half_i:IDX_ROPE_DIM] = 1.0
    rows = [
        jnp.concatenate([inv_full, inv_full]),
        jnp.asarray(sign),
        jnp.concatenate([inv_idx, inv_idx, zeros]),
        jnp.asarray(mask_a),
        jnp.asarray(mask_b),
    ]
    rows += [jnp.zeros((LANES,), jnp.float32)] * 3
    return jnp.stack(rows)


def _rope_tables_kernel(pos_ref, c_ref, cf_ref, sf_ref, ci_ref, sa_ref, sb_ref):
    pos = pos_ref[...].astype(jnp.float32)
    ang = pos * c_ref[0:1, :]
    cf_ref[...] = jnp.cos(ang)
    sf_ref[...] = jnp.sin(ang) * c_ref[1:2, :]
    ang_i = pos * c_ref[2:3, :]
    ci_ref[...] = jnp.cos(ang_i)
    s_i = jnp.sin(ang_i)
    sa_ref[...] = s_i * c_ref[3:4, :]
    sb_ref[...] = s_i * c_ref[4:5, :]


def _rope_tables(pos_col, tm=512):
    t = pos_col.shape[0]
    tm = min(tm, t)
    tab = jax.ShapeDtypeStruct((t, LANES), jnp.float32)
    spec = pl.BlockSpec((tm, LANES), lambda i: (i, 0))
    return pl.pallas_call(
        _rope_tables_kernel,
        grid=(t // tm,),
        in_specs=[pl.BlockSpec((tm, 1), lambda i: (i, 0)), pl.BlockSpec((8, LANES), lambda i: (0, 0))],
        out_specs=[spec] * 5,
        out_shape=[tab] * 5,
        compiler_params=_cparams(("parallel",)),
        name="rope_tables",
    )(pos_col, _rope_consts())


def _rope_full(x, cf, sf):
    return x * cf + pltpu.roll(x, HEAD_DIM // 2, axis=1) * sf


def _rope_idx(x, ci, sa, sb):
    half = IDX_ROPE_DIM // 2
    return x * ci + pltpu.roll(x, LANES - half, axis=1) * sa + pltpu.roll(x, half, axis=1) * sb


def _head_rms(x, g):
    ms = jnp.mean(x * x, axis=-1, keepdims=True)
    return x * lax.rsqrt(ms + EPS) * g


def _q_proj_kernel(cq_ref, gcq_ref, wa_ref, wi_ref, gqa_ref, cf_ref, sf_ref, ci_ref, sa_ref, sb_ref, qa_ref, qi_ref):
    cq = cq_ref[...].astype(jnp.float32)
    ms = jnp.mean(cq * cq, axis=-1, keepdims=True)
    cq = (cq * lax.rsqrt(ms + EPS) * gcq_ref[...]).astype(jnp.bfloat16)
    cf, sf = cf_ref[...], sf_ref[...]
    gqa = gqa_ref[...]
    for j in range(N_HEADS_A // 2):
        acc = jnp.dot(cq, wa_ref[:, j * 2 * HEAD_DIM:(j + 1) * 2 * HEAD_DIM], preferred_element_type=jnp.float32)
        for e in range(2):
            hq = _head_rms(acc[:, e * HEAD_DIM:(e + 1) * HEAD_DIM], gqa)
            qa_ref[2 * j + e] = _rope_full(hq, cf, sf).astype(qa_ref.dtype)
    ci, sa, sb = ci_ref[...], sa_ref[...], sb_ref[...]
    for j in range(N_IDX_HEADS // 2):
        acc = jnp.dot(cq, wi_ref[:, j * 2 * LANES:(j + 1) * 2 * LANES], preferred_element_type=jnp.float32)
        for e in range(2):
            qi_ref[2 * j + e] = _rope_idx(acc[:, e * LANES:(e + 1) * LANES], ci, sa, sb).astype(qi_ref.dtype)


def _q_proj(proj, g_cq, w_uq, w_uq_idx_pad, g_q_a, tabs, tm=512):
    t = proj.shape[0]
    tm = min(tm, t)
    cf, sf, ci, sa, sb = tabs
    tab_spec = pl.BlockSpec((tm, LANES), lambda i: (i, 0))
    return pl.pallas_call(
        _q_proj_kernel,
        grid=(t // tm,),
        in_specs=[
            pl.BlockSpec((tm, Q_RANK), lambda i: (i, 0)),
            pl.BlockSpec((1, Q_RANK), lambda i: (0, 0)),
            pl.BlockSpec((Q_RANK, W_A), lambda i: (0, 0)),
            pl.BlockSpec((Q_RANK, N_IDX_HEADS * LANES), lambda i: (0, 0)),
            pl.BlockSpec((1, HEAD_DIM), lambda i: (0, 0)),
            tab_spec, tab_spec, tab_spec, tab_spec, tab_spec,
        ],
        out_specs=[
            pl.BlockSpec((N_HEADS_A, tm, HEAD_DIM), lambda i: (0, i, 0)),
            pl.BlockSpec((N_IDX_HEADS, tm, LANES), lambda i: (0, i, 0)),
        ],
        out_shape=[
            jax.ShapeDtypeStruct((N_HEADS_A, t, HEAD_DIM), jnp.bfloat16),
            jax.ShapeDtypeStruct((N_IDX_HEADS, t, LANES), jnp.bfloat16),
        ],
        compiler_params=_cparams(("parallel",)),
        name="q_proj",
    )(proj, g_cq.reshape(1, Q_RANK), w_uq, w_uq_idx_pad, g_q_a.reshape(1, HEAD_DIM), cf, sf, ci, sa, sb)


def _ka_kernel(x_ref, g_ref, cf_ref, sf_ref, o_ref):
    x = x_ref[...].astype(jnp.float32)
    g, cf, sf = g_ref[...], cf_ref[...], sf_ref[...]
    for h in range(N_KV_A):
        hk = _head_rms(x[:, h * HEAD_DIM:(h + 1) * HEAD_DIM], g)
        o_ref[:, h * HEAD_DIM:(h + 1) * HEAD_DIM] = _rope_full(hk, cf, sf).astype(o_ref.dtype)


def _ka_norm_rope(proj, col, g_k_a, cf, sf, tm=512):
    t = proj.shape[0]
    tm = min(tm, t)
    tab_spec = pl.BlockSpec((tm, LANES), lambda i: (i, 0))
    return pl.pallas_call(
        _ka_kernel,
        grid=(t // tm,),
        in_specs=[
            pl.BlockSpec((tm, KV_A_W), lambda i: (i, col // KV_A_W)),
            pl.BlockSpec((1, HEAD_DIM), lambda i: (0, 0)),
            tab_spec, tab_spec,
        ],
        out_specs=pl.BlockSpec((tm, KV_A_W), lambda i: (i, 0)),
        out_shape=jax.ShapeDtypeStruct((t, KV_A_W), jnp.bfloat16),
        compiler_params=_cparams(("parallel",)),
        name="ka_norm_rope",
    )(proj, g_k_a.reshape(1, HEAD_DIM), cf, sf)


def _headnorm_kernel(x_ref, g_ref, o_ref, *, heads):
    x = x_ref[...].astype(jnp.float32)
    g = g_ref[...]
    for h in range(heads):
        sl = slice(h * HEAD_DIM, (h + 1) * HEAD_DIM)
        o_ref[:, sl] = _head_rms(x[:, sl], g[:, sl]).astype(o_ref.dtype)


def _qkb_norm(proj, col, gains, tm=512, tn=512):
    t = proj.shape[0]
    n = gains.shape[0]
    tm = min(tm, t)
    blk0 = col // tn
    return pl.pallas_call(
        functools.partial(_headnorm_kernel, heads=tn // HEAD_DIM),
        grid=(t // tm, n // tn),
        in_specs=[
            pl.BlockSpec((tm, tn), lambda i, j: (i, blk0 + j)),
            pl.BlockSpec((1, tn), lambda i, j: (0, j)),
        ],
        out_specs=pl.BlockSpec((tm, tn), lambda i, j: (i, j)),
        out_shape=jax.ShapeDtypeStruct((t, n), jnp.bfloat16),
        compiler_params=_cparams(("parallel", "parallel")),
        name="qkb_norm",
    )(proj, gains.reshape(1, n))


def _split3_bf16(x):
    hi = x.astype(jnp.bfloat16)
    r1 = x - hi.astype(jnp.float32)
    mid = r1.astype(jnp.bfloat16)
    lo = (r1 - mid.astype(jnp.float32)).astype(jnp.bfloat16)
    return hi, mid, lo


def _misc_kernel(m_ref, c_ref, ci_ref, sa_ref, sb_ref, tri_ref, kidx_ref, wf_ref, frow_ref, carry_ref, *, tiles_per_seq):
    i = pl.program_id(0)

    @pl.when(i % tiles_per_seq == 0)
    def _():
        carry_ref[...] = jnp.zeros_like(carry_ref)

    x = m_ref[...]
    lane = lax.broadcasted_iota(jnp.int32, x.shape, 1)
    is_k = lane < IDX_DIM
    xk = jnp.where(is_k, x, 0.0)
    mu = jnp.sum(xk, axis=-1, keepdims=True) * (1.0 / IDX_DIM)
    dk = jnp.where(is_k, x - mu, 0.0)
    var = jnp.sum(dk * dk, axis=-1, keepdims=True) * (1.0 / IDX_DIM)
    y = dk * lax.rsqrt(var + EPS) * c_ref[0:1, :] + c_ref[1:2, :]
    y = _rope_idx(y, ci_ref[...], sa_ref[...], sb_ref[...])
    kidx_ref[...] = jnp.where(is_k, y, 0.0).astype(kidx_ref.dtype)
    f = x + c_ref[2:3, :]
    log_f = jnp.minimum(f, 0.0) - jnp.log1p(jnp.exp(-jnp.abs(f)))
    hi, mid, lo = _split3_bf16(log_f)
    tri = tri_ref[...]
    csum = (jnp.dot(tri, hi, preferred_element_type=jnp.float32)
            + jnp.dot(tri, mid, preferred_element_type=jnp.float32)
            + jnp.dot(tri, lo, preferred_element_type=jnp.float32))
    csum = csum + carry_ref[0:1, :]
    carry_ref[...] = jnp.broadcast_to(csum[-1:, :], carry_ref.shape)
    is_w = (lane >= MISC_W_LANE) & (lane < MISC_F_LANE)
    wf = jnp.where(is_w, x * c_ref[3:4, :], csum)
    wf_ref[...] = wf
    frow_ref[...] = wf.T[MISC_F_LANE:MISC_F_LANE + N_HEADS_B, :]


def _misc_post(misc, g_kidx, b_kidx, b_forget, tabs, seq, tm=512):
    t = misc.shape[0]
    tm = min(tm, seq)
    idx_w_scale = (N_IDX_HEADS ** -0.5) * (IDX_DIM ** -0.5)
    pad = lambda v, off: jnp.zeros((LANES,), jnp.float32).at[off:off + v.shape[0]].set(v)
    consts = jnp.stack([
        pad(g_kidx, 0), pad(b_kidx, 0), pad(b_forget, MISC_F_LANE),
        pad(jnp.full((N_IDX_HEADS,), idx_w_scale, jnp.float32), MISC_W_LANE),
    ] + [jnp.zeros((LANES,), jnp.float32)] * 4)
    tri = jnp.asarray(np.tril(np.ones((tm, tm), np.float32)), jnp.bfloat16)
    _, _, ci, sa, sb = tabs
    tab_spec = pl.BlockSpec((tm, LANES), lambda i: (i, 0))
    return pl.pallas_call(
        functools.partial(_misc_kernel, tiles_per_seq=seq // tm),
        grid=(t // tm,),
        in_specs=[
            tab_spec,
            pl.BlockSpec((8, LANES), lambda i: (0, 0)),
            tab_spec, tab_spec, tab_spec,
            pl.BlockSpec((tm, tm), lambda i: (0, 0)),
        ],
        out_specs=[tab_spec, tab_spec, pl.BlockSpec((N_HEADS_B, tm), lambda i: (0, i))],
        out_shape=[
            jax.ShapeDtypeStruct((t, LANES), jnp.bfloat16),
            jax.ShapeDtypeStruct((t, LANES), jnp.float32),
            jax.ShapeDtypeStruct((N_HEADS_B, t), jnp.float32),
        ],
        scratch_shapes=[pltpu.VMEM((8, LANES), jnp.float32)],
        compiler_params=_cparams(("arbitrary",)),
        name="misc_post",
    )(misc, consts, ci, sa, sb, tri)


def _indexer_kernel(q_ref, k_ref, w_ref, o_ref, sc_ref, wb_ref, *, tq, tk, n_kt, top_k):
    qi = pl.program_id(1)
    t0 = qi * tq
    n_vis = (t0 + tq + tk - 1) // tk

    w = w_ref[...]
    for h in range(N_IDX_HEADS):
        wb_ref[h] = jnp.broadcast_to(w[:, MISC_W_LANE + h:MISC_W_LANE + h + 1], (tq, LANES))

    row = lax.broadcasted_iota(jnp.int32, (tq, 1), 0) + t0
    vis_end = (jnp.right_shift(row, CHUNK_SHIFT) + 1) * CHUNK
    q_all = q_ref[...].reshape(N_IDX_HEADS * tq, LANES)
    n_cc = tk // LANES

    def score_tile(c, carry):
        mx, mn = carry
        kt = k_ref[pl.ds(pl.multiple_of(c * tk, tk), tk), :]
        s = _nt_dot(q_all, kt)
        col = lax.broadcasted_iota(jnp.int32, (tq, LANES), 1) + c * tk
        for cc in range(n_cc):
            acc = jnp.zeros((tq, LANES), jnp.float32)
            for h in range(N_IDX_HEADS):
                sh = s[h * tq:(h + 1) * tq, cc * LANES:(cc + 1) * LANES]
                acc = acc + jnp.maximum(sh, 0.0) * wb_ref[h]
            adm = (col + cc * LANES) < vis_end
            sc_ref[c, :, cc * LANES:(cc + 1) * LANES] = jnp.where(adm, acc, SCORE_NEG)
            mx = jnp.maximum(mx, jnp.where(adm, acc, SCORE_NEG))
            mn = jnp.minimum(mn, jnp.where(adm, acc, -SCORE_NEG))
        return mx, mn

    mx, mn = lax.fori_loop(
        0, n_vis, score_tile,
        (jnp.full((tq, LANES), SCORE_NEG, jnp.float32), jnp.full((tq, LANES), -SCORE_NEG, jnp.float32)))
    hi0 = jnp.max(mx, axis=1, keepdims=True)
    lo0 = jnp.min(mn, axis=1, keepdims=True)

    def bisect(_, lohi):
        lo, hi = lohi
        mid = 0.5 * (lo + hi)
        midb = jnp.broadcast_to(mid, (tq, LANES))

        def count_tile(c, cnt):
            for cc in range(n_cc):
                v = sc_ref[c, :, cc * LANES:(cc + 1) * LANES]
                cnt = cnt + jnp.where(v >= midb, 1.0, 0.0)
            return cnt

        cnt = lax.fori_loop(0, n_vis, count_tile, jnp.zeros((tq, LANES), jnp.float32))
        enough = jnp.sum(cnt, axis=1, keepdims=True) >= float(top_k)
        return jnp.where(enough, mid, lo), jnp.where(enough, hi, mid)

    thr, _ = lax.fori_loop(0, SELECT_ITERS, bisect, (lo0, hi0))

    def write_tile(c, carry):
        o_ref[0, c] = jnp.where(sc_ref[c] >= thr, 0.0, MASK_NEG).astype(o_ref.dtype)
        return carry

    lax.fori_loop(0, n_vis, write_tile, 0)

    def fill_tile(c, carry):
        o_ref[0, c] = jnp.full((tq, tk), MASK_NEG, o_ref.dtype)
        return carry

    lax.fori_loop(n_vis, n_kt, fill_tile, 0)


def _indexer_mask(q_idx, k_idx, wf, batch, seq, top_k, tq=128, tk=512):
    tq, tk = min(tq, seq), min(tk, seq)
    n_qt, n_kt = seq // tq, seq // tk
    return pl.pallas_call(
        functools.partial(_indexer_kernel, tq=tq, tk=tk, n_kt=n_kt, top_k=top_k),
        grid=(batch, n_qt),
        in_specs=[
            pl.BlockSpec((N_IDX_HEADS, tq, LANES), lambda b, i: (0, b * n_qt + i, 0)),
            pl.BlockSpec((seq, LANES), lambda b, i: (b, 0)),
            pl.BlockSpec((tq, LANES), lambda b, i: (b * n_qt + i, 0)),
        ],
        out_specs=pl.BlockSpec((1, n_kt, tq, tk), lambda b, i: (b, 0, i, 0)),
        out_shape=jax.ShapeDtypeStruct((batch, n_kt, seq, tk), jnp.bfloat16),
        scratch_shapes=[
            pltpu.VMEM((n_kt, tq, tk), jnp.float32),
            pltpu.VMEM((N_IDX_HEADS, tq, LANES), jnp.float32),
        ],
        compiler_params=_cparams(("parallel", "parallel")),
        name="indexer_mask",
    )(q_idx, k_idx, wf)


def _softmax_step(s, v, m_ref, l_ref, acc_ref):
    m_prev = m_ref[...]
    m_new = jnp.maximum(m_prev, jnp.max(s, axis=-1, keepdims=True))
    alpha = jnp.exp(m_prev - m_new)
    p = jnp.exp(s - m_new)
    l_ref[...] = alpha * l_ref[...] + jnp.sum(p, axis=-1, keepdims=True)
    acc_ref[...] = alpha * acc_ref[...] + jnp.dot(p.astype(v.dtype), v, preferred_element_type=jnp.float32)
    m_ref[...] = m_new


def _softmax_init(m_ref, l_ref, acc_ref):
    m_ref[...] = jnp.full(m_ref.shape, MASK_NEG, jnp.float32)
    l_ref[...] = jnp.zeros(l_ref.shape, jnp.float32)
    acc_ref[...] = jnp.zeros(acc_ref.shape, jnp.float32)


def _dsa_attn_kernel(q_ref, k_ref, v_ref, b_ref, o_ref, m_ref, l_ref, acc_ref, *, tq, tk, scale):
    qi, ki = pl.program_id(1), pl.program_id(3)
    n_need = ((qi + 1) * tq + tk - 1) // tk

    @pl.when(ki == 0)
    def _():
        _softmax_init(m_ref, l_ref, acc_ref)

    @pl.when(ki < n_need)
    def _():
        q = q_ref[...].reshape(GROUP_A * tq, HEAD_DIM)
        s = _nt_dot(q, k_ref[...]) * scale
        s = (s.reshape(GROUP_A, tq, tk) + b_ref[0, 0].astype(jnp.float32)[None]).reshape(GROUP_A * tq, tk)
        _softmax_step(s, v_ref[...], m_ref, l_ref, acc_ref)

    @pl.when(ki == pl.num_programs(3) - 1)
    def _():
        out = acc_ref[...] / l_ref[...]
        for h in range(GROUP_A):
            o_ref[:, h * HEAD_DIM:(h + 1) * HEAD_DIM] = out[h * tq:(h + 1) * tq].astype(o_ref.dtype)


def _dsa_attention(q_a, k_a, proj, v_col, bias, batch, seq, tq=128):
    tk = bias.shape[-1]
    tq = min(tq, seq)
    n_qt, n_kt = seq // tq, seq // tk
    v_blk = v_col // HEAD_DIM
    last = lambda i: ((i + 1) * tq + tk - 1) // tk - 1
    kv_row = lambda b, i, k: b * n_kt + jnp.minimum(k, last(i))
    return pl.pallas_call(
        functools.partial(_dsa_attn_kernel, tq=tq, tk=tk, scale=HEAD_DIM ** -0.5),
        grid=(batch, n_qt, N_KV_A, n_kt),
        in_specs=[
            pl.BlockSpec((GROUP_A, tq, HEAD_DIM), lambda b, i, g, k: (g, b * n_qt + i, 0)),
            pl.BlockSpec((tk, HEAD_DIM), lambda b, i, g, k: (kv_row(b, i, k), g)),
            pl.BlockSpec((tk, HEAD_DIM), lambda b, i, g, k: (kv_row(b, i, k), v_blk + g)),
            pl.BlockSpec((1, 1, tq, tk), lambda b, i, g, k: (b, jnp.minimum(k, last(i)), i, 0)),
        ],
        out_specs=pl.BlockSpec((tq, GROUP_A * HEAD_DIM), lambda b, i, g, k: (b * n_qt + i, g)),
        out_shape=jax.ShapeDtypeStruct((batch * seq, W_A), jnp.bfloat16),
        scratch_shapes=[
            pltpu.VMEM((GROUP_A * tq, 1), jnp.float32),
            pltpu.VMEM((GROUP_A * tq, 1), jnp.float32),
            pltpu.VMEM((GROUP_A * tq, HEAD_DIM), jnp.float32),
        ],
        compiler_params=_cparams(("parallel", "parallel", "parallel", "arbitrary")),
        name="dsa_attention",
    )(q_a, k_a, proj, bias)


def _fox_kernel(q_ref, k_ref, v_ref, fq_ref, fk_ref, o_ref, m_ref, l_ref, acc_ref, *, tq, tk, scale):
    h, qi, ki = pl.program_id(1), pl.program_id(2), pl.program_id(3)
    n_need = ((qi + 1) * tq + tk - 1) // tk

    @pl.when(ki == 0)
    def _():
        _softmax_init(m_ref, l_ref, acc_ref)

    @pl.when(ki < n_need)
    def _():
        s = _nt_dot(q_ref[...], k_ref[...]) * scale
        lane = lax.broadcasted_iota(jnp.int32, (tq, LANES), 1)
        fq = jnp.sum(jnp.where(lane == MISC_F_LANE + h, fq_ref[...], 0.0), axis=1, keepdims=True)
        fk = fk_ref[pl.ds(h, 1), :]
        s = s + (fq - fk)
        t_pos = qi * tq + lax.broadcasted_iota(jnp.int32, (tq, tk), 0)
        s_pos = ki * tk + lax.broadcasted_iota(jnp.int32, (tq, tk), 1)
        s = jnp.where(s_pos <= t_pos, s, MASK_NEG)
        _softmax_step(s, v_ref[...], m_ref, l_ref, acc_ref)

    @pl.when(ki == pl.num_programs(3) - 1)
    def _():
        o_ref[...] = (acc_ref[...] / l_ref[...]).astype(o_ref.dtype)


def _fox_attention(qk, proj, v_col, wf, f_row, batch, seq, tq=512, tk=512):
    tq, tk = min(tq, seq), min(tk, seq)
    n_qt, n_kt = seq // tq, seq // tk
    v_blk = v_col // HEAD_DIM
    last = lambda i: ((i + 1) * tq + tk - 1) // tk - 1
    kv_row = lambda b, i, k: b * n_kt + jnp.minimum(k, last(i))
    return pl.pallas_call(
        functools.partial(_fox_kernel, tq=tq, tk=tk, scale=HEAD_DIM ** -0.5),
        grid=(batch, N_HEADS_B, n_qt, n_kt),
        in_specs=[
            pl.BlockSpec((tq, HEAD_DIM), lambda b, h, i, k: (b * n_qt + i, h)),
            pl.BlockSpec((tk, HEAD_DIM), lambda b, h, i, k: (kv_row(b, i, k), N_HEADS_B + h)),
            pl.BlockSpec((tk, HEAD_DIM), lambda b, h, i, k: (kv_row(b, i, k), v_blk + h)),
            pl.BlockSpec((tq, LANES), lambda b, h, i, k: (b * n_qt + i, 0)),
            pl.BlockSpec((N_HEADS_B, tk), lambda b, h, i, k: (0, kv_row(b, i, k))),
        ],
        out_specs=pl.BlockSpec((tq, HEAD_DIM), lambda b, h, i, k: (b * n_qt + i, h)),
        out_shape=jax.ShapeDtypeStruct((batch * seq, W_B), jnp.bfloat16),
        scratch_shapes=[
            pltpu.VMEM((tq, 1), jnp.float32),
            pltpu.VMEM((tq, 1), jnp.float32),
            pltpu.VMEM((tq, HEAD_DIM), jnp.float32),
        ],
        compiler_params=_cparams(("parallel", "parallel", "parallel", "arbitrary")),
        name="fox_attention",
    )(qk, qk, proj, wf, f_row)


def _layer(x, p, pos_col, batch, seq, g_attn, w_in, g_cq, w_uq, w_uq_idx, g_kidx, b_kidx, g_q_a, g_k_a, b_forget,
           g_q_b, g_k_b, w_up_a, w_up_b, w_o, g_ffn, w_ffn_gate, w_ffn_up, w_ffn_down, g_ple, w_ple, w_ple_gate):
    d = x.shape[1]
    bf = jnp.bfloat16
    top_k = min(INDEX_TOPK, seq // 4)

    o_kidx = Q_RANK + 2 * KV_A_W
    o_qb = o_kidx + IDX_DIM + N_IDX_HEADS
    o_fb = o_qb + 3 * W_B
    o_ga = o_fb + N_HEADS_B
    w_main = jnp.concatenate([w_in[:, :o_kidx], w_in[:, o_qb:o_fb], w_in[:, o_ga:]], axis=1).astype(bf)
    w_misc = jnp.concatenate(
        [w_in[:, o_kidx:o_qb], w_in[:, o_fb:o_ga], jnp.zeros((d, LANES - (o_qb - o_kidx) - N_HEADS_B), w_in.dtype)],
        axis=1).astype(bf)
    col_ka, col_va = Q_RANK, Q_RANK + KV_A_W
    col_qb = Q_RANK + 2 * KV_A_W
    col_vb = col_qb + 2 * W_B
    col_ga = col_vb + W_B
    col_gb = col_ga + d
    w_qidx_pad = jnp.pad(w_uq_idx.reshape(Q_RANK, N_IDX_HEADS, IDX_DIM), ((0, 0), (0, 0), (0, LANES - IDX_DIM)))
    w_qidx_pad = w_qidx_pad.reshape(Q_RANK, N_IDX_HEADS * LANES).astype(bf)
    d_ff = w_ffn_gate.shape[1]
    ff_tile = 512
    d_ff_pad = -(-d_ff // ff_tile) * ff_tile
    pad_cols = lambda w: jnp.pad(w, ((0, 0), (0, d_ff_pad - d_ff))).astype(bf)
    w_g, w_u = pad_cols(w_ffn_gate), pad_cols(w_ffn_up)
    w_d = jnp.pad(w_ffn_down, ((0, d_ff_pad - d_ff), (0, 0))).astype(bf)

    tabs = _rope_tables(pos_col)
    cf, sf = tabs[0], tabs[1]

    h = _rmsnorm(x, g_attn)
    proj = _matmul(h, w_main, bf, name="proj_main")
    misc = _matmul(h, w_misc, jnp.float32, tn=LANES, name="proj_misc")

    q_a, q_idx = _q_proj(proj, g_cq, w_uq.astype(bf), w_qidx_pad, g_q_a, tabs)
    k_a = _ka_norm_rope(proj, col_ka, g_k_a, cf, sf)
    k_idx, wf, f_row = _misc_post(misc, g_kidx, b_kidx, b_forget, tabs, seq)
    bias = _indexer_mask(q_idx, k_idx, wf, batch, seq, top_k)
    o_a = _dsa_attention(q_a, k_a, proj, col_va, bias, batch, seq)

    qk_b = _qkb_norm(proj, col_qb, jnp.concatenate([jnp.tile(g_q_b, N_HEADS_B), jnp.tile(g_k_b, N_HEADS_B)]))
    o_b = _fox_attention(qk_b, proj, col_vb, wf, f_row, batch, seq)

    merged = _merge(o_a, o_b, w_up_a.astype(bf), w_up_b.astype(bf), proj, col_ga, col_gb)
    x = _matmul_residual(merged, w_o.astype(bf), x, name="out_proj")

    h = _rmsnorm(x, g_ffn)
    u = _swiglu(h, w_g, w_u, tn=ff_tile)
    x = _matmul_residual_ksplit(u, w_d, x, name="ffn_down")

    h = _rmsnorm(x, g_ple)
    x = _ple(h, w_ple_gate.astype(bf), p.astype(bf), w_ple.astype(bf), x)
    return x


def kernel(x, p, positions, g_attn, w_in, g_cq, w_uq, w_uq_idx, g_kidx, b_kidx, g_q_a, g_k_a, b_forget, g_q_b, g_k_b,
           w_up_a, w_up_b, w_o, g_ffn, w_ffn_gate, w_ffn_up, w_ffn_down, g_ple, w_ple, w_ple_gate):
    batch, seq, d = x.shape
    depth = w_in.shape[0]
    xf = x.reshape(batch * seq, d)
    pos_col = positions.reshape(batch * seq, 1)
    for i in range(depth):
        xf = _layer(xf, p[i].reshape(batch * seq, -1), pos_col, batch, seq, g_attn[i], w_in[i], g_cq[i], w_uq[i],
                    w_uq_idx[i], g_kidx[i], b_kidx[i], g_q_a[i], g_k_a[i], b_forget[i], g_q_b[i], g_k_b[i],
                    w_up_a[i], w_up_b[i], w_o[i], g_ffn[i], w_ffn_gate[i], w_ffn_up[i], w_ffn_down[i], g_ple[i],
                    w_ple[i], w_ple_gate[i])
    return xf.reshape(batch, seq, d)
```

```python
import functools

import numpy as np
import jax
import jax.numpy as jnp
from jax import lax
from jax.experimental import pallas as pl
from jax.experimental.pallas import tpu as pltpu

CHUNK = 64
CHUNK_SHIFT = CHUNK.bit_length() - 1
HEAD_DIM = 128
ROPE_THETA = 10000.0
EPS = 1e-6
N_HEADS_A = 16
N_KV_A = 2
Q_RANK = 1024
N_IDX_HEADS = 32
IDX_DIM = 64
IDX_ROPE_DIM = 32
INDEX_TOPK = 256
N_HEADS_B = 16
PLE_DIM = 256

LANES = 128
KV_A_W = N_KV_A * HEAD_DIM
W_A = N_HEADS_A * HEAD_DIM
W_B = N_HEADS_B * HEAD_DIM
GROUP_A = N_HEADS_A // N_KV_A

MISC_W_LANE = IDX_DIM
MISC_F_LANE = IDX_DIM + N_IDX_HEADS

MASK_NEG = -1e30
SCORE_NEG = -3e38
SELECT_ITERS = 32
LOG2E = 1.4426950408889634
Q_SCALE = HEAD_DIM ** -0.5 * LOG2E
KV_TILE = 512
VMEM_LIMIT = 56 * 1024 * 1024


def _cparams(sem):
    return pltpu.CompilerParams(dimension_semantics=sem, vmem_limit_bytes=VMEM_LIMIT)


def _sigmoid(x):
    return 1.0 / (1.0 + jnp.exp(-x))


def _nt_dot(a, b):
    return lax.dot_general(a, b, (((1,), (1,)), ((), ())), preferred_element_type=jnp.float32)


def _rmsnorm_kernel(x_ref, g_ref, o_ref):
    x = x_ref[...]
    ms = jnp.mean(x * x, axis=-1, keepdims=True)
    o_ref[...] = (x * lax.rsqrt(ms + EPS) * g_ref[...]).astype(o_ref.dtype)


def _rmsnorm(x, g, tm=256):
    t, d = x.shape
    tm = min(tm, t)
    return pl.pallas_call(
        _rmsnorm_kernel,
        grid=(t // tm,),
        in_specs=[pl.BlockSpec((tm, d), lambda i: (i, 0)), pl.BlockSpec((1, d), lambda i: (0, 0))],
        out_specs=pl.BlockSpec((tm, d), lambda i: (i, 0)),
        out_shape=jax.ShapeDtypeStruct((t, d), jnp.bfloat16),
        compiler_params=_cparams(("parallel",)),
        name="rmsnorm",
    )(x, g.reshape(1, d))


def _mm_kernel(a_ref, w_ref, o_ref):
    o_ref[...] = jnp.dot(a_ref[...], w_ref[...], preferred_element_type=jnp.float32).astype(o_ref.dtype)


def _matmul(a, w, out_dtype, tm=1024, tn=512, name="matmul"):
    t, k = a.shape
    n = w.shape[1]
    tm, tn = min(tm, t), min(tn, n)
    return pl.pallas_call(
        _mm_kernel,
        grid=(t // tm, n // tn),
        in_specs=[pl.BlockSpec((tm, k), lambda i, j: (i, 0)), pl.BlockSpec((k, tn), lambda i, j: (0, j))],
        out_specs=pl.BlockSpec((tm, tn), lambda i, j: (i, j)),
        out_shape=jax.ShapeDtypeStruct((t, n), out_dtype),
        compiler_params=_cparams(("parallel", "parallel")),
        name=name,
    )(a, w)


def _mm_res_kernel(a_ref, w_ref, r_ref, o_ref):
    o_ref[...] = r_ref[...] + jnp.dot(a_ref[...], w_ref[...], preferred_element_type=jnp.float32)


def _matmul_residual(a, w, r, tm=1024, tn=512, name="matmul_residual"):
    t, k = a.shape
    n = w.shape[1]
    tm, tn = min(tm, t), min(tn, n)
    return pl.pallas_call(
        _mm_res_kernel,
        grid=(t // tm, n // tn),
        in_specs=[
            pl.BlockSpec((tm, k), lambda i, j: (i, 0)),
            pl.BlockSpec((k, tn), lambda i, j: (0, j)),
            pl.BlockSpec((tm, tn), lambda i, j: (i, j)),
        ],
        out_specs=pl.BlockSpec((tm, tn), lambda i, j: (i, j)),
        out_shape=jax.ShapeDtypeStruct((t, n), jnp.float32),
        compiler_params=_cparams(("parallel", "parallel")),
        name=name,
    )(a, w, r)


def _mm_res_acc_kernel(a_ref, w_ref, r_ref, o_ref, acc_ref):
    kk = pl.program_id(2)

    @pl.when(kk == 0)
    def _():
        acc_ref[...] = r_ref[...]

    acc_ref[...] += jnp.dot(a_ref[...], w_ref[...], preferred_element_type=jnp.float32)

    @pl.when(kk == pl.num_programs(2) - 1)
    def _():
        o_ref[...] = acc_ref[...]


def _matmul_residual_ksplit(a, w, r, tm=1024, tn=1024, tk=1024, name="matmul_residual_ksplit"):
    t, k = a.shape
    n = w.shape[1]
    tm, tn, tk = min(tm, t), min(tn, n), min(tk, k)
    return pl.pallas_call(
        _mm_res_acc_kernel,
        grid=(t // tm, n // tn, k // tk),
        in_specs=[
            pl.BlockSpec((tm, tk), lambda i, j, kk: (i, kk)),
            pl.BlockSpec((tk, tn), lambda i, j, kk: (kk, j)),
            pl.BlockSpec((tm, tn), lambda i, j, kk: (i, j)),
        ],
        out_specs=pl.BlockSpec((tm, tn), lambda i, j, kk: (i, j)),
        out_shape=jax.ShapeDtypeStruct((t, n), jnp.float32),
        scratch_shapes=[pltpu.VMEM((tm, tn), jnp.float32)],
        compiler_params=_cparams(("parallel", "parallel", "arbitrary")),
        name=name,
    )(a, w, r)


def _swiglu_kernel(a_ref, wg_ref, wu_ref, o_ref):
    a = a_ref[...]
    g = jnp.dot(a, wg_ref[...], preferred_element_type=jnp.float32)
    u = jnp.dot(a, wu_ref[...], preferred_element_type=jnp.float32)
    o_ref[...] = (g * _sigmoid(g) * u).astype(o_ref.dtype)


def _swiglu(a, wg, wu, tm=1024, tn=512):
    t, k = a.shape
    n = wg.shape[1]
    tm, tn = min(tm, t), min(tn, n)
    return pl.pallas_call(
        _swiglu_kernel,
        grid=(t // tm, n // tn),
        in_specs=[
            pl.BlockSpec((tm, k), lambda i, j: (i, 0)),
            pl.BlockSpec((k, tn), lambda i, j: (0, j)),
            pl.BlockSpec((k, tn), lambda i, j: (0, j)),
        ],
        out_specs=pl.BlockSpec((tm, tn), lambda i, j: (i, j)),
        out_shape=jax.ShapeDtypeStruct((t, n), jnp.bfloat16),
        compiler_params=_cparams(("parallel", "parallel")),
        name="swiglu",
    )(a, wg, wu)


def _merge_kernel(oa_ref, ob_ref, wa_ref, wb_ref, ga_ref, gb_ref, o_ref):
    a = jnp.dot(oa_ref[...], wa_ref[...], preferred_element_type=jnp.float32)
    b = jnp.dot(ob_ref[...], wb_ref[...], preferred_element_type=jnp.float32)
    ga = _sigmoid(ga_ref[...].astype(jnp.float32))
    gb = _sigmoid(gb_ref[...].astype(jnp.float32))
    o_ref[...] = (ga * a + gb * b).astype(o_ref.dtype)


def _merge(o_a, o_b, w_up_a, w_up_b, proj, ga_col, gb_col, tm=1024, tn=512):
    t, ka = o_a.shape
    kb = o_b.shape[1]
    n = w_up_a.shape[1]
    tm, tn = min(tm, t), min(tn, n)
    ga_blk, gb_blk = ga_col // tn, gb_col // tn
    return pl.pallas_call(
        _merge_kernel,
        grid=(t // tm, n // tn),
        in_specs=[
            pl.BlockSpec((tm, ka), lambda i, j: (i, 0)),
            pl.BlockSpec((tm, kb), lambda i, j: (i, 0)),
            pl.BlockSpec((ka, tn), lambda i, j: (0, j)),
            pl.BlockSpec((kb, tn), lambda i, j: (0, j)),
            pl.BlockSpec((tm, tn), lambda i, j: (i, ga_blk + j)),
            pl.BlockSpec((tm, tn), lambda i, j: (i, gb_blk + j)),
        ],
        out_specs=pl.BlockSpec((tm, tn), lambda i, j: (i, j)),
        out_shape=jax.ShapeDtypeStruct((t, n), jnp.bfloat16),
        compiler_params=_cparams(("parallel", "parallel")),
        name="merge",
    )(o_a, o_b, w_up_a, w_up_b, proj, proj)


def _ple_kernel(h_ref, wg_ref, p_ref, wp_ref, r_ref, o_ref):
    g = jnp.dot(h_ref[...], wg_ref[...], preferred_element_type=jnp.float32)
    e = jnp.dot(p_ref[...], wp_ref[...], preferred_element_type=jnp.float32)
    o_ref[...] = r_ref[...] + _sigmoid(g) * e


def _ple(h, w_gate, p, w_ple, r, tm=1024, tn=512):
    t, k = h.shape
    kp = p.shape[1]
    n = w_gate.shape[1]
    tm, tn = min(tm, t), min(tn, n)
    return pl.pallas_call(
        _ple_kernel,
        grid=(t // tm, n // tn),
        in_specs=[
            pl.BlockSpec((tm, k), lambda i, j: (i, 0)),
            pl.BlockSpec((k, tn), lambda i, j: (0, j)),
            pl.BlockSpec((tm, kp), lambda i, j: (i, 0)),
            pl.BlockSpec((kp, tn), lambda i, j: (0, j)),
            pl.BlockSpec((tm, tn), lambda i, j: (i, j)),
        ],
        out_specs=pl.BlockSpec((tm, tn), lambda i, j: (i, j)),
        out_shape=jax.ShapeDtypeStruct((t, n), jnp.float32),
        compiler_params=_cparams(("parallel", "parallel")),
        name="ple",
    )(h, w_gate, p, w_ple, r)


def _rope_consts():
    half = HEAD_DIM // 2
    inv_full = jnp.power(ROPE_THETA, -jnp.arange(half, dtype=jnp.float32) * (2.0 / HEAD_DIM))
    half_i = IDX_ROPE_DIM // 2
    inv_idx = jnp.power(ROPE_THETA, -jnp.arange(half_i, dtype=jnp.float32) * (2.0 / IDX_ROPE_DIM))
    zeros = jnp.zeros((LANES - IDX_ROPE_DIM,), jnp.float32)
    sign = np.concatenate([-np.ones(half, np.float32), np.ones(half, np.float32)])
    mask_a = np.zeros(LANES, np.float32)
    mask_a[:half_i] = -1.0
    mask_b = np.zeros(LANES, np.float32)
    mask_b[half_i:IDX_ROPE_DIM] = 1.0
    rows = [
        jnp.concatenate([inv_full, inv_full]),
        jnp.asarray(sign),
        jnp.concatenate([inv_idx, inv_idx, zeros]),
        jnp.asarray(mask_a),
        jnp.asarray(mask_b),
    ]
    rows += [jnp.zeros((LANES,), jnp.float32)] * 3
    return jnp.stack(rows)


def _rope_tables_kernel(pos_ref, c_ref, cf_ref, sf_ref, ci_ref, sa_ref, sb_ref):
    pos = pos_ref[...].astype(jnp.float32)
    ang = pos * c_ref[0:1, :]
    cf_ref[...] = jnp.cos(ang)
    sf_ref[...] = jnp.sin(ang) * c_ref[1:2, :]
    ang_i = pos * c_ref[2:3, :]
    ci_ref[...] = jnp.cos(ang_i)
    s_i = jnp.sin(ang_i)
    sa_ref[...] = s_i * c_ref[3:4, :]
    sb_ref[...] = s_i * c_ref[4:5, :]


def _rope_tables(pos_col, tm=512):
    t = pos_col.shape[0]
    tm = min(tm, t)
    tab = jax.ShapeDtypeStruct((t, LANES), jnp.float32)
    spec = pl.BlockSpec((tm, LANES), lambda i: (i, 0))
    return pl.pallas_call(
        _rope_tables_kernel,
        grid=(t // tm,),
        in_specs=[pl.BlockSpec((tm, 1), lambda i: (i, 0)), pl.BlockSpec((8, LANES), lambda i: (0, 0))],
        out_specs=[spec] * 5,
        out_shape=[tab] * 5,
        compiler_params=_cparams(("parallel",)),
        name="rope_tables",
    )(pos_col, _rope_consts())


def _rope_full(x, cf, sf):
    return x * cf + pltpu.roll(x, HEAD_DIM // 2, axis=1) * sf


def _rope_idx(x, ci, sa, sb):
    half = IDX_ROPE_DIM // 2
    return x * ci + pltpu.roll(x, LANES - half, axis=1) * sa + pltpu.roll(x, half, axis=1) * sb


def _head_rms(x, g):
    ms = jnp.mean(x * x, axis=-1, keepdims=True)
    return x * lax.rsqrt(ms + EPS) * g


def _q_proj_kernel(cq_ref, gcq_ref, wa_ref, wi_ref, gqa_ref, cf_ref, sf_ref, ci_ref, sa_ref, sb_ref, qa_ref, qi_ref):
    cq = cq_ref[...].astype(jnp.float32)
    ms = jnp.mean(cq * cq, axis=-1, keepdims=True)
    cq = (cq * lax.rsqrt(ms + EPS) * gcq_ref[...]).astype(jnp.bfloat16)
    cf, sf = cf_ref[...], sf_ref[...]
    gqa = gqa_ref[...]
    for j in range(N_HEADS_A // 2):
        acc = jnp.dot(cq, wa_ref[:, j * 2 * HEAD_DIM:(j + 1) * 2 * HEAD_DIM], preferred_element_type=jnp.float32)
        for e in range(2):
            hq = _head_rms(acc[:, e * HEAD_DIM:(e + 1) * HEAD_DIM], gqa)
            qa_ref[2 * j + e] = (_rope_full(hq, cf, sf) * Q_SCALE).astype(qa_ref.dtype)
    ci, sa, sb = ci_ref[...], sa_ref[...], sb_ref[...]
    for j in range(N_IDX_HEADS // 2):
        acc = jnp.dot(cq, wi_ref[:, j * 2 * LANES:(j + 1) * 2 * LANES], preferred_element_type=jnp.float32)
        for e in range(2):
            qi_ref[2 * j + e] = _rope_idx(acc[:, e * LANES:(e + 1) * LANES], ci, sa, sb).astype(qi_ref.dtype)


def _q_proj(proj, g_cq, w_uq, w_uq_idx_pad, g_q_a, tabs, tm=512):
    t = proj.shape[0]
    tm = min(tm, t)
    cf, sf, ci, sa, sb = tabs
    tab_spec = pl.BlockSpec((tm, LANES), lambda i: (i, 0))
    return pl.pallas_call(
        _q_proj_kernel,
        grid=(t // tm,),
        in_specs=[
            pl.BlockSpec((tm, Q_RANK), lambda i: (i, 0)),
            pl.BlockSpec((1, Q_RANK), lambda i: (0, 0)),
            pl.BlockSpec((Q_RANK, W_A), lambda i: (0, 0)),
            pl.BlockSpec((Q_RANK, N_IDX_HEADS * LANES), lambda i: (0, 0)),
            pl.BlockSpec((1, HEAD_DIM), lambda i: (0, 0)),
            tab_spec, tab_spec, tab_spec, tab_spec, tab_spec,
        ],
        out_specs=[
            pl.BlockSpec((N_HEADS_A, tm, HEAD_DIM), lambda i: (0, i, 0)),
            pl.BlockSpec((N_IDX_HEADS, tm, LANES), lambda i: (0, i, 0)),
        ],
        out_shape=[
            jax.ShapeDtypeStruct((N_HEADS_A, t, HEAD_DIM), jnp.bfloat16),
            jax.ShapeDtypeStruct((N_IDX_HEADS, t, LANES), jnp.bfloat16),
        ],
        compiler_params=_cparams(("parallel",)),
        name="q_proj",
    )(proj, g_cq.reshape(1, Q_RANK), w_uq, w_uq_idx_pad, g_q_a.reshape(1, HEAD_DIM), cf, sf, ci, sa, sb)


def _ka_kernel(x_ref, g_ref, cf_ref, sf_ref, o_ref):
    x = x_ref[...].astype(jnp.float32)
    g, cf, sf = g_ref[...], cf_ref[...], sf_ref[...]
    for h in range(N_KV_A):
        hk = _head_rms(x[:, h * HEAD_DIM:(h + 1) * HEAD_DIM], g)
        o_ref[:, h * HEAD_DIM:(h + 1) * HEAD_DIM] = _rope_full(hk, cf, sf).astype(o_ref.dtype)


def _ka_norm_rope(proj, col, g_k_a, cf, sf, tm=512):
    t = proj.shape[0]
    tm = min(tm, t)
    tab_spec = pl.BlockSpec((tm, LANES), lambda i: (i, 0))
    return pl.pallas_call(
        _ka_kernel,
        grid=(t // tm,),
        in_specs=[
            pl.BlockSpec((tm, KV_A_W), lambda i: (i, col // KV_A_W)),
            pl.BlockSpec((1, HEAD_DIM), lambda i: (0, 0)),
            tab_spec, tab_spec,
        ],
        out_specs=pl.BlockSpec((tm, KV_A_W), lambda i: (i, 0)),
        out_shape=jax.ShapeDtypeStruct((t, KV_A_W), jnp.bfloat16),
        compiler_params=_cparams(("parallel",)),
        name="ka_norm_rope",
    )(proj, g_k_a.reshape(1, HEAD_DIM), cf, sf)


def _headnorm_kernel(x_ref, g_ref, o_ref, *, heads):
    x = x_ref[...].astype(jnp.float32)
    g = g_ref[...]
    for h in range(heads):
        sl = slice(h * HEAD_DIM, (h + 1) * HEAD_DIM)
        o_ref[:, sl] = _head_rms(x[:, sl], g[:, sl]).astype(o_ref.dtype)


def _qkb_norm(proj, col, gains, tm=512, tn=512):
    t = proj.shape[0]
    n = gains.shape[0]
    tm = min(tm, t)
    blk0 = col // tn
    return pl.pallas_call(
        functools.partial(_headnorm_kernel, heads=tn // HEAD_DIM),
        grid=(t // tm, n // tn),
        in_specs=[
            pl.BlockSpec((tm, tn), lambda i, j: (i, blk0 + j)),
            pl.BlockSpec((1, tn), lambda i, j: (0, j)),
        ],
        out_specs=pl.BlockSpec((tm, tn), lambda i, j: (i, j)),
        out_shape=jax.ShapeDtypeStruct((t, n), jnp.bfloat16),
        compiler_params=_cparams(("parallel", "parallel")),
        name="qkb_norm",
    )(proj, gains.reshape(1, n))


def _split3_bf16(x):
    hi = x.astype(jnp.bfloat16)
    r1 = x - hi.astype(jnp.float32)
    mid = r1.astype(jnp.bfloat16)
    lo = (r1 - mid.astype(jnp.float32)).astype(jnp.bfloat16)
    return hi, mid, lo


def _misc_kernel(m_ref, c_ref, ci_ref, sa_ref, sb_ref, tri_ref, kidx_ref, wf_ref, wt_ref, frow_ref, carry_ref, *,
                 tiles_per_seq):
    i = pl.program_id(0)

    @pl.when(i % tiles_per_seq == 0)
    def _():
        carry_ref[...] = jnp.zeros_like(carry_ref)

    x = m_ref[...]
    lane = lax.broadcasted_iota(jnp.int32, x.shape, 1)
    is_k = lane < IDX_DIM
    xk = jnp.where(is_k, x, 0.0)
    mu = jnp.sum(xk, axis=-1, keepdims=True) * (1.0 / IDX_DIM)
    dk = jnp.where(is_k, x - mu, 0.0)
    var = jnp.sum(dk * dk, axis=-1, keepdims=True) * (1.0 / IDX_DIM)
    y = dk * lax.rsqrt(var + EPS) * c_ref[0:1, :] + c_ref[1:2, :]
    y = _rope_idx(y, ci_ref[...], sa_ref[...], sb_ref[...])
    kidx_ref[...] = jnp.where(is_k, y, 0.0).astype(kidx_ref.dtype)
    f = x + c_ref[2:3, :]
    log_f = jnp.minimum(f, 0.0) - jnp.log1p(jnp.exp(-jnp.abs(f)))
    hi, mid, lo = _split3_bf16(log_f)
    tri = tri_ref[...]
    csum = (jnp.dot(tri, hi, preferred_element_type=jnp.float32)
            + jnp.dot(tri, mid, preferred_element_type=jnp.float32)
            + jnp.dot(tri, lo, preferred_element_type=jnp.float32))
    csum = csum + carry_ref[0:1, :]
    carry_ref[...] = jnp.broadcast_to(csum[-1:, :], carry_ref.shape)
    is_w = (lane >= MISC_W_LANE) & (lane < MISC_F_LANE)
    wf = jnp.where(is_w, x * c_ref[3:4, :], csum * LOG2E)
    wf_ref[...] = wf
    wf_t = wf.T
    wt_ref[...] = wf_t[MISC_W_LANE:MISC_F_LANE, :]
    frow_ref[0] = wf_t[MISC_F_LANE:MISC_F_LANE + N_HEADS_B, :]


def _misc_post(misc, g_kidx, b_kidx, b_forget, tabs, seq):
    t = misc.shape[0]
    tm = min(KV_TILE, seq)
    idx_w_scale = (N_IDX_HEADS ** -0.5) * (IDX_DIM ** -0.5)
    pad = lambda v, off: jnp.zeros((LANES,), jnp.float32).at[off:off + v.shape[0]].set(v)
    consts = jnp.stack([
        pad(g_kidx, 0), pad(b_kidx, 0), pad(b_forget, MISC_F_LANE),
        pad(jnp.full((N_IDX_HEADS,), idx_w_scale, jnp.float32), MISC_W_LANE),
    ] + [jnp.zeros((LANES,), jnp.float32)] * 4)
    tri = jnp.asarray(np.tril(np.ones((tm, tm), np.float32)), jnp.bfloat16)
    _, _, ci, sa, sb = tabs
    tab_spec = pl.BlockSpec((tm, LANES), lambda i: (i, 0))
    return pl.pallas_call(
        functools.partial(_misc_kernel, tiles_per_seq=seq // tm),
        grid=(t // tm,),
        in_specs=[
            tab_spec,
            pl.BlockSpec((8, LANES), lambda i: (0, 0)),
            tab_spec, tab_spec, tab_spec,
            pl.BlockSpec((tm, tm), lambda i: (0, 0)),
        ],
        out_specs=[
            tab_spec, tab_spec,
            pl.BlockSpec((N_IDX_HEADS, tm), lambda i: (0, i)),
            pl.BlockSpec((1, N_HEADS_B, tm), lambda i: (i, 0, 0)),
        ],
        out_shape=[
            jax.ShapeDtypeStruct((t, LANES), jnp.bfloat16),
            jax.ShapeDtypeStruct((t, LANES), jnp.float32),
            jax.ShapeDtypeStruct((N_IDX_HEADS, t), jnp.float32),
            jax.ShapeDtypeStruct((t // tm, N_HEADS_B, tm), jnp.float32),
        ],
        scratch_shapes=[pltpu.VMEM((8, LANES), jnp.float32)],
        compiler_params=_cparams(("arbitrary",)),
        name="misc_post",
    )(misc, consts, ci, sa, sb, tri)


IDX_HEAD_GROUP = 8
IDX_SLAB = 128
COUNT_ROWS = 64


def _indexer_kernel(q_ref, k_ref, w_ref, o_ref, sc_ref, *, tq, tk, n_kt, top_k):
    qi = pl.program_id(1)
    t0 = qi * tq
    n_vis = (t0 + tq + tk - 1) // tk
    t_row = lax.broadcasted_iota(jnp.int32, (1, tq), 1) + t0
    vis_end = (jnp.right_shift(t_row, CHUNK_SHIFT) + 1) * CHUNK

    def score_tile(c, carry):
        mx, mn = carry
        kt = k_ref[pl.ds(pl.multiple_of(c * tk, tk), tk), :]
        for g in range(N_IDX_HEADS // IDX_HEAD_GROUP):
            qg = q_ref[g * IDX_HEAD_GROUP:(g + 1) * IDX_HEAD_GROUP].reshape(IDX_HEAD_GROUP * tq, LANES)
            st = _nt_dot(kt, qg)
            for r in range(tk // IDX_SLAB):
                rows = slice(r * IDX_SLAB, (r + 1) * IDX_SLAB)
                part = jnp.zeros((IDX_SLAB, tq), jnp.float32)
                for j in range(IDX_HEAD_GROUP):
                    h = g * IDX_HEAD_GROUP + j
                    part = part + jnp.maximum(st[rows, j * tq:(j + 1) * tq], 0.0) * w_ref[h:h + 1, :]
                if g == 0:
                    sc_ref[c, rows, :] = part
                else:
                    sc_ref[c, rows, :] += part
        s_col = lax.broadcasted_iota(jnp.int32, (tk, 1), 0) + c * tk
        adm = s_col < vis_end
        sc = sc_ref[c]
        sc_ref[c] = jnp.where(adm, sc, SCORE_NEG)
        mx = jnp.maximum(mx, jnp.max(jnp.where(adm, sc, SCORE_NEG), axis=0, keepdims=True))
        mn = jnp.minimum(mn, jnp.min(jnp.where(adm, sc, -SCORE_NEG), axis=0, keepdims=True))
        return mx, mn

    hi0, lo0 = lax.fori_loop(
        0, n_vis, score_tile,
        (jnp.full((1, tq), SCORE_NEG, jnp.float32), jnp.full((1, tq), -SCORE_NEG, jnp.float32)))

    def bisect(_, lohi):
        lo, hi = lohi
        mid = 0.5 * (lo + hi)

        def count_tile(c, cnt):
            ge = jnp.where(sc_ref[c] >= mid, 1.0, 0.0)
            return cnt + jnp.sum(ge.reshape(tk // COUNT_ROWS, COUNT_ROWS, tq), axis=0)

        cnt = lax.fori_loop(0, n_vis, count_tile, jnp.zeros((COUNT_ROWS, tq), jnp.float32))
        enough = jnp.sum(cnt, axis=0, keepdims=True) >= float(top_k)
        return jnp.where(enough, mid, lo), jnp.where(enough, hi, mid)

    thr, _ = lax.fori_loop(0, SELECT_ITERS, bisect, (lo0, hi0))

    def write_tile(c, carry):
        sel = jnp.where(sc_ref[c] >= thr, 0.0, MASK_NEG)
        o_ref[0, c] = sel.T.astype(o_ref.dtype)
        return carry

    lax.fori_loop(0, n_vis, write_tile, 0)

    def fill_tile(c, carry):
        o_ref[0, c] = jnp.full((tq, tk), MASK_NEG, o_ref.dtype)
        return carry

    lax.fori_loop(n_vis, n_kt, fill_tile, 0)


def _indexer_mask(q_idx, k_idx, w_t, batch, seq, top_k, tq=256):
    tq, tk = min(tq, seq), min(KV_TILE, seq)
    n_qt, n_kt = seq // tq, seq // tk
    return pl.pallas_call(
        functools.partial(_indexer_kernel, tq=tq, tk=tk, n_kt=n_kt, top_k=top_k),
        grid=(batch, n_qt),
        in_specs=[
            pl.BlockSpec((N_IDX_HEADS, tq, LANES), lambda b, i: (0, b * n_qt + i, 0)),
            pl.BlockSpec((seq, LANES), lambda b, i: (b, 0)),
            pl.BlockSpec((N_IDX_HEADS, tq), lambda b, i: (0, b * n_qt + i)),
        ],
        out_specs=pl.BlockSpec((1, n_kt, tq, tk), lambda b, i: (b, 0, i, 0)),
        out_shape=jax.ShapeDtypeStruct((batch, n_kt, seq, tk), jnp.bfloat16),
        scratch_shapes=[pltpu.VMEM((n_kt, tk, tq), jnp.float32)],
        compiler_params=_cparams(("parallel", "parallel")),
        name="indexer_mask",
    )(q_idx, k_idx, w_t)


def _softmax_update(s, row_shift, v, m_ref, l_ref, acc_ref, e):
    rows, tk = s.shape
    chunks = [s[:, c * LANES:(c + 1) * LANES] for c in range(tk // LANES)]
    mx = functools.reduce(jnp.maximum, chunks)
    mx = jnp.broadcast_to(jnp.max(mx, axis=-1, keepdims=True), (rows, LANES))
    m_prev = m_ref[e]
    m_new = jnp.maximum(m_prev, mx + row_shift)
    alpha = jnp.exp2(m_prev - m_new)
    r = m_new - row_shift
    p = jnp.concatenate([jnp.exp2(c - r) for c in chunks], axis=1).astype(v.dtype)
    v_ones = jnp.concatenate([v, jnp.ones((tk, LANES), v.dtype)], axis=1)
    pv = jnp.dot(p, v_ones, preferred_element_type=jnp.float32)
    acc_ref[e] = alpha * acc_ref[e] + pv[:, :HEAD_DIM]
    l_ref[e] = alpha * l_ref[e] + pv[:, HEAD_DIM:]
    m_ref[e] = m_new


def _softmax_init(m_ref, l_ref, acc_ref):
    m_ref[...] = jnp.full(m_ref.shape, MASK_NEG, jnp.float32)
    l_ref[...] = jnp.zeros(l_ref.shape, jnp.float32)
    acc_ref[...] = jnp.zeros(acc_ref.shape, jnp.float32)


def _dsa_attn_kernel(q_ref, k_ref, v_ref, b_ref, o_ref, m_ref, l_ref, acc_ref, *, tq, tk):
    qi = pl.program_id(1)
    n_need = ((qi + 1) * tq + tk - 1) // tk
    rows = GROUP_A * tq
    _softmax_init(m_ref, l_ref, acc_ref)

    def kv_step(j, carry):
        off = pl.multiple_of(j * tk, tk)
        bias = b_ref[0, j].astype(jnp.float32)[None]
        for g in range(N_KV_A):
            cols = slice(g * HEAD_DIM, (g + 1) * HEAD_DIM)
            q = q_ref[g * GROUP_A:(g + 1) * GROUP_A].reshape(rows, HEAD_DIM)
            s = _nt_dot(q, k_ref[pl.ds(off, tk), cols])
            s = (s.reshape(GROUP_A, tq, tk) + bias).reshape(rows, tk)
            _softmax_update(s, 0.0, v_ref[pl.ds(off, tk), cols], m_ref, l_ref, acc_ref, g)
        return carry

    lax.fori_loop(0, n_need, kv_step, 0)

    for g in range(N_KV_A):
        out = acc_ref[g] / l_ref[g]
        for h in range(GROUP_A):
            c0 = (g * GROUP_A + h) * HEAD_DIM
            o_ref[:, c0:c0 + HEAD_DIM] = out[h * tq:(h + 1) * tq].astype(o_ref.dtype)


def _dsa_attention(q_a, k_a, proj, v_col, bias, batch, seq, tq=256):
    tk = bias.shape[-1]
    tq = min(tq, seq)
    n_qt, n_kt = seq // tq, seq // tk
    rows = GROUP_A * tq
    return pl.pallas_call(
        functools.partial(_dsa_attn_kernel, tq=tq, tk=tk),
        grid=(batch, n_qt),
        in_specs=[
            pl.BlockSpec((N_HEADS_A, tq, HEAD_DIM), lambda b, i: (0, b * n_qt + i, 0)),
            pl.BlockSpec((seq, KV_A_W), lambda b, i: (b, 0)),
            pl.BlockSpec((seq, KV_A_W), lambda b, i: (b, v_col // KV_A_W)),
            pl.BlockSpec((1, n_kt, tq, tk), lambda b, i: (b, 0, i, 0)),
        ],
        out_specs=pl.BlockSpec((tq, W_A), lambda b, i: (b * n_qt + i, 0)),
        out_shape=jax.ShapeDtypeStruct((batch * seq, W_A), jnp.bfloat16),
        scratch_shapes=[
            pltpu.VMEM((N_KV_A, rows, LANES), jnp.float32),
            pltpu.VMEM((N_KV_A, rows, LANES), jnp.float32),
            pltpu.VMEM((N_KV_A, rows, HEAD_DIM), jnp.float32),
        ],
        compiler_params=_cparams(("parallel", "parallel")),
        name="dsa_attention",
    )(q_a, k_a, proj, bias)


FOX_HEADS_PER_STEP = 2


def _fox_kernel(q_ref, k_ref, v_ref, fq_ref, fk_ref, o_ref, m_ref, l_ref, acc_ref, *, tile):
    hp, qi = pl.program_id(1), pl.program_id(2)
    lane = lax.broadcasted_iota(jnp.int32, (tile, LANES), 1)
    heads = [hp * FOX_HEADS_PER_STEP + e for e in range(FOX_HEADS_PER_STEP)]
    fq = [jnp.broadcast_to(jnp.sum(jnp.where(lane == MISC_F_LANE + h, fq_ref[...], 0.0), axis=1, keepdims=True),
                           (tile, LANES)) for h in heads]
    _softmax_init(m_ref, l_ref, acc_ref)

    def kv_step(j, diagonal):
        off = pl.multiple_of(j * tile, tile)
        for e, h in enumerate(heads):
            cols = slice(e * HEAD_DIM, (e + 1) * HEAD_DIM)
            s = _nt_dot(q_ref[:, cols], k_ref[pl.ds(off, tile), cols]) - fk_ref[j, pl.ds(h, 1), :]
            if diagonal:
                t_pos = lax.broadcasted_iota(jnp.int32, (tile, tile), 0)
                s_pos = lax.broadcasted_iota(jnp.int32, (tile, tile), 1)
                s = jnp.where(s_pos <= t_pos, s, MASK_NEG)
            _softmax_update(s, fq[e], v_ref[pl.ds(off, tile), cols], m_ref, l_ref, acc_ref, e)

    def full_step(j, carry):
        kv_step(j, False)
        return carry

    lax.fori_loop(0, qi, full_step, 0)
    kv_step(qi, True)

    for e in range(FOX_HEADS_PER_STEP):
        o_ref[:, e * HEAD_DIM:(e + 1) * HEAD_DIM] = (acc_ref[e] / l_ref[e]).astype(o_ref.dtype)


def _fox_attention(qk, proj, v_col, wf, f_row, batch, seq):
    tile = min(KV_TILE, seq)
    n_t = seq // tile
    wb = FOX_HEADS_PER_STEP * HEAD_DIM
    return pl.pallas_call(
        functools.partial(_fox_kernel, tile=tile),
        grid=(batch, N_HEADS_B // FOX_HEADS_PER_STEP, n_t),
        in_specs=[
            pl.BlockSpec((tile, wb), lambda b, h, i: (b * n_t + i, h)),
            pl.BlockSpec((seq, wb), lambda b, h, i: (b, W_B // wb + h)),
            pl.BlockSpec((seq, wb), lambda b, h, i: (b, v_col // wb + h)),
            pl.BlockSpec((tile, LANES), lambda b, h, i: (b * n_t + i, 0)),
            pl.BlockSpec((n_t, N_HEADS_B, tile), lambda b, h, i: (b, 0, 0)),
        ],
        out_specs=pl.BlockSpec((tile, wb), lambda b, h, i: (b * n_t + i, h)),
        out_shape=jax.ShapeDtypeStruct((batch * seq, W_B), jnp.bfloat16),
        scratch_shapes=[
            pltpu.VMEM((FOX_HEADS_PER_STEP, tile, LANES), jnp.float32),
            pltpu.VMEM((FOX_HEADS_PER_STEP, tile, LANES), jnp.float32),
            pltpu.VMEM((FOX_HEADS_PER_STEP, tile, HEAD_DIM), jnp.float32),
        ],
        compiler_params=_cparams(("parallel", "parallel", "parallel")),
        name="fox_attention",
    )(qk, qk, proj, wf, f_row)


def _layer(x, p, pos_col, batch, seq, g_attn, w_in, g_cq, w_uq, w_uq_idx, g_kidx, b_kidx, g_q_a, g_k_a, b_forget,
           g_q_b, g_k_b, w_up_a, w_up_b, w_o, g_ffn, w_ffn_gate, w_ffn_up, w_ffn_down, g_ple, w_ple, w_ple_gate):
    d = x.shape[1]
    bf = jnp.bfloat16
    top_k = min(INDEX_TOPK, seq // 4)

    o_kidx = Q_RANK + 2 * KV_A_W
    o_qb = o_kidx + IDX_DIM + N_IDX_HEADS
    o_fb = o_qb + 3 * W_B
    o_ga = o_fb + N_HEADS_B
    w_main = jnp.concatenate([w_in[:, :o_kidx], w_in[:, o_qb:o_fb], w_in[:, o_ga:]], axis=1).astype(bf)
    w_misc = jnp.concatenate(
        [w_in[:, o_kidx:o_qb], w_in[:, o_fb:o_ga], jnp.zeros((d, LANES - (o_qb - o_kidx) - N_HEADS_B), w_in.dtype)],
        axis=1).astype(bf)
    col_ka, col_va = Q_RANK, Q_RANK + KV_A_W
    col_qb = Q_RANK + 2 * KV_A_W
    col_vb = col_qb + 2 * W_B
    col_ga = col_vb + W_B
    col_gb = col_ga + d
    w_qidx_pad = jnp.pad(w_uq_idx.reshape(Q_RANK, N_IDX_HEADS, IDX_DIM), ((0, 0), (0, 0), (0, LANES - IDX_DIM)))
    w_qidx_pad = w_qidx_pad.reshape(Q_RANK, N_IDX_HEADS * LANES).astype(bf)
    d_ff = w_ffn_gate.shape[1]
    ff_tile = 512
    d_ff_pad = -(-d_ff // ff_tile) * ff_tile
    pad_cols = lambda w: jnp.pad(w, ((0, 0), (0, d_ff_pad - d_ff))).astype(bf)
    w_g, w_u = pad_cols(w_ffn_gate), pad_cols(w_ffn_up)
    w_d = jnp.pad(w_ffn_down, ((0, d_ff_pad - d_ff), (0, 0))).astype(bf)

    tabs = _rope_tables(pos_col)
    cf, sf = tabs[0], tabs[1]

    h = _rmsnorm(x, g_attn)
    proj = _matmul(h, w_main, bf, name="proj_main")
    misc = _matmul(h, w_misc, jnp.float32, tn=LANES, name="proj_misc")

    q_a, q_idx = _q_proj(proj, g_cq, w_uq.astype(bf), w_qidx_pad, g_q_a, tabs)
    k_a = _ka_norm_rope(proj, col_ka, g_k_a, cf, sf)
    k_idx, wf, w_t, f_row = _misc_post(misc, g_kidx, b_kidx, b_forget, tabs, seq)
    bias = _indexer_mask(q_idx, k_idx, w_t, batch, seq, top_k)
    o_a = _dsa_attention(q_a, k_a, proj, col_va, bias, batch, seq)

    gains_b = jnp.concatenate([jnp.tile(g_q_b * Q_SCALE, N_HEADS_B), jnp.tile(g_k_b, N_HEADS_B)])
    qk_b = _qkb_norm(proj, col_qb, gains_b)
    o_b = _fox_attention(qk_b, proj, col_vb, wf, f_row, batch, seq)

    merged = _merge(o_a, o_b, w_up_a.astype(bf), w_up_b.astype(bf), proj, col_ga, col_gb)
    x = _matmul_residual(merged, w_o.astype(bf), x, name="out_proj")

    h = _rmsnorm(x, g_ffn)
    u = _swiglu(h, w_g, w_u, tn=ff_tile)
    x = _matmul_residual_ksplit(u, w_d, x, name="ffn_down")

    h = _rmsnorm(x, g_ple)
    x = _ple(h, w_ple_gate.astype(bf), p.astype(bf), w_ple.astype(bf), x)
    return x


def kernel(x, p, positions, g_attn, w_in, g_cq, w_uq, w_uq_idx, g_kidx, b_kidx, g_q_a, g_k_a, b_forget, g_q_b, g_k_b,
           w_up_a, w_up_b, w_o, g_ffn, w_ffn_gate, w_ffn_up, w_ffn_down, g_ple, w_ple, w_ple_gate):
    batch, seq, d = x.shape
    depth = w_in.shape[0]
    xf = x.reshape(batch * seq, d)
    pos_col = positions.reshape(batch * seq, 1)
    for i in range(depth):
        xf = _layer(xf, p[i].reshape(batch * seq, -1), pos_col, batch, seq, g_attn[i], w_in[i], g_cq[i], w_uq[i],
                    w_uq_idx[i], g_kidx[i], b_kidx[i], g_q_a[i], g_k_a[i], b_forget[i], g_q_b[i], g_k_b[i],
                    w_up_a[i], w_up_b[i], w_o[i], g_ffn[i], w_ffn_gate[i], w_ffn_up[i], w_ffn_down[i], g_ple[i],
                    w_ple[i], w_ple_gate[i])
    return xf.reshape(batch, seq, d)
```

```python
import functools

import numpy as np
import jax
import jax.numpy as jnp
from jax import lax
from jax.experimental import pallas as pl
from jax.experimental.pallas import tpu as pltpu

CHUNK = 64
CHUNK_SHIFT = CHUNK.bit_length() - 1
HEAD_DIM = 128
ROPE_THETA = 10000.0
EPS = 1e-6
N_HEADS_A = 16
N_KV_A = 2
Q_RANK = 1024
N_IDX_HEADS = 32
IDX_DIM = 64
IDX_ROPE_DIM = 32
INDEX_TOPK = 256
N_HEADS_B = 16
PLE_DIM = 256

LANES = 128
KV_A_W = N_KV_A * HEAD_DIM
W_A = N_HEADS_A * HEAD_DIM
W_B = N_HEADS_B * HEAD_DIM
GROUP_A = N_HEADS_A // N_KV_A

MISC_W_LANE = IDX_DIM
MISC_F_LANE = IDX_DIM + N_IDX_HEADS

MASK_NEG = -1e30
SCORE_NEG = -3e38
SELECT_ITERS = 32
LOG2E = 1.4426950408889634
Q_SCALE = HEAD_DIM ** -0.5 * LOG2E
KV_TILE = 512
VMEM_LIMIT = 56 * 1024 * 1024


def _cparams(sem):
    return pltpu.CompilerParams(dimension_semantics=sem, vmem_limit_bytes=VMEM_LIMIT)


def _sigmoid(x):
    return 1.0 / (1.0 + jnp.exp(-x))


def _nt_dot(a, b):
    return lax.dot_general(a, b, (((1,), (1,)), ((), ())), preferred_element_type=jnp.float32)


def _rmsnorm_kernel(x_ref, g_ref, o_ref):
    x = x_ref[...]
    ms = jnp.mean(x * x, axis=-1, keepdims=True)
    o_ref[...] = (x * lax.rsqrt(ms + EPS) * g_ref[...]).astype(o_ref.dtype)


def _rmsnorm(x, g, tm=256):
    t, d = x.shape
    tm = min(tm, t)
    return pl.pallas_call(
        _rmsnorm_kernel,
        grid=(t // tm,),
        in_specs=[pl.BlockSpec((tm, d), lambda i: (i, 0)), pl.BlockSpec((1, d), lambda i: (0, 0))],
        out_specs=pl.BlockSpec((tm, d), lambda i: (i, 0)),
        out_shape=jax.ShapeDtypeStruct((t, d), jnp.bfloat16),
        compiler_params=_cparams(("parallel",)),
        name="rmsnorm",
    )(x, g.reshape(1, d))


def _mm_kernel(a_ref, w_ref, o_ref):
    o_ref[...] = jnp.dot(a_ref[...], w_ref[...], preferred_element_type=jnp.float32).astype(o_ref.dtype)


def _matmul(a, w, out_dtype, tm=1024, tn=512, name="matmul"):
    t, k = a.shape
    n = w.shape[1]
    tm, tn = min(tm, t), min(tn, n)
    return pl.pallas_call(
        _mm_kernel,
        grid=(t // tm, n // tn),
        in_specs=[pl.BlockSpec((tm, k), lambda i, j: (i, 0)), pl.BlockSpec((k, tn), lambda i, j: (0, j))],
        out_specs=pl.BlockSpec((tm, tn), lambda i, j: (i, j)),
        out_shape=jax.ShapeDtypeStruct((t, n), out_dtype),
        compiler_params=_cparams(("parallel", "parallel")),
        name=name,
    )(a, w)


def _mm_res_kernel(a_ref, w_ref, r_ref, o_ref):
    o_ref[...] = r_ref[...] + jnp.dot(a_ref[...], w_ref[...], preferred_element_type=jnp.float32)


def _matmul_residual(a, w, r, tm=1024, tn=512, name="matmul_residual"):
    t, k = a.shape
    n = w.shape[1]
    tm, tn = min(tm, t), min(tn, n)
    return pl.pallas_call(
        _mm_res_kernel,
        grid=(t // tm, n // tn),
        in_specs=[
            pl.BlockSpec((tm, k), lambda i, j: (i, 0)),
            pl.BlockSpec((k, tn), lambda i, j: (0, j)),
            pl.BlockSpec((tm, tn), lambda i, j: (i, j)),
        ],
        out_specs=pl.BlockSpec((tm, tn), lambda i, j: (i, j)),
        out_shape=jax.ShapeDtypeStruct((t, n), jnp.float32),
        compiler_params=_cparams(("parallel", "parallel")),
        name=name,
    )(a, w, r)


def _mm_res_acc_kernel(a_ref, w_ref, r_ref, o_ref, acc_ref):
    kk = pl.program_id(2)

    @pl.when(kk == 0)
    def _():
        acc_ref[...] = r_ref[...]

    acc_ref[...] += jnp.dot(a_ref[...], w_ref[...], preferred_element_type=jnp.float32)

    @pl.when(kk == pl.num_programs(2) - 1)
    def _():
        o_ref[...] = acc_ref[...]


def _matmul_residual_ksplit(a, w, r, tm=1024, tn=1024, tk=1024, name="matmul_residual_ksplit"):
    t, k = a.shape
    n = w.shape[1]
    tm, tn, tk = min(tm, t), min(tn, n), min(tk, k)
    return pl.pallas_call(
        _mm_res_acc_kernel,
        grid=(t // tm, n // tn, k // tk),
        in_specs=[
            pl.BlockSpec((tm, tk), lambda i, j, kk: (i, kk)),
            pl.BlockSpec((tk, tn), lambda i, j, kk: (kk, j)),
            pl.BlockSpec((tm, tn), lambda i, j, kk: (i, j)),
        ],
        out_specs=pl.BlockSpec((tm, tn), lambda i, j, kk: (i, j)),
        out_shape=jax.ShapeDtypeStruct((t, n), jnp.float32),
        scratch_shapes=[pltpu.VMEM((tm, tn), jnp.float32)],
        compiler_params=_cparams(("parallel", "parallel", "arbitrary")),
        name=name,
    )(a, w, r)


def _swiglu_kernel(a_ref, wg_ref, wu_ref, o_ref):
    a = a_ref[...]
    g = jnp.dot(a, wg_ref[...], preferred_element_type=jnp.float32)
    u = jnp.dot(a, wu_ref[...], preferred_element_type=jnp.float32)
    o_ref[...] = (g * _sigmoid(g) * u).astype(o_ref.dtype)


def _swiglu(a, wg, wu, tm=1024, tn=512):
    t, k = a.shape
    n = wg.shape[1]
    tm, tn = min(tm, t), min(tn, n)
    return pl.pallas_call(
        _swiglu_kernel,
        grid=(t // tm, n // tn),
        in_specs=[
            pl.BlockSpec((tm, k), lambda i, j: (i, 0)),
            pl.BlockSpec((k, tn), lambda i, j: (0, j)),
            pl.BlockSpec((k, tn), lambda i, j: (0, j)),
        ],
        out_specs=pl.BlockSpec((tm, tn), lambda i, j: (i, j)),
        out_shape=jax.ShapeDtypeStruct((t, n), jnp.bfloat16),
        compiler_params=_cparams(("parallel", "parallel")),
        name="swiglu",
    )(a, wg, wu)


def _merge_kernel(oa_ref, ob_ref, wa_ref, wb_ref, ga_ref, gb_ref, o_ref):
    a = jnp.dot(oa_ref[...], wa_ref[...], preferred_element_type=jnp.float32)
    b = jnp.dot(ob_ref[...], wb_ref[...], preferred_element_type=jnp.float32)
    ga = _sigmoid(ga_ref[...].astype(jnp.float32))
    gb = _sigmoid(gb_ref[...].astype(jnp.float32))
    o_ref[...] = (ga * a + gb * b).astype(o_ref.dtype)


def _merge(o_a, o_b, w_up_a, w_up_b, proj, ga_col, gb_col, tm=1024, tn=512):
    t, ka = o_a.shape
    kb = o_b.shape[1]
    n = w_up_a.shape[1]
    tm, tn = min(tm, t), min(tn, n)
    ga_blk, gb_blk = ga_col // tn, gb_col // tn
    return pl.pallas_call(
        _merge_kernel,
        grid=(t // tm, n // tn),
        in_specs=[
            pl.BlockSpec((tm, ka), lambda i, j: (i, 0)),
            pl.BlockSpec((tm, kb), lambda i, j: (i, 0)),
            pl.BlockSpec((ka, tn), lambda i, j: (0, j)),
            pl.BlockSpec((kb, tn), lambda i, j: (0, j)),
            pl.BlockSpec((tm, tn), lambda i, j: (i, ga_blk + j)),
            pl.BlockSpec((tm, tn), lambda i, j: (i, gb_blk + j)),
        ],
        out_specs=pl.BlockSpec((tm, tn), lambda i, j: (i, j)),
        out_shape=jax.ShapeDtypeStruct((t, n), jnp.bfloat16),
        compiler_params=_cparams(("parallel", "parallel")),
        name="merge",
    )(o_a, o_b, w_up_a, w_up_b, proj, proj)


def _ple_kernel(h_ref, wg_ref, p_ref, wp_ref, r_ref, o_ref):
    g = jnp.dot(h_ref[...], wg_ref[...], preferred_element_type=jnp.float32)
    e = jnp.dot(p_ref[...], wp_ref[...], preferred_element_type=jnp.float32)
    o_ref[...] = r_ref[...] + _sigmoid(g) * e


def _ple(h, w_gate, p, w_ple, r, tm=1024, tn=512):
    t, k = h.shape
    kp = p.shape[1]
    n = w_gate.shape[1]
    tm, tn = min(tm, t), min(tn, n)
    return pl.pallas_call(
        _ple_kernel,
        grid=(t // tm, n // tn),
        in_specs=[
            pl.BlockSpec((tm, k), lambda i, j: (i, 0)),
            pl.BlockSpec((k, tn), lambda i, j: (0, j)),
            pl.BlockSpec((tm, kp), lambda i, j: (i, 0)),
            pl.BlockSpec((kp, tn), lambda i, j: (0, j)),
            pl.BlockSpec((tm, tn), lambda i, j: (i, j)),
        ],
        out_specs=pl.BlockSpec((tm, tn), lambda i, j: (i, j)),
        out_shape=jax.ShapeDtypeStruct((t, n), jnp.float32),
        compiler_params=_cparams(("parallel", "parallel")),
        name="ple",
    )(h, w_gate, p, w_ple, r)


def _rope_consts():
    half = HEAD_DIM // 2
    inv_full = jnp.power(ROPE_THETA, -jnp.arange(half, dtype=jnp.float32) * (2.0 / HEAD_DIM))
    half_i = IDX_ROPE_DIM // 2
    inv_idx = jnp.power(ROPE_THETA, -jnp.arange(half_i, dtype=jnp.float32) * (2.0 / IDX_ROPE_DIM))
    zeros = jnp.zeros((LANES - IDX_ROPE_DIM,), jnp.float32)
    sign = np.concatenate([-np.ones(half, np.float32), np.ones(half, np.float32)])
    mask_a = np.zeros(LANES, np.float32)
    mask_a[:half_i] = -1.0
    mask_b = np.zeros(LANES, np.float32)
    mask_b[half_i:IDX_ROPE_DIM] = 1.0
    rows = [
        jnp.concatenate([inv_full, inv_full]),
        jnp.asarray(sign),
        jnp.concatenate([inv_idx, inv_idx, zeros]),
        jnp.asarray(mask_a),
        jnp.asarray(mask_b),
    ]
    rows += [jnp.zeros((LANES,), jnp.float32)] * 3
    return jnp.stack(rows)


def _rope_tables_kernel(pos_ref, c_ref, cf_ref, sf_ref, ci_ref, sa_ref, sb_ref):
    pos = pos_ref[...].astype(jnp.float32)
    ang = pos * c_ref[0:1, :]
    cf_ref[...] = jnp.cos(ang)
    sf_ref[...] = jnp.sin(ang) * c_ref[1:2, :]
    ang_i = pos * c_ref[2:3, :]
    ci_ref[...] = jnp.cos(ang_i)
    s_i = jnp.sin(ang_i)
    sa_ref[...] = s_i * c_ref[3:4, :]
    sb_ref[...] = s_i * c_ref[4:5, :]


def _rope_tables(pos_col, tm=512):
    t = pos_col.shape[0]
    tm = min(tm, t)
    tab = jax.ShapeDtypeStruct((t, LANES), jnp.float32)
    spec = pl.BlockSpec((tm, LANES), lambda i: (i, 0))
    return pl.pallas_call(
        _rope_tables_kernel,
        grid=(t // tm,),
        in_specs=[pl.BlockSpec((tm, 1), lambda i: (i, 0)), pl.BlockSpec((8, LANES), lambda i: (0, 0))],
        out_specs=[spec] * 5,
        out_shape=[tab] * 5,
        compiler_params=_cparams(("parallel",)),
        name="rope_tables",
    )(pos_col, _rope_consts())


def _rope_full(x, cf, sf):
    return x * cf + pltpu.roll(x, HEAD_DIM // 2, axis=1) * sf


def _rope_idx(x, ci, sa, sb):
    half = IDX_ROPE_DIM // 2
    return x * ci + pltpu.roll(x, LANES - half, axis=1) * sa + pltpu.roll(x, half, axis=1) * sb


def _head_rms(x, g):
    ms = jnp.mean(x * x, axis=-1, keepdims=True)
    return x * lax.rsqrt(ms + EPS) * g


def _q_proj_kernel(cq_ref, gcq_ref, wa_ref, wi_ref, gqa_ref, cf_ref, sf_ref, ci_ref, sa_ref, sb_ref, qa_ref, qi_ref):
    cq = cq_ref[...].astype(jnp.float32)
    ms = jnp.mean(cq * cq, axis=-1, keepdims=True)
    cq = (cq * lax.rsqrt(ms + EPS) * gcq_ref[...]).astype(jnp.bfloat16)
    cf, sf = cf_ref[...], sf_ref[...]
    gqa = gqa_ref[...]
    for j in range(N_HEADS_A // 2):
        acc = jnp.dot(cq, wa_ref[:, j * 2 * HEAD_DIM:(j + 1) * 2 * HEAD_DIM], preferred_element_type=jnp.float32)
        for e in range(2):
            hq = _head_rms(acc[:, e * HEAD_DIM:(e + 1) * HEAD_DIM], gqa)
            qa_ref[2 * j + e] = (_rope_full(hq, cf, sf) * Q_SCALE).astype(qa_ref.dtype)
    ci, sa, sb = ci_ref[...], sa_ref[...], sb_ref[...]
    for j in range(N_IDX_HEADS // 2):
        acc = jnp.dot(cq, wi_ref[:, j * 2 * LANES:(j + 1) * 2 * LANES], preferred_element_type=jnp.float32)
        for e in range(2):
            qi_ref[2 * j + e] = _rope_idx(acc[:, e * LANES:(e + 1) * LANES], ci, sa, sb).astype(qi_ref.dtype)


def _q_proj(proj, g_cq, w_uq, w_uq_idx_pad, g_q_a, tabs, tm=512):
    t = proj.shape[0]
    tm = min(tm, t)
    cf, sf, ci, sa, sb = tabs
    tab_spec = pl.BlockSpec((tm, LANES), lambda i: (i, 0))
    return pl.pallas_call(
        _q_proj_kernel,
        grid=(t // tm,),
        in_specs=[
            pl.BlockSpec((tm, Q_RANK), lambda i: (i, 0)),
            pl.BlockSpec((1, Q_RANK), lambda i: (0, 0)),
            pl.BlockSpec((Q_RANK, W_A), lambda i: (0, 0)),
            pl.BlockSpec((Q_RANK, N_IDX_HEADS * LANES), lambda i: (0, 0)),
            pl.BlockSpec((1, HEAD_DIM), lambda i: (0, 0)),
            tab_spec, tab_spec, tab_spec, tab_spec, tab_spec,
        ],
        out_specs=[
            pl.BlockSpec((N_HEADS_A, tm, HEAD_DIM), lambda i: (0, i, 0)),
            pl.BlockSpec((N_IDX_HEADS, tm, LANES), lambda i: (0, i, 0)),
        ],
        out_shape=[
            jax.ShapeDtypeStruct((N_HEADS_A, t, HEAD_DIM), jnp.bfloat16),
            jax.ShapeDtypeStruct((N_IDX_HEADS, t, LANES), jnp.bfloat16),
        ],
        compiler_params=_cparams(("parallel",)),
        name="q_proj",
    )(proj, g_cq.reshape(1, Q_RANK), w_uq, w_uq_idx_pad, g_q_a.reshape(1, HEAD_DIM), cf, sf, ci, sa, sb)


def _ka_kernel(x_ref, g_ref, cf_ref, sf_ref, o_ref):
    x = x_ref[...].astype(jnp.float32)
    g, cf, sf = g_ref[...], cf_ref[...], sf_ref[...]
    for h in range(N_KV_A):
        hk = _head_rms(x[:, h * HEAD_DIM:(h + 1) * HEAD_DIM], g)
        o_ref[:, h * HEAD_DIM:(h + 1) * HEAD_DIM] = _rope_full(hk, cf, sf).astype(o_ref.dtype)


def _ka_norm_rope(proj, col, g_k_a, cf, sf, tm=512):
    t = proj.shape[0]
    tm = min(tm, t)
    tab_spec = pl.BlockSpec((tm, LANES), lambda i: (i, 0))
    return pl.pallas_call(
        _ka_kernel,
        grid=(t // tm,),
        in_specs=[
            pl.BlockSpec((tm, KV_A_W), lambda i: (i, col // KV_A_W)),
            pl.BlockSpec((1, HEAD_DIM), lambda i: (0, 0)),
            tab_spec, tab_spec,
        ],
        out_specs=pl.BlockSpec((tm, KV_A_W), lambda i: (i, 0)),
        out_shape=jax.ShapeDtypeStruct((t, KV_A_W), jnp.bfloat16),
        compiler_params=_cparams(("parallel",)),
        name="ka_norm_rope",
    )(proj, g_k_a.reshape(1, HEAD_DIM), cf, sf)


def _split3_bf16(x):
    hi = x.astype(jnp.bfloat16)
    r1 = x - hi.astype(jnp.float32)
    mid = r1.astype(jnp.bfloat16)
    lo = (r1 - mid.astype(jnp.float32)).astype(jnp.bfloat16)
    return hi, mid, lo


def _misc_kernel(m_ref, c_ref, ci_ref, sa_ref, sb_ref, tri_ref, kidx_ref, wf_ref, wt_ref, frow_ref, carry_ref, *,
                 tiles_per_seq):
    i = pl.program_id(0)

    @pl.when(i % tiles_per_seq == 0)
    def _():
        carry_ref[...] = jnp.zeros_like(carry_ref)

    x = m_ref[...]
    lane = lax.broadcasted_iota(jnp.int32, x.shape, 1)
    is_k = lane < IDX_DIM
    xk = jnp.where(is_k, x, 0.0)
    mu = jnp.sum(xk, axis=-1, keepdims=True) * (1.0 / IDX_DIM)
    dk = jnp.where(is_k, x - mu, 0.0)
    var = jnp.sum(dk * dk, axis=-1, keepdims=True) * (1.0 / IDX_DIM)
    y = dk * lax.rsqrt(var + EPS) * c_ref[0:1, :] + c_ref[1:2, :]
    y = _rope_idx(y, ci_ref[...], sa_ref[...], sb_ref[...])
    kidx_ref[...] = jnp.where(is_k, y, 0.0).astype(kidx_ref.dtype)
    f = x + c_ref[2:3, :]
    log_f = jnp.minimum(f, 0.0) - jnp.log1p(jnp.exp(-jnp.abs(f)))
    hi, mid, lo = _split3_bf16(log_f)
    tri = tri_ref[...]
    csum = (jnp.dot(tri, hi, preferred_element_type=jnp.float32)
            + jnp.dot(tri, mid, preferred_element_type=jnp.float32)
            + jnp.dot(tri, lo, preferred_element_type=jnp.float32))
    csum = csum + carry_ref[0:1, :]
    carry_ref[...] = jnp.broadcast_to(csum[-1:, :], carry_ref.shape)
    is_w = (lane >= MISC_W_LANE) & (lane < MISC_F_LANE)
    wf = jnp.where(is_w, x * c_ref[3:4, :], csum * LOG2E)
    wf_ref[...] = wf
    wf_t = wf.T
    wt_ref[...] = wf_t[MISC_W_LANE:MISC_F_LANE, :]
    frow_ref[0] = wf_t[MISC_F_LANE:MISC_F_LANE + N_HEADS_B, :]


def _misc_post(misc, g_kidx, b_kidx, b_forget, tabs, seq):
    t = misc.shape[0]
    tm = min(KV_TILE, seq)
    idx_w_scale = (N_IDX_HEADS ** -0.5) * (IDX_DIM ** -0.5)
    pad = lambda v, off: jnp.zeros((LANES,), jnp.float32).at[off:off + v.shape[0]].set(v)
    consts = jnp.stack([
        pad(g_kidx, 0), pad(b_kidx, 0), pad(b_forget, MISC_F_LANE),
        pad(jnp.full((N_IDX_HEADS,), idx_w_scale, jnp.float32), MISC_W_LANE),
    ] + [jnp.zeros((LANES,), jnp.float32)] * 4)
    tri = jnp.asarray(np.tril(np.ones((tm, tm), np.float32)), jnp.bfloat16)
    _, _, ci, sa, sb = tabs
    tab_spec = pl.BlockSpec((tm, LANES), lambda i: (i, 0))
    return pl.pallas_call(
        functools.partial(_misc_kernel, tiles_per_seq=seq // tm),
        grid=(t // tm,),
        in_specs=[
            tab_spec,
            pl.BlockSpec((8, LANES), lambda i: (0, 0)),
            tab_spec, tab_spec, tab_spec,
            pl.BlockSpec((tm, tm), lambda i: (0, 0)),
        ],
        out_specs=[
            tab_spec, tab_spec,
            pl.BlockSpec((N_IDX_HEADS, tm), lambda i: (0, i)),
            pl.BlockSpec((1, N_HEADS_B, tm), lambda i: (i, 0, 0)),
        ],
        out_shape=[
            jax.ShapeDtypeStruct((t, LANES), jnp.bfloat16),
            jax.ShapeDtypeStruct((t, LANES), jnp.float32),
            jax.ShapeDtypeStruct((N_IDX_HEADS, t), jnp.float32),
            jax.ShapeDtypeStruct((t // tm, N_HEADS_B, tm), jnp.float32),
        ],
        scratch_shapes=[pltpu.VMEM((8, LANES), jnp.float32)],
        compiler_params=_cparams(("arbitrary",)),
        name="misc_post",
    )(misc, consts, ci, sa, sb, tri)


IDX_HEAD_GROUP = 8
IDX_SLAB = 128
COUNT_ROWS = 64


def _indexer_kernel(q_ref, k_ref, w_ref, o_ref, sc_ref, *, tq, tk, n_kt, top_k):
    qi = pl.program_id(1)
    t0 = qi * tq
    n_vis = (t0 + tq + tk - 1) // tk
    t_row = lax.broadcasted_iota(jnp.int32, (1, tq), 1) + t0
    vis_end = (jnp.right_shift(t_row, CHUNK_SHIFT) + 1) * CHUNK

    def score_tile(c, carry):
        mx, mn = carry
        kt = k_ref[pl.ds(pl.multiple_of(c * tk, tk), tk), :]
        for g in range(N_IDX_HEADS // IDX_HEAD_GROUP):
            qg = q_ref[g * IDX_HEAD_GROUP:(g + 1) * IDX_HEAD_GROUP].reshape(IDX_HEAD_GROUP * tq, LANES)
            st = _nt_dot(kt, qg)
            for r in range(tk // IDX_SLAB):
                rows = slice(r * IDX_SLAB, (r + 1) * IDX_SLAB)
                part = jnp.zeros((IDX_SLAB, tq), jnp.float32)
                for j in range(IDX_HEAD_GROUP):
                    h = g * IDX_HEAD_GROUP + j
                    part = part + jnp.maximum(st[rows, j * tq:(j + 1) * tq], 0.0) * w_ref[h:h + 1, :]
                if g == 0:
                    sc_ref[c, rows, :] = part
                else:
                    sc_ref[c, rows, :] += part
        s_col = lax.broadcasted_iota(jnp.int32, (tk, 1), 0) + c * tk
        adm = s_col < vis_end
        sc = sc_ref[c]
        sc_ref[c] = jnp.where(adm, sc, SCORE_NEG)
        mx = jnp.maximum(mx, jnp.max(jnp.where(adm, sc, SCORE_NEG), axis=0, keepdims=True))
        mn = jnp.minimum(mn, jnp.min(jnp.where(adm, sc, -SCORE_NEG), axis=0, keepdims=True))
        return mx, mn

    hi0, lo0 = lax.fori_loop(
        0, n_vis, score_tile,
        (jnp.full((1, tq), SCORE_NEG, jnp.float32), jnp.full((1, tq), -SCORE_NEG, jnp.float32)))

    def unsettled(state):
        it, _, _, active = state
        return jnp.logical_and(it < SELECT_ITERS, jnp.max(active) > 0.0)

    def bisect(state):
        it, lo, hi, active = state
        mid = 0.5 * (lo + hi)

        def count_tile(c, cnt):
            ge = jnp.where(sc_ref[c] >= mid, 1.0, 0.0)
            return cnt + jnp.sum(ge.reshape(tk // COUNT_ROWS, COUNT_ROWS, tq), axis=0)

        cnt = lax.fori_loop(0, n_vis, count_tile, jnp.zeros((COUNT_ROWS, tq), jnp.float32))
        cnt = jnp.sum(cnt, axis=0, keepdims=True)
        enough = cnt >= float(top_k)
        active = jnp.where(cnt == float(top_k), 0.0, active)
        return it + 1, jnp.where(enough, mid, lo), jnp.where(enough, hi, mid), active

    active0 = jnp.where(vis_end > top_k, 1.0, 0.0)
    _, thr, _, _ = lax.while_loop(unsettled, bisect, (jnp.int32(0), lo0, hi0, active0))

    def write_tile(c, carry):
        sel = jnp.where(sc_ref[c] >= thr, 0.0, MASK_NEG)
        o_ref[0, c] = sel.T.astype(o_ref.dtype)
        return carry

    lax.fori_loop(0, n_vis, write_tile, 0)

    def fill_tile(c, carry):
        o_ref[0, c] = jnp.full((tq, tk), MASK_NEG, o_ref.dtype)
        return carry

    lax.fori_loop(n_vis, n_kt, fill_tile, 0)


def _indexer_mask(q_idx, k_idx, w_t, batch, seq, top_k, tq=256):
    tq, tk = min(tq, seq), min(KV_TILE, seq)
    n_qt, n_kt = seq // tq, seq // tk
    return pl.pallas_call(
        functools.partial(_indexer_kernel, tq=tq, tk=tk, n_kt=n_kt, top_k=top_k),
        grid=(batch, n_qt),
        in_specs=[
            pl.BlockSpec((N_IDX_HEADS, tq, LANES), lambda b, i: (0, b * n_qt + i, 0)),
            pl.BlockSpec((seq, LANES), lambda b, i: (b, 0)),
            pl.BlockSpec((N_IDX_HEADS, tq), lambda b, i: (0, b * n_qt + i)),
        ],
        out_specs=pl.BlockSpec((1, n_kt, tq, tk), lambda b, i: (b, 0, i, 0)),
        out_shape=jax.ShapeDtypeStruct((batch, n_kt, seq, tk), jnp.bfloat16),
        scratch_shapes=[pltpu.VMEM((n_kt, tk, tq), jnp.float32)],
        compiler_params=_cparams(("parallel", "parallel")),
        name="indexer_mask",
    )(q_idx, k_idx, w_t)


def _softmax_update(s, row_shift, v, m_ref, l_ref, acc_ref, e):
    rows, tk = s.shape
    chunks = [s[:, c * LANES:(c + 1) * LANES] for c in range(tk // LANES)]
    mx = functools.reduce(jnp.maximum, chunks)
    mx = jnp.broadcast_to(jnp.max(mx, axis=-1, keepdims=True), (rows, LANES))
    m_prev = m_ref[e]
    m_new = jnp.maximum(m_prev, mx + row_shift)
    alpha = jnp.exp2(m_prev - m_new)
    r = m_new - row_shift
    p = jnp.concatenate([jnp.exp2(c - r) for c in chunks], axis=1).astype(v.dtype)
    v_ones = jnp.concatenate([v, jnp.ones((tk, LANES), v.dtype)], axis=1)
    pv = jnp.dot(p, v_ones, preferred_element_type=jnp.float32)
    acc_ref[e] = alpha * acc_ref[e] + pv[:, :HEAD_DIM]
    l_ref[e] = alpha * l_ref[e] + pv[:, HEAD_DIM:]
    m_ref[e] = m_new


def _softmax_init(m_ref, l_ref, acc_ref):
    m_ref[...] = jnp.full(m_ref.shape, MASK_NEG, jnp.float32)
    l_ref[...] = jnp.zeros(l_ref.shape, jnp.float32)
    acc_ref[...] = jnp.zeros(acc_ref.shape, jnp.float32)


def _dsa_attn_kernel(q_ref, k_ref, v_ref, b_ref, o_ref, m_ref, l_ref, acc_ref, *, tq, tk):
    qi = pl.program_id(1)
    n_need = ((qi + 1) * tq + tk - 1) // tk
    rows = GROUP_A * tq
    _softmax_init(m_ref, l_ref, acc_ref)

    def kv_step(j, carry):
        off = pl.multiple_of(j * tk, tk)
        bias = b_ref[0, j].astype(jnp.float32)[None]
        for g in range(N_KV_A):
            cols = slice(g * HEAD_DIM, (g + 1) * HEAD_DIM)
            q = q_ref[g * GROUP_A:(g + 1) * GROUP_A].reshape(rows, HEAD_DIM)
            s = _nt_dot(q, k_ref[pl.ds(off, tk), cols])
            s = (s.reshape(GROUP_A, tq, tk) + bias).reshape(rows, tk)
            _softmax_update(s, 0.0, v_ref[pl.ds(off, tk), cols], m_ref, l_ref, acc_ref, g)
        return carry

    lax.fori_loop(0, n_need, kv_step, 0)

    for g in range(N_KV_A):
        out = acc_ref[g] / l_ref[g]
        for h in range(GROUP_A):
            c0 = (g * GROUP_A + h) * HEAD_DIM
            o_ref[:, c0:c0 + HEAD_DIM] = out[h * tq:(h + 1) * tq].astype(o_ref.dtype)


def _dsa_attention(q_a, k_a, proj, v_col, bias, batch, seq, tq=256):
    tk = bias.shape[-1]
    tq = min(tq, seq)
    n_qt, n_kt = seq // tq, seq // tk
    rows = GROUP_A * tq
    return pl.pallas_call(
        functools.partial(_dsa_attn_kernel, tq=tq, tk=tk),
        grid=(batch, n_qt),
        in_specs=[
            pl.BlockSpec((N_HEADS_A, tq, HEAD_DIM), lambda b, i: (0, b * n_qt + i, 0)),
            pl.BlockSpec((seq, KV_A_W), lambda b, i: (b, 0)),
            pl.BlockSpec((seq, KV_A_W), lambda b, i: (b, v_col // KV_A_W)),
            pl.BlockSpec((1, n_kt, tq, tk), lambda b, i: (b, 0, i, 0)),
        ],
        out_specs=pl.BlockSpec((tq, W_A), lambda b, i: (b * n_qt + i, 0)),
        out_shape=jax.ShapeDtypeStruct((batch * seq, W_A), jnp.bfloat16),
        scratch_shapes=[
            pltpu.VMEM((N_KV_A, rows, LANES), jnp.float32),
            pltpu.VMEM((N_KV_A, rows, LANES), jnp.float32),
            pltpu.VMEM((N_KV_A, rows, HEAD_DIM), jnp.float32),
        ],
        compiler_params=_cparams(("parallel", "parallel")),
        name="dsa_attention",
    )(q_a, k_a, proj, bias)


FOX_HEADS_PER_STEP = 4


def _fox_kernel(q_ref, k_ref, v_ref, gq_ref, gk_ref, fq_ref, fk_ref, o_ref, kn_ref, m_ref, l_ref, acc_ref, *, tile):
    hp, qi = pl.program_id(1), pl.program_id(2)
    n_t = kn_ref.shape[0] // tile

    @pl.when(qi == 0)
    def _():
        def norm_keys(j, carry):
            off = pl.multiple_of(j * tile, tile)
            for e in range(FOX_HEADS_PER_STEP):
                cols = slice(e * HEAD_DIM, (e + 1) * HEAD_DIM)
                k = k_ref[pl.ds(off, tile), cols].astype(jnp.float32)
                kn_ref[pl.ds(off, tile), cols] = _head_rms(k, gk_ref[...]).astype(kn_ref.dtype)
            return carry

        lax.fori_loop(0, n_t, norm_keys, 0)

    lane = lax.broadcasted_iota(jnp.int32, (tile, LANES), 1)
    heads = [hp * FOX_HEADS_PER_STEP + e for e in range(FOX_HEADS_PER_STEP)]
    fq = [jnp.broadcast_to(jnp.sum(jnp.where(lane == MISC_F_LANE + h, fq_ref[...], 0.0), axis=1, keepdims=True),
                           (tile, LANES)) for h in heads]
    qn = [_head_rms(q_ref[:, e * HEAD_DIM:(e + 1) * HEAD_DIM].astype(jnp.float32), gq_ref[...]).astype(kn_ref.dtype)
          for e in range(FOX_HEADS_PER_STEP)]
    _softmax_init(m_ref, l_ref, acc_ref)

    def kv_step(j, diagonal):
        off = pl.multiple_of(j * tile, tile)
        for e, h in enumerate(heads):
            cols = slice(e * HEAD_DIM, (e + 1) * HEAD_DIM)
            s = _nt_dot(qn[e], kn_ref[pl.ds(off, tile), cols]) - fk_ref[j, pl.ds(h, 1), :]
            if diagonal:
                t_pos = lax.broadcasted_iota(jnp.int32, (tile, tile), 0)
                s_pos = lax.broadcasted_iota(jnp.int32, (tile, tile), 1)
                s = jnp.where(s_pos <= t_pos, s, MASK_NEG)
            _softmax_update(s, fq[e], v_ref[pl.ds(off, tile), cols], m_ref, l_ref, acc_ref, e)

    def full_step(j, carry):
        kv_step(j, False)
        return carry

    lax.fori_loop(0, qi, full_step, 0)
    kv_step(qi, True)

    for e in range(FOX_HEADS_PER_STEP):
        o_ref[:, e * HEAD_DIM:(e + 1) * HEAD_DIM] = (acc_ref[e] / l_ref[e]).astype(o_ref.dtype)


def _fox_attention(proj, q_col, g_q, g_k, wf, f_row, batch, seq):
    tile = min(KV_TILE, seq)
    n_t = seq // tile
    wb = FOX_HEADS_PER_STEP * HEAD_DIM
    q_blk, k_blk, v_blk = q_col // wb, (q_col + W_B) // wb, (q_col + 2 * W_B) // wb
    gain_spec = pl.BlockSpec((1, HEAD_DIM), lambda b, h, i: (0, 0))
    return pl.pallas_call(
        functools.partial(_fox_kernel, tile=tile),
        grid=(batch, N_HEADS_B // FOX_HEADS_PER_STEP, n_t),
        in_specs=[
            pl.BlockSpec((tile, wb), lambda b, h, i: (b * n_t + i, q_blk + h)),
            pl.BlockSpec((seq, wb), lambda b, h, i: (b, k_blk + h)),
            pl.BlockSpec((seq, wb), lambda b, h, i: (b, v_blk + h)),
            gain_spec, gain_spec,
            pl.BlockSpec((tile, LANES), lambda b, h, i: (b * n_t + i, 0)),
            pl.BlockSpec((n_t, N_HEADS_B, tile), lambda b, h, i: (b, 0, 0)),
        ],
        out_specs=pl.BlockSpec((tile, wb), lambda b, h, i: (b * n_t + i, h)),
        out_shape=jax.ShapeDtypeStruct((batch * seq, W_B), jnp.bfloat16),
        scratch_shapes=[
            pltpu.VMEM((seq, wb), jnp.bfloat16),
            pltpu.VMEM((FOX_HEADS_PER_STEP, tile, LANES), jnp.float32),
            pltpu.VMEM((FOX_HEADS_PER_STEP, tile, LANES), jnp.float32),
            pltpu.VMEM((FOX_HEADS_PER_STEP, tile, HEAD_DIM), jnp.float32),
        ],
        compiler_params=_cparams(("parallel", "parallel", "arbitrary")),
        name="fox_attention",
    )(proj, proj, proj, g_q.reshape(1, HEAD_DIM), g_k.reshape(1, HEAD_DIM), wf, f_row)


def _layer(x, p, pos_col, batch, seq, g_attn, w_in, g_cq, w_uq, w_uq_idx, g_kidx, b_kidx, g_q_a, g_k_a, b_forget,
           g_q_b, g_k_b, w_up_a, w_up_b, w_o, g_ffn, w_ffn_gate, w_ffn_up, w_ffn_down, g_ple, w_ple, w_ple_gate):
    d = x.shape[1]
    bf = jnp.bfloat16
    top_k = min(INDEX_TOPK, seq // 4)

    o_kidx = Q_RANK + 2 * KV_A_W
    o_qb = o_kidx + IDX_DIM + N_IDX_HEADS
    o_fb = o_qb + 3 * W_B
    o_ga = o_fb + N_HEADS_B
    w_main = jnp.concatenate([w_in[:, :o_kidx].astype(bf), w_in[:, o_qb:o_fb].astype(bf), w_in[:, o_ga:].astype(bf)],
                             axis=1)
    w_misc = jnp.concatenate(
        [w_in[:, o_kidx:o_qb], w_in[:, o_fb:o_ga], jnp.zeros((d, LANES - (o_qb - o_kidx) - N_HEADS_B), w_in.dtype)],
        axis=1).astype(bf)
    col_ka, col_va = Q_RANK, Q_RANK + KV_A_W
    col_qb = Q_RANK + 2 * KV_A_W
    col_vb = col_qb + 2 * W_B
    col_ga = col_vb + W_B
    col_gb = col_ga + d
    w_qidx_pad = jnp.pad(w_uq_idx.reshape(Q_RANK, N_IDX_HEADS, IDX_DIM), ((0, 0), (0, 0), (0, LANES - IDX_DIM)))
    w_qidx_pad = w_qidx_pad.reshape(Q_RANK, N_IDX_HEADS * LANES).astype(bf)
    d_ff = w_ffn_gate.shape[1]
    ff_tile = 512
    d_ff_pad = -(-d_ff // ff_tile) * ff_tile
    pad_cols = lambda w: jnp.pad(w, ((0, 0), (0, d_ff_pad - d_ff))).astype(bf)
    w_g, w_u = pad_cols(w_ffn_gate), pad_cols(w_ffn_up)
    w_d = jnp.pad(w_ffn_down, ((0, d_ff_pad - d_ff), (0, 0))).astype(bf)

    tabs = _rope_tables(pos_col)
    cf, sf = tabs[0], tabs[1]

    h = _rmsnorm(x, g_attn)
    proj = _matmul(h, w_main, bf, name="proj_main")
    misc = _matmul(h, w_misc, jnp.float32, tn=LANES, name="proj_misc")

    q_a, q_idx = _q_proj(proj, g_cq, w_uq.astype(bf), w_qidx_pad, g_q_a, tabs)
    k_a = _ka_norm_rope(proj, col_ka, g_k_a, cf, sf)
    k_idx, wf, w_t, f_row = _misc_post(misc, g_kidx, b_kidx, b_forget, tabs, seq)
    bias = _indexer_mask(q_idx, k_idx, w_t, batch, seq, top_k)
    o_a = _dsa_attention(q_a, k_a, proj, col_va, bias, batch, seq)

    o_b = _fox_attention(proj, col_qb, g_q_b * Q_SCALE, g_k_b, wf, f_row, batch, seq)

    merged = _merge(o_a, o_b, w_up_a.astype(bf), w_up_b.astype(bf), proj, col_ga, col_gb)
    x = _matmul_residual(merged, w_o.astype(bf), x, name="out_proj")

    h = _rmsnorm(x, g_ffn)
    u = _swiglu(h, w_g, w_u, tn=ff_tile)
    x = _matmul_residual_ksplit(u, w_d, x, tk=d_ff_pad // 4, name="ffn_down")

    h = _rmsnorm(x, g_ple)
    x = _ple(h, w_ple_gate.astype(bf), p.astype(bf), w_ple.astype(bf), x)
    return x


def kernel(x, p, positions, g_attn, w_in, g_cq, w_uq, w_uq_idx, g_kidx, b_kidx, g_q_a, g_k_a, b_forget, g_q_b, g_k_b,
           w_up_a, w_up_b, w_o, g_ffn, w_ffn_gate, w_ffn_up, w_ffn_down, g_ple, w_ple, w_ple_gate):
    batch, seq, d = x.shape
    depth = w_in.shape[0]
    xf = x.reshape(batch * seq, d)
    pos_col = positions.reshape(batch * seq, 1)
    for i in range(depth):
        xf = _layer(xf, p[i].reshape(batch * seq, -1), pos_col, batch, seq, g_attn[i], w_in[i], g_cq[i], w_uq[i],
                    w_uq_idx[i], g_kidx[i], b_kidx[i], g_q_a[i], g_k_a[i], b_forget[i], g_q_b[i], g_k_b[i],
                    w_up_a[i], w_up_b[i], w_o[i], g_ffn[i], w_ffn_gate[i], w_ffn_up[i], w_ffn_down[i], g_ple[i],
                    w_ple[i], w_ple_gate[i])
    return xf.reshape(batch, seq, d)
```

```python
import functools

import numpy as np
import jax
import jax.numpy as jnp
from jax import lax
from jax.experimental import pallas as pl
from jax.experimental.pallas import tpu as pltpu

CHUNK = 64
CHUNK_SHIFT = CHUNK.bit_length() - 1
HEAD_DIM = 128
ROPE_THETA = 10000.0
EPS = 1e-6
N_HEADS_A = 16
N_KV_A = 2
Q_RANK = 1024
N_IDX_HEADS = 32
IDX_DIM = 64
IDX_ROPE_DIM = 32
INDEX_TOPK = 256
N_HEADS_B = 16
PLE_DIM = 256

LANES = 128
KV_A_W = N_KV_A * HEAD_DIM
W_A = N_HEADS_A * HEAD_DIM
W_B = N_HEADS_B * HEAD_DIM
GROUP_A = N_HEADS_A // N_KV_A

MISC_W_LANE = IDX_DIM
MISC_F_LANE = IDX_DIM + N_IDX_HEADS

MASK_NEG = -1e30
SCORE_NEG = -3e38
SELECT_ITERS = 32
LOG2E = 1.4426950408889634
Q_SCALE = HEAD_DIM ** -0.5 * LOG2E
KV_TILE = 512
VMEM_LIMIT = 56 * 1024 * 1024


def _cparams(sem):
    return pltpu.CompilerParams(dimension_semantics=sem, vmem_limit_bytes=VMEM_LIMIT)


def _sigmoid(x):
    return 1.0 / (1.0 + jnp.exp(-x))


def _nt_dot(a, b):
    return lax.dot_general(a, b, (((1,), (1,)), ((), ())), preferred_element_type=jnp.float32)


def _stage_pad_kernel(w_ref, o_ref, *, rows, cols):
    tr, tc = o_ref.shape
    r = pl.program_id(0) * tr + lax.broadcasted_iota(jnp.int32, (tr, tc), 0)
    c = pl.program_id(1) * tc + lax.broadcasted_iota(jnp.int32, (tr, tc), 1)
    o_ref[...] = jnp.where((r < rows) & (c < cols), w_ref[...], 0.0).astype(o_ref.dtype)


def _stage_pad(w, out_rows, out_cols, tr=512, tc=1024):
    rows, cols = w.shape
    return pl.pallas_call(
        functools.partial(_stage_pad_kernel, rows=rows, cols=cols),
        grid=(out_rows // tr, out_cols // tc),
        in_specs=[pl.BlockSpec((tr, tc), lambda i, j: (i, j))],
        out_specs=pl.BlockSpec((tr, tc), lambda i, j: (i, j)),
        out_shape=jax.ShapeDtypeStruct((out_rows, out_cols), jnp.bfloat16),
        compiler_params=_cparams(("parallel", "parallel")),
        name="stage_pad",
    )(w)


def _stage_regroup_kernel(a_ref, b_ref, o_ref, *, tile_shifts):
    j = pl.program_id(1)
    for first, last, shift in tile_shifts:
        @pl.when((j >= first) & (j < last))
        def _(shift=shift):
            if shift == 0:
                o_ref[...] = a_ref[...].astype(o_ref.dtype)
            else:
                o_ref[...] = jnp.concatenate([a_ref[:, shift:], b_ref[:, :shift]], axis=1).astype(o_ref.dtype)


def _stage_regroup(w, segments, tr=1024, tc=512):
    rows = w.shape[0]
    tile_shifts, dst = [], 0
    for src, width in segments:
        assert dst % tc == 0 and width % tc == 0 and 0 <= src - dst < LANES
        tile_shifts.append((dst // tc, (dst + width) // tc, src - dst))
        dst += width
    tr = min(tr, rows)
    return pl.pallas_call(
        functools.partial(_stage_regroup_kernel, tile_shifts=tuple(tile_shifts)),
        grid=(rows // tr, dst // tc),
        in_specs=[
            pl.BlockSpec((tr, tc), lambda i, j: (i, j)),
            pl.BlockSpec((tr, LANES), lambda i, j: (i, (tc // LANES) * (j + 1))),
        ],
        out_specs=pl.BlockSpec((tr, tc), lambda i, j: (i, j)),
        out_shape=jax.ShapeDtypeStruct((rows, dst), jnp.bfloat16),
        compiler_params=_cparams(("parallel", "parallel")),
        name="stage_regroup",
    )(w, w)


def _rmsnorm_kernel(x_ref, g_ref, o_ref):
    x = x_ref[...]
    ms = jnp.mean(x * x, axis=-1, keepdims=True)
    o_ref[...] = (x * lax.rsqrt(ms + EPS) * g_ref[...]).astype(o_ref.dtype)


def _rmsnorm(x, g, tm=256):
    t, d = x.shape
    tm = min(tm, t)
    return pl.pallas_call(
        _rmsnorm_kernel,
        grid=(t // tm,),
        in_specs=[pl.BlockSpec((tm, d), lambda i: (i, 0)), pl.BlockSpec((1, d), lambda i: (0, 0))],
        out_specs=pl.BlockSpec((tm, d), lambda i: (i, 0)),
        out_shape=jax.ShapeDtypeStruct((t, d), jnp.bfloat16),
        compiler_params=_cparams(("parallel",)),
        name="rmsnorm",
    )(x, g.reshape(1, d))


def _mm_kernel(a_ref, w_ref, o_ref):
    o_ref[...] = jnp.dot(a_ref[...], w_ref[...], preferred_element_type=jnp.float32).astype(o_ref.dtype)


def _matmul(a, w, out_dtype, tm=1024, tn=512, name="matmul"):
    t, k = a.shape
    n = w.shape[1]
    tm, tn = min(tm, t), min(tn, n)
    return pl.pallas_call(
        _mm_kernel,
        grid=(t // tm, n // tn),
        in_specs=[pl.BlockSpec((tm, k), lambda i, j: (i, 0)), pl.BlockSpec((k, tn), lambda i, j: (0, j))],
        out_specs=pl.BlockSpec((tm, tn), lambda i, j: (i, j)),
        out_shape=jax.ShapeDtypeStruct((t, n), out_dtype),
        compiler_params=_cparams(("parallel", "parallel")),
        name=name,
    )(a, w)


def _mm_res_kernel(a_ref, w_ref, r_ref, o_ref):
    o_ref[...] = r_ref[...] + jnp.dot(a_ref[...], w_ref[...], preferred_element_type=jnp.float32)


def _matmul_residual(a, w, r, tm=1024, tn=512, name="matmul_residual"):
    t, k = a.shape
    n = w.shape[1]
    tm, tn = min(tm, t), min(tn, n)
    return pl.pallas_call(
        _mm_res_kernel,
        grid=(t // tm, n // tn),
        in_specs=[
            pl.BlockSpec((tm, k), lambda i, j: (i, 0)),
            pl.BlockSpec((k, tn), lambda i, j: (0, j)),
            pl.BlockSpec((tm, tn), lambda i, j: (i, j)),
        ],
        out_specs=pl.BlockSpec((tm, tn), lambda i, j: (i, j)),
        out_shape=jax.ShapeDtypeStruct((t, n), jnp.float32),
        compiler_params=_cparams(("parallel", "parallel")),
        name=name,
    )(a, w, r)


def _mm_res_acc_kernel(a_ref, w_ref, r_ref, o_ref, acc_ref):
    kk = pl.program_id(2)

    @pl.when(kk == 0)
    def _():
        acc_ref[...] = r_ref[...]

    acc_ref[...] += jnp.dot(a_ref[...], w_ref[...], preferred_element_type=jnp.float32)

    @pl.when(kk == pl.num_programs(2) - 1)
    def _():
        o_ref[...] = acc_ref[...]


def _matmul_residual_ksplit(a, w, r, tm=1024, tn=1024, tk=1024, name="matmul_residual_ksplit"):
    t, k = a.shape
    n = w.shape[1]
    tm, tn, tk = min(tm, t), min(tn, n), min(tk, k)
    return pl.pallas_call(
        _mm_res_acc_kernel,
        grid=(t // tm, n // tn, k // tk),
        in_specs=[
            pl.BlockSpec((tm, tk), lambda i, j, kk: (i, kk)),
            pl.BlockSpec((tk, tn), lambda i, j, kk: (kk, j)),
            pl.BlockSpec((tm, tn), lambda i, j, kk: (i, j)),
        ],
        out_specs=pl.BlockSpec((tm, tn), lambda i, j, kk: (i, j)),
        out_shape=jax.ShapeDtypeStruct((t, n), jnp.float32),
        scratch_shapes=[pltpu.VMEM((tm, tn), jnp.float32)],
        compiler_params=_cparams(("parallel", "parallel", "arbitrary")),
        name=name,
    )(a, w, r)


def _swiglu_kernel(a_ref, wg_ref, wu_ref, o_ref):
    a = a_ref[...]
    g = jnp.dot(a, wg_ref[...], preferred_element_type=jnp.float32)
    u = jnp.dot(a, wu_ref[...], preferred_element_type=jnp.float32)
    o_ref[...] = (g * _sigmoid(g) * u).astype(o_ref.dtype)


def _swiglu(a, wg, wu, tm=1024, tn=512):
    t, k = a.shape
    n = wg.shape[1]
    tm, tn = min(tm, t), min(tn, n)
    return pl.pallas_call(
        _swiglu_kernel,
        grid=(t // tm, n // tn),
        in_specs=[
            pl.BlockSpec((tm, k), lambda i, j: (i, 0)),
            pl.BlockSpec((k, tn), lambda i, j: (0, j)),
            pl.BlockSpec((k, tn), lambda i, j: (0, j)),
        ],
        out_specs=pl.BlockSpec((tm, tn), lambda i, j: (i, j)),
        out_shape=jax.ShapeDtypeStruct((t, n), jnp.bfloat16),
        compiler_params=_cparams(("parallel", "parallel")),
        name="swiglu",
    )(a, wg, wu)


def _merge_kernel(oa_ref, ob_ref, wa_ref, wb_ref, ga_ref, gb_ref, o_ref):
    a = jnp.dot(oa_ref[...], wa_ref[...], preferred_element_type=jnp.float32)
    b = jnp.dot(ob_ref[...], wb_ref[...], preferred_element_type=jnp.float32)
    ga = _sigmoid(ga_ref[...].astype(jnp.float32))
    gb = _sigmoid(gb_ref[...].astype(jnp.float32))
    o_ref[...] = (ga * a + gb * b).astype(o_ref.dtype)


def _merge(o_a, o_b, w_up_a, w_up_b, proj, ga_col, gb_col, tm=1024, tn=512):
    t, ka = o_a.shape
    kb = o_b.shape[1]
    n = w_up_a.shape[1]
    tm, tn = min(tm, t), min(tn, n)
    ga_blk, gb_blk = ga_col // tn, gb_col // tn
    return pl.pallas_call(
        _merge_kernel,
        grid=(t // tm, n // tn),
        in_specs=[
            pl.BlockSpec((tm, ka), lambda i, j: (i, 0)),
            pl.BlockSpec((tm, kb), lambda i, j: (i, 0)),
            pl.BlockSpec((ka, tn), lambda i, j: (0, j)),
            pl.BlockSpec((kb, tn), lambda i, j: (0, j)),
            pl.BlockSpec((tm, tn), lambda i, j: (i, ga_blk + j)),
            pl.BlockSpec((tm, tn), lambda i, j: (i, gb_blk + j)),
        ],
        out_specs=pl.BlockSpec((tm, tn), lambda i, j: (i, j)),
        out_shape=jax.ShapeDtypeStruct((t, n), jnp.bfloat16),
        compiler_params=_cparams(("parallel", "parallel")),
        name="merge",
    )(o_a, o_b, w_up_a, w_up_b, proj, proj)


def _ple_kernel(h_ref, wg_ref, p_ref, wp_ref, r_ref, o_ref):
    g = jnp.dot(h_ref[...], wg_ref[...], preferred_element_type=jnp.float32)
    e = jnp.dot(p_ref[...], wp_ref[...], preferred_element_type=jnp.float32)
    o_ref[...] = r_ref[...] + _sigmoid(g) * e


def _ple(h, w_gate, p, w_ple, r, tm=1024, tn=512):
    t, k = h.shape
    kp = p.shape[1]
    n = w_gate.shape[1]
    tm, tn = min(tm, t), min(tn, n)
    return pl.pallas_call(
        _ple_kernel,
        grid=(t // tm, n // tn),
        in_specs=[
            pl.BlockSpec((tm, k), lambda i, j: (i, 0)),
            pl.BlockSpec((k, tn), lambda i, j: (0, j)),
            pl.BlockSpec((tm, kp), lambda i, j: (i, 0)),
            pl.BlockSpec((kp, tn), lambda i, j: (0, j)),
            pl.BlockSpec((tm, tn), lambda i, j: (i, j)),
        ],
        out_specs=pl.BlockSpec((tm, tn), lambda i, j: (i, j)),
        out_shape=jax.ShapeDtypeStruct((t, n), jnp.float32),
        compiler_params=_cparams(("parallel", "parallel")),
        name="ple",
    )(h, w_gate, p, w_ple, r)


def _rope_consts():
    half = HEAD_DIM // 2
    inv_full = jnp.power(ROPE_THETA, -jnp.arange(half, dtype=jnp.float32) * (2.0 / HEAD_DIM))
    half_i = IDX_ROPE_DIM // 2
    inv_idx = jnp.power(ROPE_THETA, -jnp.arange(half_i, dtype=jnp.float32) * (2.0 / IDX_ROPE_DIM))
    zeros = jnp.zeros((IDX_DIM - IDX_ROPE_DIM,), jnp.float32)
    sign = np.concatenate([-np.ones(half, np.float32), np.ones(half, np.float32)])
    mask_a = np.zeros(IDX_DIM, np.float32)
    mask_a[:half_i] = -1.0
    mask_b = np.zeros(IDX_DIM, np.float32)
    mask_b[half_i:IDX_ROPE_DIM] = 1.0
    reps = LANES // IDX_DIM
    rows = [
        jnp.concatenate([inv_full, inv_full]),
        jnp.asarray(sign),
        jnp.tile(jnp.concatenate([inv_idx, inv_idx, zeros]), reps),
        jnp.asarray(np.tile(mask_a, reps)),
        jnp.asarray(np.tile(mask_b, reps)),
    ]
    rows += [jnp.zeros((LANES,), jnp.float32)] * 3
    return jnp.stack(rows)


def _rope_tables_kernel(pos_ref, c_ref, cf_ref, sf_ref, ci_ref, sa_ref, sb_ref):
    pos = pos_ref[...].astype(jnp.float32)
    ang = pos * c_ref[0:1, :]
    cf_ref[...] = jnp.cos(ang)
    sf_ref[...] = jnp.sin(ang) * c_ref[1:2, :]
    ang_i = pos * c_ref[2:3, :]
    ci_ref[...] = jnp.cos(ang_i)
    s_i = jnp.sin(ang_i)
    sa_ref[...] = s_i * c_ref[3:4, :]
    sb_ref[...] = s_i * c_ref[4:5, :]


def _rope_tables(pos_col, tm=512):
    t = pos_col.shape[0]
    tm = min(tm, t)
    tab = jax.ShapeDtypeStruct((t, LANES), jnp.float32)
    spec = pl.BlockSpec((tm, LANES), lambda i: (i, 0))
    return pl.pallas_call(
        _rope_tables_kernel,
        grid=(t // tm,),
        in_specs=[pl.BlockSpec((tm, 1), lambda i: (i, 0)), pl.BlockSpec((8, LANES), lambda i: (0, 0))],
        out_specs=[spec] * 5,
        out_shape=[tab] * 5,
        compiler_params=_cparams(("parallel",)),
        name="rope_tables",
    )(pos_col, _rope_consts())


def _rope_full(x, cf, sf):
    return x * cf + pltpu.roll(x, HEAD_DIM // 2, axis=1) * sf


def _rope_idx(x, ci, sa, sb):
    half = IDX_ROPE_DIM // 2
    return x * ci + pltpu.roll(x, LANES - half, axis=1) * sa + pltpu.roll(x, half, axis=1) * sb


def _head_rms(x, g):
    ms = jnp.mean(x * x, axis=-1, keepdims=True)
    return x * lax.rsqrt(ms + EPS) * g


def _q_proj_kernel(cq_ref, gcq_ref, wa_ref, wi_ref, gqa_ref, cf_ref, sf_ref, ci_ref, sa_ref, sb_ref, qa_ref, qi_ref):
    cq = cq_ref[...].astype(jnp.float32)
    ms = jnp.mean(cq * cq, axis=-1, keepdims=True)
    cq = (cq * lax.rsqrt(ms + EPS) * gcq_ref[...]).astype(jnp.bfloat16)
    cf, sf = cf_ref[...], sf_ref[...]
    gqa = gqa_ref[...]
    for j in range(N_HEADS_A // 2):
        acc = jnp.dot(cq, wa_ref[:, j * 2 * HEAD_DIM:(j + 1) * 2 * HEAD_DIM], preferred_element_type=jnp.float32)
        for e in range(2):
            hq = _head_rms(acc[:, e * HEAD_DIM:(e + 1) * HEAD_DIM], gqa)
            qa_ref[2 * j + e] = (_rope_full(hq, cf, sf) * Q_SCALE).astype(qa_ref.dtype)
    ci, sa, sb = ci_ref[...], sa_ref[...], sb_ref[...]
    low_half = lax.broadcasted_iota(jnp.int32, (cq.shape[0], LANES), 1) < IDX_DIM
    per_dot = 2 * LANES // IDX_DIM
    for j in range(N_IDX_HEADS // per_dot):
        acc = jnp.dot(cq, wi_ref[:, j * 2 * LANES:(j + 1) * 2 * LANES], preferred_element_type=jnp.float32)
        for c in range(2):
            pair = _rope_idx(acc[:, c * LANES:(c + 1) * LANES], ci, sa, sb)
            for e, head in enumerate((pair, pltpu.roll(pair, IDX_DIM, axis=1))):
                qi_ref[per_dot * j + 2 * c + e] = jnp.where(low_half, head, 0.0).astype(qi_ref.dtype)


def _q_proj(proj, g_cq, w_uq, w_uq_idx, g_q_a, tabs, tm=512):
    t = proj.shape[0]
    tm = min(tm, t)
    cf, sf, ci, sa, sb = tabs
    tab_spec = pl.BlockSpec((tm, LANES), lambda i: (i, 0))
    return pl.pallas_call(
        _q_proj_kernel,
        grid=(t // tm,),
        in_specs=[
            pl.BlockSpec((tm, Q_RANK), lambda i: (i, 0)),
            pl.BlockSpec((1, Q_RANK), lambda i: (0, 0)),
            pl.BlockSpec((Q_RANK, W_A), lambda i: (0, 0)),
            pl.BlockSpec((Q_RANK, N_IDX_HEADS * IDX_DIM), lambda i: (0, 0)),
            pl.BlockSpec((1, HEAD_DIM), lambda i: (0, 0)),
            tab_spec, tab_spec, tab_spec, tab_spec, tab_spec,
        ],
        out_specs=[
            pl.BlockSpec((N_HEADS_A, tm, HEAD_DIM), lambda i: (0, i, 0)),
            pl.BlockSpec((N_IDX_HEADS, tm, LANES), lambda i: (0, i, 0)),
        ],
        out_shape=[
            jax.ShapeDtypeStruct((N_HEADS_A, t, HEAD_DIM), jnp.bfloat16),
            jax.ShapeDtypeStruct((N_IDX_HEADS, t, LANES), jnp.bfloat16),
        ],
        compiler_params=_cparams(("parallel",)),
        name="q_proj",
    )(proj, g_cq.reshape(1, Q_RANK), w_uq, w_uq_idx, g_q_a.reshape(1, HEAD_DIM), cf, sf, ci, sa, sb)


def _ka_kernel(x_ref, g_ref, cf_ref, sf_ref, o_ref):
    x = x_ref[...].astype(jnp.float32)
    g, cf, sf = g_ref[...], cf_ref[...], sf_ref[...]
    for h in range(N_KV_A):
        hk = _head_rms(x[:, h * HEAD_DIM:(h + 1) * HEAD_DIM], g)
        o_ref[:, h * HEAD_DIM:(h + 1) * HEAD_DIM] = _rope_full(hk, cf, sf).astype(o_ref.dtype)


def _ka_norm_rope(proj, col, g_k_a, cf, sf, tm=512):
    t = proj.shape[0]
    tm = min(tm, t)
    tab_spec = pl.BlockSpec((tm, LANES), lambda i: (i, 0))
    return pl.pallas_call(
        _ka_kernel,
        grid=(t // tm,),
        in_specs=[
            pl.BlockSpec((tm, KV_A_W), lambda i: (i, col // KV_A_W)),
            pl.BlockSpec((1, HEAD_DIM), lambda i: (0, 0)),
            tab_spec, tab_spec,
        ],
        out_specs=pl.BlockSpec((tm, KV_A_W), lambda i: (i, 0)),
        out_shape=jax.ShapeDtypeStruct((t, KV_A_W), jnp.bfloat16),
        compiler_params=_cparams(("parallel",)),
        name="ka_norm_rope",
    )(proj, g_k_a.reshape(1, HEAD_DIM), cf, sf)


def _split3_bf16(x):
    hi = x.astype(jnp.bfloat16)
    r1 = x - hi.astype(jnp.float32)
    mid = r1.astype(jnp.bfloat16)
    lo = (r1 - mid.astype(jnp.float32)).astype(jnp.bfloat16)
    return hi, mid, lo


def _misc_kernel(m_ref, c_ref, ci_ref, sa_ref, sb_ref, tri_ref, kidx_ref, wf_ref, wt_ref, frow_ref, carry_ref, *,
                 tiles_per_seq):
    i = pl.program_id(0)

    @pl.when(i % tiles_per_seq == 0)
    def _():
        carry_ref[...] = jnp.zeros_like(carry_ref)

    x = m_ref[...]
    lane = lax.broadcasted_iota(jnp.int32, x.shape, 1)
    is_k = lane < IDX_DIM
    xk = jnp.where(is_k, x, 0.0)
    mu = jnp.sum(xk, axis=-1, keepdims=True) * (1.0 / IDX_DIM)
    dk = jnp.where(is_k, x - mu, 0.0)
    var = jnp.sum(dk * dk, axis=-1, keepdims=True) * (1.0 / IDX_DIM)
    y = dk * lax.rsqrt(var + EPS) * c_ref[0:1, :] + c_ref[1:2, :]
    y = _rope_idx(y, ci_ref[...], sa_ref[...], sb_ref[...])
    kidx_ref[...] = jnp.where(is_k, y, 0.0).astype(kidx_ref.dtype)
    f = x + c_ref[2:3, :]
    log_f = jnp.minimum(f, 0.0) - jnp.log1p(jnp.exp(-jnp.abs(f)))
    hi, mid, lo = _split3_bf16(log_f)
    tri = tri_ref[...]
    csum = (jnp.dot(tri, hi, preferred_element_type=jnp.float32)
            + jnp.dot(tri, mid, preferred_element_type=jnp.float32)
            + jnp.dot(tri, lo, preferred_element_type=jnp.float32))
    csum = csum + carry_ref[0:1, :]
    carry_ref[...] = jnp.broadcast_to(csum[-1:, :], carry_ref.shape)
    is_w = (lane >= MISC_W_LANE) & (lane < MISC_F_LANE)
    wf = jnp.where(is_w, x * c_ref[3:4, :], csum * LOG2E)
    wf_ref[...] = wf
    wf_t = wf.T
    wt_ref[...] = wf_t[MISC_W_LANE:MISC_F_LANE, :]
    frow_ref[0] = wf_t[MISC_F_LANE:MISC_F_LANE + N_HEADS_B, :]


def _misc_post(misc, g_kidx, b_kidx, b_forget, tabs, seq):
    t = misc.shape[0]
    tm = min(KV_TILE, seq)
    idx_w_scale = (N_IDX_HEADS ** -0.5) * (IDX_DIM ** -0.5)
    pad = lambda v, off: jnp.zeros((LANES,), jnp.float32).at[off:off + v.shape[0]].set(v)
    consts = jnp.stack([
        pad(g_kidx, 0), pad(b_kidx, 0), pad(b_forget, MISC_F_LANE),
        pad(jnp.full((N_IDX_HEADS,), idx_w_scale, jnp.float32), MISC_W_LANE),
    ] + [jnp.zeros((LANES,), jnp.float32)] * 4)
    tri = jnp.asarray(np.tril(np.ones((tm, tm), np.float32)), jnp.bfloat16)
    _, _, ci, sa, sb = tabs
    tab_spec = pl.BlockSpec((tm, LANES), lambda i: (i, 0))
    return pl.pallas_call(
        functools.partial(_misc_kernel, tiles_per_seq=seq // tm),
        grid=(t // tm,),
        in_specs=[
            tab_spec,
            pl.BlockSpec((8, LANES), lambda i: (0, 0)),
            tab_spec, tab_spec, tab_spec,
            pl.BlockSpec((tm, tm), lambda i: (0, 0)),
        ],
        out_specs=[
            tab_spec, tab_spec,
            pl.BlockSpec((N_IDX_HEADS, tm), lambda i: (0, i)),
            pl.BlockSpec((1, N_HEADS_B, tm), lambda i: (i, 0, 0)),
        ],
        out_shape=[
            jax.ShapeDtypeStruct((t, LANES), jnp.bfloat16),
            jax.ShapeDtypeStruct((t, LANES), jnp.float32),
            jax.ShapeDtypeStruct((N_IDX_HEADS, t), jnp.float32),
            jax.ShapeDtypeStruct((t // tm, N_HEADS_B, tm), jnp.float32),
        ],
        scratch_shapes=[pltpu.VMEM((8, LANES), jnp.float32)],
        compiler_params=_cparams(("arbitrary",)),
        name="misc_post",
    )(misc, consts, ci, sa, sb, tri)


IDX_HEAD_GROUP = 8
IDX_SLAB = 128
COUNT_ROWS = 64


def _indexer_kernel(q_ref, k_ref, w_ref, o_ref, sc_ref, *, tq, tk, n_kt, top_k):
    qi = pl.program_id(1)
    t0 = qi * tq
    n_vis = (t0 + tq + tk - 1) // tk
    t_row = lax.broadcasted_iota(jnp.int32, (1, tq), 1) + t0
    vis_end = (jnp.right_shift(t_row, CHUNK_SHIFT) + 1) * CHUNK

    def score_tile(c, carry):
        mx, mn = carry
        kt = k_ref[pl.ds(pl.multiple_of(c * tk, tk), tk), :]
        for g in range(N_IDX_HEADS // IDX_HEAD_GROUP):
            qg = q_ref[g * IDX_HEAD_GROUP:(g + 1) * IDX_HEAD_GROUP].reshape(IDX_HEAD_GROUP * tq, LANES)
            st = _nt_dot(kt, qg)
            for r in range(tk // IDX_SLAB):
                rows = slice(r * IDX_SLAB, (r + 1) * IDX_SLAB)
                part = jnp.zeros((IDX_SLAB, tq), jnp.float32)
                for j in range(IDX_HEAD_GROUP):
                    h = g * IDX_HEAD_GROUP + j
                    part = part + jnp.maximum(st[rows, j * tq:(j + 1) * tq], 0.0) * w_ref[h:h + 1, :]
                if g == 0:
                    sc_ref[c, rows, :] = part
                else:
                    sc_ref[c, rows, :] += part
        s_col = lax.broadcasted_iota(jnp.int32, (tk, 1), 0) + c * tk
        adm = s_col < vis_end
        sc = sc_ref[c]
        sc_ref[c] = jnp.where(adm, sc, SCORE_NEG)
        mx = jnp.maximum(mx, jnp.max(jnp.where(adm, sc, SCORE_NEG), axis=0, keepdims=True))
        mn = jnp.minimum(mn, jnp.min(jnp.where(adm, sc, -SCORE_NEG), axis=0, keepdims=True))
        return mx, mn

    hi0, lo0 = lax.fori_loop(
        0, n_vis, score_tile,
        (jnp.full((1, tq), SCORE_NEG, jnp.float32), jnp.full((1, tq), -SCORE_NEG, jnp.float32)))

    def unsettled(state):
        it, _, _, active = state
        return jnp.logical_and(it < SELECT_ITERS, jnp.max(active) > 0.0)

    def bisect(state):
        it, lo, hi, active = state
        mid = 0.5 * (lo + hi)

        def count_tile(c, cnt):
            ge = jnp.where(sc_ref[c] >= mid, 1.0, 0.0)
            return cnt + jnp.sum(ge.reshape(tk // COUNT_ROWS, COUNT_ROWS, tq), axis=0)

        cnt = lax.fori_loop(0, n_vis, count_tile, jnp.zeros((COUNT_ROWS, tq), jnp.float32))
        cnt = jnp.sum(cnt, axis=0, keepdims=True)
        enough = cnt >= float(top_k)
        active = jnp.where(cnt == float(top_k), 0.0, active)
        return it + 1, jnp.where(enough, mid, lo), jnp.where(enough, hi, mid), active

    active0 = jnp.where(vis_end > top_k, 1.0, 0.0)
    _, thr, _, _ = lax.while_loop(unsettled, bisect, (jnp.int32(0), lo0, hi0, active0))

    def write_tile(c, carry):
        sel = jnp.where(sc_ref[c] >= thr, 0.0, MASK_NEG)
        o_ref[0, c] = sel.T.astype(o_ref.dtype)
        return carry

    lax.fori_loop(0, n_vis, write_tile, 0)

    def fill_tile(c, carry):
        o_ref[0, c] = jnp.full((tq, tk), MASK_NEG, o_ref.dtype)
        return carry

    lax.fori_loop(n_vis, n_kt, fill_tile, 0)


def _indexer_mask(q_idx, k_idx, w_t, batch, seq, top_k, tq=256):
    tq, tk = min(tq, seq), min(KV_TILE, seq)
    n_qt, n_kt = seq // tq, seq // tk
    return pl.pallas_call(
        functools.partial(_indexer_kernel, tq=tq, tk=tk, n_kt=n_kt, top_k=top_k),
        grid=(batch, n_qt),
        in_specs=[
            pl.BlockSpec((N_IDX_HEADS, tq, LANES), lambda b, i: (0, b * n_qt + i, 0)),
            pl.BlockSpec((seq, LANES), lambda b, i: (b, 0)),
            pl.BlockSpec((N_IDX_HEADS, tq), lambda b, i: (0, b * n_qt + i)),
        ],
        out_specs=pl.BlockSpec((1, n_kt, tq, tk), lambda b, i: (b, 0, i, 0)),
        out_shape=jax.ShapeDtypeStruct((batch, n_kt, seq, tk), jnp.bfloat16),
        scratch_shapes=[pltpu.VMEM((n_kt, tk, tq), jnp.float32)],
        compiler_params=_cparams(("parallel", "parallel")),
        name="indexer_mask",
    )(q_idx, k_idx, w_t)


def _softmax_update(s, row_shift, v, m_ref, l_ref, acc_ref, e):
    rows, tk = s.shape
    chunks = [s[:, c * LANES:(c + 1) * LANES] for c in range(tk // LANES)]
    mx = functools.reduce(jnp.maximum, chunks)
    mx = jnp.broadcast_to(jnp.max(mx, axis=-1, keepdims=True), (rows, LANES))
    m_prev = m_ref[e]
    m_new = jnp.maximum(m_prev, mx + row_shift)
    alpha = jnp.exp2(m_prev - m_new)
    r = m_new - row_shift
    p = jnp.concatenate([jnp.exp2(c - r) for c in chunks], axis=1).astype(v.dtype)
    v_ones = jnp.concatenate([v, jnp.ones((tk, LANES), v.dtype)], axis=1)
    pv = jnp.dot(p, v_ones, preferred_element_type=jnp.float32)
    acc_ref[e] = alpha * acc_ref[e] + pv[:, :HEAD_DIM]
    l_ref[e] = alpha * l_ref[e] + pv[:, HEAD_DIM:]
    m_ref[e] = m_new


def _softmax_init(m_ref, l_ref, acc_ref):
    m_ref[...] = jnp.full(m_ref.shape, MASK_NEG, jnp.float32)
    l_ref[...] = jnp.zeros(l_ref.shape, jnp.float32)
    acc_ref[...] = jnp.zeros(acc_ref.shape, jnp.float32)


def _dsa_attn_kernel(q_ref, k_ref, v_ref, b_ref, o_ref, m_ref, l_ref, acc_ref, *, tq, tk):
    qi = pl.program_id(1)
    n_need = ((qi + 1) * tq + tk - 1) // tk
    rows = GROUP_A * tq
    _softmax_init(m_ref, l_ref, acc_ref)

    def kv_step(j, carry):
        off = pl.multiple_of(j * tk, tk)
        bias = b_ref[0, j].astype(jnp.float32)[None]
        for g in range(N_KV_A):
            cols = slice(g * HEAD_DIM, (g + 1) * HEAD_DIM)
            q = q_ref[g * GROUP_A:(g + 1) * GROUP_A].reshape(rows, HEAD_DIM)
            s = _nt_dot(q, k_ref[pl.ds(off, tk), cols])
            s = (s.reshape(GROUP_A, tq, tk) + bias).reshape(rows, tk)
            _softmax_update(s, 0.0, v_ref[pl.ds(off, tk), cols], m_ref, l_ref, acc_ref, g)
        return carry

    lax.fori_loop(0, n_need, kv_step, 0)

    for g in range(N_KV_A):
        out = acc_ref[g] / l_ref[g]
        for h in range(GROUP_A):
            c0 = (g * GROUP_A + h) * HEAD_DIM
            o_ref[:, c0:c0 + HEAD_DIM] = out[h * tq:(h + 1) * tq].astype(o_ref.dtype)


def _dsa_attention(q_a, k_a, proj, v_col, bias, batch, seq, tq=512):
    tk = bias.shape[-1]
    tq = min(tq, seq)
    n_qt, n_kt = seq // tq, seq // tk
    rows = GROUP_A * tq
    return pl.pallas_call(
        functools.partial(_dsa_attn_kernel, tq=tq, tk=tk),
        grid=(batch, n_qt),
        in_specs=[
            pl.BlockSpec((N_HEADS_A, tq, HEAD_DIM), lambda b, i: (0, b * n_qt + i, 0)),
            pl.BlockSpec((seq, KV_A_W), lambda b, i: (b, 0)),
            pl.BlockSpec((seq, KV_A_W), lambda b, i: (b, v_col // KV_A_W)),
            pl.BlockSpec((1, n_kt, tq, tk), lambda b, i: (b, 0, i, 0)),
        ],
        out_specs=pl.BlockSpec((tq, W_A), lambda b, i: (b * n_qt + i, 0)),
        out_shape=jax.ShapeDtypeStruct((batch * seq, W_A), jnp.bfloat16),
        scratch_shapes=[
            pltpu.VMEM((N_KV_A, rows, LANES), jnp.float32),
            pltpu.VMEM((N_KV_A, rows, LANES), jnp.float32),
            pltpu.VMEM((N_KV_A, rows, HEAD_DIM), jnp.float32),
        ],
        compiler_params=_cparams(("parallel", "parallel")),
        name="dsa_attention",
    )(q_a, k_a, proj, bias)


FOX_HEADS_PER_STEP = 4


def _fox_kernel(q_ref, k_ref, v_ref, gq_ref, gk_ref, fq_ref, fk_ref, o_ref, kn_ref, m_ref, l_ref, acc_ref, *, tile):
    hp, qi = pl.program_id(1), pl.program_id(2)
    n_t = kn_ref.shape[0] // tile

    @pl.when(qi == 0)
    def _():
        def norm_keys(j, carry):
            off = pl.multiple_of(j * tile, tile)
            for e in range(FOX_HEADS_PER_STEP):
                cols = slice(e * HEAD_DIM, (e + 1) * HEAD_DIM)
                k = k_ref[pl.ds(off, tile), cols].astype(jnp.float32)
                kn_ref[pl.ds(off, tile), cols] = _head_rms(k, gk_ref[...]).astype(kn_ref.dtype)
            return carry

        lax.fori_loop(0, n_t, norm_keys, 0)

    lane = lax.broadcasted_iota(jnp.int32, (tile, LANES), 1)
    heads = [hp * FOX_HEADS_PER_STEP + e for e in range(FOX_HEADS_PER_STEP)]
    fq = [jnp.broadcast_to(jnp.sum(jnp.where(lane == MISC_F_LANE + h, fq_ref[...], 0.0), axis=1, keepdims=True),
                           (tile, LANES)) for h in heads]
    qn = [_head_rms(q_ref[:, e * HEAD_DIM:(e + 1) * HEAD_DIM].astype(jnp.float32), gq_ref[...]).astype(kn_ref.dtype)
          for e in range(FOX_HEADS_PER_STEP)]
    _softmax_init(m_ref, l_ref, acc_ref)

    def kv_step(j, diagonal):
        off = pl.multiple_of(j * tile, tile)
        for e, h in enumerate(heads):
            cols = slice(e * HEAD_DIM, (e + 1) * HEAD_DIM)
            s = _nt_dot(qn[e], kn_ref[pl.ds(off, tile), cols]) - fk_ref[j, pl.ds(h, 1), :]
            if diagonal:
                t_pos = lax.broadcasted_iota(jnp.int32, (tile, tile), 0)
                s_pos = lax.broadcasted_iota(jnp.int32, (tile, tile), 1)
                s = jnp.where(s_pos <= t_pos, s, MASK_NEG)
            _softmax_update(s, fq[e], v_ref[pl.ds(off, tile), cols], m_ref, l_ref, acc_ref, e)

    def full_step(j, carry):
        kv_step(j, False)
        return carry

    lax.fori_loop(0, qi, full_step, 0)
    kv_step(qi, True)

    for e in range(FOX_HEADS_PER_STEP):
        o_ref[:, e * HEAD_DIM:(e + 1) * HEAD_DIM] = (acc_ref[e] / l_ref[e]).astype(o_ref.dtype)


def _fox_attention(proj, q_col, g_q, g_k, wf, f_row, batch, seq):
    tile = min(KV_TILE, seq)
    n_t = seq // tile
    wb = FOX_HEADS_PER_STEP * HEAD_DIM
    q_blk, k_blk, v_blk = q_col // wb, (q_col + W_B) // wb, (q_col + 2 * W_B) // wb
    gain_spec = pl.BlockSpec((1, HEAD_DIM), lambda b, h, i: (0, 0))
    return pl.pallas_call(
        functools.partial(_fox_kernel, tile=tile),
        grid=(batch, N_HEADS_B // FOX_HEADS_PER_STEP, n_t),
        in_specs=[
            pl.BlockSpec((tile, wb), lambda b, h, i: (b * n_t + i, q_blk + h)),
            pl.BlockSpec((seq, wb), lambda b, h, i: (b, k_blk + h)),
            pl.BlockSpec((seq, wb), lambda b, h, i: (b, v_blk + h)),
            gain_spec, gain_spec,
            pl.BlockSpec((tile, LANES), lambda b, h, i: (b * n_t + i, 0)),
            pl.BlockSpec((n_t, N_HEADS_B, tile), lambda b, h, i: (b, 0, 0)),
        ],
        out_specs=pl.BlockSpec((tile, wb), lambda b, h, i: (b * n_t + i, h)),
        out_shape=jax.ShapeDtypeStruct((batch * seq, W_B), jnp.bfloat16),
        scratch_shapes=[
            pltpu.VMEM((seq, wb), jnp.bfloat16),
            pltpu.VMEM((FOX_HEADS_PER_STEP, tile, LANES), jnp.float32),
            pltpu.VMEM((FOX_HEADS_PER_STEP, tile, LANES), jnp.float32),
            pltpu.VMEM((FOX_HEADS_PER_STEP, tile, HEAD_DIM), jnp.float32),
        ],
        compiler_params=_cparams(("parallel", "parallel", "arbitrary")),
        name="fox_attention",
    )(proj, proj, proj, g_q.reshape(1, HEAD_DIM), g_k.reshape(1, HEAD_DIM), wf, f_row)


def _layer(x, p, pos_col, batch, seq, g_attn, w_in, g_cq, w_uq, w_uq_idx, g_kidx, b_kidx, g_q_a, g_k_a, b_forget,
           g_q_b, g_k_b, w_up_a, w_up_b, w_o, g_ffn, w_ffn_gate, w_ffn_up, w_ffn_down, g_ple, w_ple, w_ple_gate):
    d = x.shape[1]
    bf = jnp.bfloat16
    top_k = min(INDEX_TOPK, seq // 4)

    o_kidx = Q_RANK + 2 * KV_A_W
    o_qb = o_kidx + IDX_DIM + N_IDX_HEADS
    o_fb = o_qb + 3 * W_B
    o_ga = o_fb + N_HEADS_B
    w_main = _stage_regroup(w_in, [(0, o_kidx), (o_qb, o_fb - o_qb), (o_ga, 2 * d)])
    w_misc = jnp.concatenate(
        [w_in[:, o_kidx:o_qb], w_in[:, o_fb:o_ga], jnp.zeros((d, LANES - (o_qb - o_kidx) - N_HEADS_B), w_in.dtype)],
        axis=1).astype(bf)
    col_ka, col_va = Q_RANK, Q_RANK + KV_A_W
    col_qb = Q_RANK + 2 * KV_A_W
    col_vb = col_qb + 2 * W_B
    col_ga = col_vb + W_B
    col_gb = col_ga + d
    d_ff = w_ffn_gate.shape[1]
    ff_tile = 512
    d_ff_pad = -(-d_ff // (2 * ff_tile)) * (2 * ff_tile)
    w_g = _stage_pad(w_ffn_gate, d, d_ff_pad)
    w_u = _stage_pad(w_ffn_up, d, d_ff_pad)
    w_d = _stage_pad(w_ffn_down, d_ff_pad, d)

    tabs = _rope_tables(pos_col)
    cf, sf = tabs[0], tabs[1]

    h = _rmsnorm(x, g_attn)
    proj = _matmul(h, w_main, bf, name="proj_main")
    misc = _matmul(h, w_misc, jnp.float32, tn=LANES, name="proj_misc")

    q_a, q_idx = _q_proj(proj, g_cq, w_uq.astype(bf), w_uq_idx.astype(bf), g_q_a, tabs)
    k_a = _ka_norm_rope(proj, col_ka, g_k_a, cf, sf)
    k_idx, wf, w_t, f_row = _misc_post(misc, g_kidx, b_kidx, b_forget, tabs, seq)
    bias = _indexer_mask(q_idx, k_idx, w_t, batch, seq, top_k)
    o_a = _dsa_attention(q_a, k_a, proj, col_va, bias, batch, seq)

    o_b = _fox_attention(proj, col_qb, g_q_b * Q_SCALE, g_k_b, wf, f_row, batch, seq)

    merged = _merge(o_a, o_b, w_up_a.astype(bf), w_up_b.astype(bf), proj, col_ga, col_gb)
    x = _matmul_residual(merged, w_o.astype(bf), x, name="out_proj")

    h = _rmsnorm(x, g_ffn)
    u = _swiglu(h, w_g, w_u, tn=ff_tile)
    x = _matmul_residual_ksplit(u, w_d, x, tk=d_ff_pad // 4, name="ffn_down")

    h = _rmsnorm(x, g_ple)
    x = _ple(h, w_ple_gate.astype(bf), p.astype(bf), w_ple.astype(bf), x)
    return x


def kernel(x, p, positions, g_attn, w_in, g_cq, w_uq, w_uq_idx, g_kidx, b_kidx, g_q_a, g_k_a, b_forget, g_q_b, g_k_b,
           w_up_a, w_up_b, w_o, g_ffn, w_ffn_gate, w_ffn_up, w_ffn_down, g_ple, w_ple, w_ple_gate):
    batch, seq, d = x.shape
    depth = w_in.shape[0]
    xf = x.reshape(batch * seq, d)
    pos_col = positions.reshape(batch * seq, 1)
    for i in range(depth):
        xf = _layer(xf, p[i].reshape(batch * seq, -1), pos_col, batch, seq, g_attn[i], w_in[i], g_cq[i], w_uq[i],
                    w_uq_idx[i], g_kidx[i], b_kidx[i], g_q_a[i], g_k_a[i], b_forget[i], g_q_b[i], g_k_b[i],
                    w_up_a[i], w_up_b[i], w_o[i], g_ffn[i], w_ffn_gate[i], w_ffn_up[i], w_ffn_down[i], g_ple[i],
                    w_ple[i], w_ple_gate[i])
    return xf.reshape(batch, seq, d)
```

```python
import functools

import numpy as np
import jax
import jax.numpy as jnp
from jax import lax
from jax.experimental import pallas as pl
from jax.experimental.pallas import tpu as pltpu

CHUNK = 64
CHUNK_SHIFT = CHUNK.bit_length() - 1
HEAD_DIM = 128
ROPE_THETA = 10000.0
EPS = 1e-6
N_HEADS_A = 16
N_KV_A = 2
Q_RANK = 1024
N_IDX_HEADS = 32
IDX_DIM = 64
IDX_ROPE_DIM = 32
INDEX_TOPK = 256
N_HEADS_B = 16
PLE_DIM = 256

LANES = 128
KV_A_W = N_KV_A * HEAD_DIM
W_A = N_HEADS_A * HEAD_DIM
W_B = N_HEADS_B * HEAD_DIM
GROUP_A = N_HEADS_A // N_KV_A

MISC_W_LANE = IDX_DIM
MISC_F_LANE = IDX_DIM + N_IDX_HEADS

MASK_NEG = -1e30
SCORE_NEG = -3e38
SELECT_ITERS = 32
LOG2E = 1.4426950408889634
Q_SCALE = HEAD_DIM ** -0.5 * LOG2E
KV_TILE = 512
FFN_TILE = 256
VMEM_LIMIT = 56 * 1024 * 1024


def _cparams(sem):
    return pltpu.CompilerParams(dimension_semantics=sem, vmem_limit_bytes=VMEM_LIMIT)


def _sigmoid(x):
    return 1.0 / (1.0 + jnp.exp(-x))


def _nt_dot(a, b):
    return lax.dot_general(a, b, (((1,), (1,)), ((), ())), preferred_element_type=jnp.float32)


def _stage_regroup_kernel(w_ref, o_ref):
    o_ref[...] = w_ref[...].T.astype(o_ref.dtype)


def _stage_regroup(w_t, segments, tn=512, tk=1024):
    n_src, k = w_t.shape
    lags, dst = [], 0
    for src, width in segments:
        assert width % tn == 0 and src % 16 == 0 and src + width <= n_src
        lags.append((dst // tn, src - dst))
        dst += width
    tk = min(tk, k)

    def src_row(j):
        lag = lags[0][1]
        for first_tile, seg_lag in lags[1:]:
            lag = jnp.where(j >= first_tile, seg_lag, lag)
        return pl.multiple_of(j * tn + lag, 16)

    return pl.pallas_call(
        _stage_regroup_kernel,
        grid=(dst // tn, k // tk),
        in_specs=[pl.BlockSpec((pl.Element(tn), pl.Element(tk)),
                               lambda j, kk: (src_row(j), pl.multiple_of(kk * tk, tk)))],
        out_specs=pl.BlockSpec((tk, tn), lambda j, kk: (kk, j)),
        out_shape=jax.ShapeDtypeStruct((k, dst), jnp.bfloat16),
        compiler_params=_cparams(("parallel", "parallel")),
        name="stage_regroup",
    )(w_t)


def _stage_misc_kernel(a_ref, b_ref, o_ref):
    a, b = a_ref[...], b_ref[...]
    pad = jnp.zeros((LANES - a.shape[0] - b.shape[0], a.shape[1]), a.dtype)
    o_ref[...] = jnp.concatenate([a, b, pad], axis=0).T.astype(o_ref.dtype)


def _stage_misc(w_t, start_a, n_a, start_b, n_b, tk=1024):
    k = w_t.shape[1]
    tk = min(tk, k)
    col = lambda kk: pl.multiple_of(kk * tk, tk)
    return pl.pallas_call(
        _stage_misc_kernel,
        grid=(k // tk,),
        in_specs=[
            pl.BlockSpec((pl.Element(n_a), pl.Element(tk)), lambda kk: (start_a, col(kk))),
            pl.BlockSpec((pl.Element(n_b), pl.Element(tk)), lambda kk: (start_b, col(kk))),
        ],
        out_specs=pl.BlockSpec((tk, LANES), lambda kk: (kk, 0)),
        out_shape=jax.ShapeDtypeStruct((k, LANES), jnp.bfloat16),
        compiler_params=_cparams(("parallel",)),
        name="stage_misc",
    )(w_t, w_t)


def _rmsnorm_kernel(x_ref, g_ref, o_ref):
    x = x_ref[...]
    ms = jnp.mean(x * x, axis=-1, keepdims=True)
    o_ref[...] = (x * lax.rsqrt(ms + EPS) * g_ref[...]).astype(o_ref.dtype)


def _rmsnorm(x, g, tm=256):
    t, d = x.shape
    tm = min(tm, t)
    return pl.pallas_call(
        _rmsnorm_kernel,
        grid=(t // tm,),
        in_specs=[pl.BlockSpec((tm, d), lambda i: (i, 0)), pl.BlockSpec((1, d), lambda i: (0, 0))],
        out_specs=pl.BlockSpec((tm, d), lambda i: (i, 0)),
        out_shape=jax.ShapeDtypeStruct((t, d), jnp.bfloat16),
        compiler_params=_cparams(("parallel",)),
        name="rmsnorm",
    )(x, g.reshape(1, d))


def _mm_kernel(a_ref, w_ref, o_ref):
    o_ref[...] = jnp.dot(a_ref[...], w_ref[...], preferred_element_type=jnp.float32).astype(o_ref.dtype)


def _matmul(a, w, out_dtype, tm=1024, tn=512, name="matmul"):
    t, k = a.shape
    n = w.shape[1]
    tm, tn = min(tm, t), min(tn, n)
    return pl.pallas_call(
        _mm_kernel,
        grid=(t // tm, n // tn),
        in_specs=[pl.BlockSpec((tm, k), lambda i, j: (i, 0)), pl.BlockSpec((k, tn), lambda i, j: (0, j))],
        out_specs=pl.BlockSpec((tm, tn), lambda i, j: (i, j)),
        out_shape=jax.ShapeDtypeStruct((t, n), out_dtype),
        compiler_params=_cparams(("parallel", "parallel")),
        name=name,
    )(a, w)


def _mm_res_kernel(a_ref, w_ref, r_ref, o_ref):
    o_ref[...] = r_ref[...] + jnp.dot(a_ref[...], w_ref[...], preferred_element_type=jnp.float32)


def _matmul_residual(a, w, r, tm=1024, tn=512, name="matmul_residual"):
    t, k = a.shape
    n = w.shape[1]
    tm, tn = min(tm, t), min(tn, n)
    return pl.pallas_call(
        _mm_res_kernel,
        grid=(t // tm, n // tn),
        in_specs=[
            pl.BlockSpec((tm, k), lambda i, j: (i, 0)),
            pl.BlockSpec((k, tn), lambda i, j: (0, j)),
            pl.BlockSpec((tm, tn), lambda i, j: (i, j)),
        ],
        out_specs=pl.BlockSpec((tm, tn), lambda i, j: (i, j)),
        out_shape=jax.ShapeDtypeStruct((t, n), jnp.float32),
        compiler_params=_cparams(("parallel", "parallel")),
        name=name,
    )(a, w, r)


def _swiglu_kernel(a_ref, wg_ref, wu_ref, o_ref):
    a = a_ref[...]
    g = jnp.dot(a, wg_ref[...], preferred_element_type=jnp.float32)
    u = jnp.dot(a, wu_ref[...], preferred_element_type=jnp.float32)
    o_ref[...] = (g * _sigmoid(g) * u).astype(o_ref.dtype)


def _swiglu(a, wg, wu, tm=1024, tn=512):
    t, k = a.shape
    n = wg.shape[1]
    tm, tn = min(tm, t), min(tn, n)
    return pl.pallas_call(
        _swiglu_kernel,
        grid=(t // tm, n // tn),
        in_specs=[
            pl.BlockSpec((tm, k), lambda i, j: (i, 0)),
            pl.BlockSpec((k, tn), lambda i, j: (0, j)),
            pl.BlockSpec((k, tn), lambda i, j: (0, j)),
        ],
        out_specs=pl.BlockSpec((tm, tn), lambda i, j: (i, j)),
        out_shape=jax.ShapeDtypeStruct((t, n), jnp.bfloat16),
        compiler_params=_cparams(("parallel", "parallel")),
        name="swiglu",
    )(a, wg, wu)


def _merge_kernel(oa_ref, ob_ref, wa_ref, wb_ref, ga_ref, gb_ref, o_ref):
    a = jnp.dot(oa_ref[...], wa_ref[...], preferred_element_type=jnp.float32)
    b = jnp.dot(ob_ref[...], wb_ref[...], preferred_element_type=jnp.float32)
    ga = _sigmoid(ga_ref[...].astype(jnp.float32))
    gb = _sigmoid(gb_ref[...].astype(jnp.float32))
    o_ref[...] = (ga * a + gb * b).astype(o_ref.dtype)


def _merge(o_a, o_b, w_up_a, w_up_b, proj, ga_col, gb_col, tm=1024, tn=512):
    t, ka = o_a.shape
    kb = o_b.shape[1]
    n = w_up_a.shape[1]
    tm, tn = min(tm, t), min(tn, n)
    ga_blk, gb_blk = ga_col // tn, gb_col // tn
    return pl.pallas_call(
        _merge_kernel,
        grid=(t // tm, n // tn),
        in_specs=[
            pl.BlockSpec((tm, ka), lambda i, j: (i, 0)),
            pl.BlockSpec((tm, kb), lambda i, j: (i, 0)),
            pl.BlockSpec((ka, tn), lambda i, j: (0, j)),
            pl.BlockSpec((kb, tn), lambda i, j: (0, j)),
            pl.BlockSpec((tm, tn), lambda i, j: (i, ga_blk + j)),
            pl.BlockSpec((tm, tn), lambda i, j: (i, gb_blk + j)),
        ],
        out_specs=pl.BlockSpec((tm, tn), lambda i, j: (i, j)),
        out_shape=jax.ShapeDtypeStruct((t, n), jnp.bfloat16),
        compiler_params=_cparams(("parallel", "parallel")),
        name="merge",
    )(o_a, o_b, w_up_a, w_up_b, proj, proj)


def _ple_kernel(h_ref, wg_ref, p_ref, wp_ref, r_ref, o_ref):
    g = jnp.dot(h_ref[...], wg_ref[...], preferred_element_type=jnp.float32)
    e = jnp.dot(p_ref[...], wp_ref[...], preferred_element_type=jnp.float32)
    o_ref[...] = r_ref[...] + _sigmoid(g) * e


def _ple(h, w_gate, p, w_ple, r, tm=1024, tn=512):
    t, k = h.shape
    kp = p.shape[1]
    n = w_gate.shape[1]
    tm, tn = min(tm, t), min(tn, n)
    return pl.pallas_call(
        _ple_kernel,
        grid=(t // tm, n // tn),
        in_specs=[
            pl.BlockSpec((tm, k), lambda i, j: (i, 0)),
            pl.BlockSpec((k, tn), lambda i, j: (0, j)),
            pl.BlockSpec((tm, kp), lambda i, j: (i, 0)),
            pl.BlockSpec((kp, tn), lambda i, j: (0, j)),
            pl.BlockSpec((tm, tn), lambda i, j: (i, j)),
        ],
        out_specs=pl.BlockSpec((tm, tn), lambda i, j: (i, j)),
        out_shape=jax.ShapeDtypeStruct((t, n), jnp.float32),
        compiler_params=_cparams(("parallel", "parallel")),
        name="ple",
    )(h, w_gate, p, w_ple, r)


def _rope_consts():
    half = HEAD_DIM // 2
    inv_full = jnp.power(ROPE_THETA, -jnp.arange(half, dtype=jnp.float32) * (2.0 / HEAD_DIM))
    half_i = IDX_ROPE_DIM // 2
    inv_idx = jnp.power(ROPE_THETA, -jnp.arange(half_i, dtype=jnp.float32) * (2.0 / IDX_ROPE_DIM))
    zeros = jnp.zeros((IDX_DIM - IDX_ROPE_DIM,), jnp.float32)
    sign = np.concatenate([-np.ones(half, np.float32), np.ones(half, np.float32)])
    mask_a = np.zeros(IDX_DIM, np.float32)
    mask_a[:half_i] = -1.0
    mask_b = np.zeros(IDX_DIM, np.float32)
    mask_b[half_i:IDX_ROPE_DIM] = 1.0
    reps = LANES // IDX_DIM
    rows = [
        jnp.concatenate([inv_full, inv_full]),
        jnp.asarray(sign),
        jnp.tile(jnp.concatenate([inv_idx, inv_idx, zeros]), reps),
        jnp.asarray(np.tile(mask_a, reps)),
        jnp.asarray(np.tile(mask_b, reps)),
    ]
    rows += [jnp.zeros((LANES,), jnp.float32)] * 3
    return jnp.stack(rows)


def _rope_tables_kernel(pos_ref, c_ref, cf_ref, sf_ref, ci_ref, sa_ref, sb_ref):
    pos = pos_ref[...].astype(jnp.float32)
    ang = pos * c_ref[0:1, :]
    cf_ref[...] = jnp.cos(ang)
    sf_ref[...] = jnp.sin(ang) * c_ref[1:2, :]
    ang_i = pos * c_ref[2:3, :]
    ci_ref[...] = jnp.cos(ang_i)
    s_i = jnp.sin(ang_i)
    sa_ref[...] = s_i * c_ref[3:4, :]
    sb_ref[...] = s_i * c_ref[4:5, :]


def _rope_tables(pos_col, tm=512):
    t = pos_col.shape[0]
    tm = min(tm, t)
    tab = jax.ShapeDtypeStruct((t, LANES), jnp.float32)
    spec = pl.BlockSpec((tm, LANES), lambda i: (i, 0))
    return pl.pallas_call(
        _rope_tables_kernel,
        grid=(t // tm,),
        in_specs=[pl.BlockSpec((tm, 1), lambda i: (i, 0)), pl.BlockSpec((8, LANES), lambda i: (0, 0))],
        out_specs=[spec] * 5,
        out_shape=[tab] * 5,
        compiler_params=_cparams(("parallel",)),
        name="rope_tables",
    )(pos_col, _rope_consts())


def _rope_full(x, cf, sf):
    return x * cf + pltpu.roll(x, HEAD_DIM // 2, axis=1) * sf


def _rope_idx(x, ci, sa, sb):
    half = IDX_ROPE_DIM // 2
    return x * ci + pltpu.roll(x, LANES - half, axis=1) * sa + pltpu.roll(x, half, axis=1) * sb


def _head_rms(x, g):
    ms = jnp.mean(x * x, axis=-1, keepdims=True)
    return x * lax.rsqrt(ms + EPS) * g


def _q_proj_kernel(cq_ref, gcq_ref, wa_ref, wi_ref, gqa_ref, cf_ref, sf_ref, ci_ref, sa_ref, sb_ref, qa_ref, qi_ref):
    cq = cq_ref[...].astype(jnp.float32)
    ms = jnp.mean(cq * cq, axis=-1, keepdims=True)
    cq = (cq * lax.rsqrt(ms + EPS) * gcq_ref[...]).astype(jnp.bfloat16)
    cf, sf = cf_ref[...], sf_ref[...]
    gqa = gqa_ref[...]
    for j in range(N_HEADS_A // 2):
        acc = jnp.dot(cq, wa_ref[:, j * 2 * HEAD_DIM:(j + 1) * 2 * HEAD_DIM], preferred_element_type=jnp.float32)
        for e in range(2):
            hq = _head_rms(acc[:, e * HEAD_DIM:(e + 1) * HEAD_DIM], gqa)
            qa_ref[2 * j + e] = (_rope_full(hq, cf, sf) * Q_SCALE).astype(qa_ref.dtype)
    ci, sa, sb = ci_ref[...], sa_ref[...], sb_ref[...]
    low_half = lax.broadcasted_iota(jnp.int32, (cq.shape[0], LANES), 1) < IDX_DIM
    per_dot = 2 * LANES // IDX_DIM
    for j in range(N_IDX_HEADS // per_dot):
        acc = jnp.dot(cq, wi_ref[:, j * 2 * LANES:(j + 1) * 2 * LANES], preferred_element_type=jnp.float32)
        for c in range(2):
            pair = _rope_idx(acc[:, c * LANES:(c + 1) * LANES], ci, sa, sb)
            for e, head in enumerate((pair, pltpu.roll(pair, IDX_DIM, axis=1))):
                qi_ref[per_dot * j + 2 * c + e] = jnp.where(low_half, head, 0.0).astype(qi_ref.dtype)


def _q_proj(proj, g_cq, w_uq, w_uq_idx, g_q_a, tabs, tm=512):
    t = proj.shape[0]
    tm = min(tm, t)
    cf, sf, ci, sa, sb = tabs
    tab_spec = pl.BlockSpec((tm, LANES), lambda i: (i, 0))
    return pl.pallas_call(
        _q_proj_kernel,
        grid=(t // tm,),
        in_specs=[
            pl.BlockSpec((tm, Q_RANK), lambda i: (i, 0)),
            pl.BlockSpec((1, Q_RANK), lambda i: (0, 0)),
            pl.BlockSpec((Q_RANK, W_A), lambda i: (0, 0)),
            pl.BlockSpec((Q_RANK, N_IDX_HEADS * IDX_DIM), lambda i: (0, 0)),
            pl.BlockSpec((1, HEAD_DIM), lambda i: (0, 0)),
            tab_spec, tab_spec, tab_spec, tab_spec, tab_spec,
        ],
        out_specs=[
            pl.BlockSpec((N_HEADS_A, tm, HEAD_DIM), lambda i: (0, i, 0)),
            pl.BlockSpec((N_IDX_HEADS, tm, LANES), lambda i: (0, i, 0)),
        ],
        out_shape=[
            jax.ShapeDtypeStruct((N_HEADS_A, t, HEAD_DIM), jnp.bfloat16),
            jax.ShapeDtypeStruct((N_IDX_HEADS, t, LANES), jnp.bfloat16),
        ],
        compiler_params=_cparams(("parallel",)),
        name="q_proj",
    )(proj, g_cq.reshape(1, Q_RANK), w_uq, w_uq_idx, g_q_a.reshape(1, HEAD_DIM), cf, sf, ci, sa, sb)


def _ka_kernel(x_ref, g_ref, cf_ref, sf_ref, o_ref):
    x = x_ref[...].astype(jnp.float32)
    g, cf, sf = g_ref[...], cf_ref[...], sf_ref[...]
    for h in range(N_KV_A):
        hk = _head_rms(x[:, h * HEAD_DIM:(h + 1) * HEAD_DIM], g)
        o_ref[:, h * HEAD_DIM:(h + 1) * HEAD_DIM] = _rope_full(hk, cf, sf).astype(o_ref.dtype)


def _ka_norm_rope(proj, col, g_k_a, cf, sf, tm=512):
    t = proj.shape[0]
    tm = min(tm, t)
    tab_spec = pl.BlockSpec((tm, LANES), lambda i: (i, 0))
    return pl.pallas_call(
        _ka_kernel,
        grid=(t // tm,),
        in_specs=[
            pl.BlockSpec((tm, KV_A_W), lambda i: (i, col // KV_A_W)),
            pl.BlockSpec((1, HEAD_DIM), lambda i: (0, 0)),
            tab_spec, tab_spec,
        ],
        out_specs=pl.BlockSpec((tm, KV_A_W), lambda i: (i, 0)),
        out_shape=jax.ShapeDtypeStruct((t, KV_A_W), jnp.bfloat16),
        compiler_params=_cparams(("parallel",)),
        name="ka_norm_rope",
    )(proj, g_k_a.reshape(1, HEAD_DIM), cf, sf)


def _split3_bf16(x):
    hi = x.astype(jnp.bfloat16)
    r1 = x - hi.astype(jnp.float32)
    mid = r1.astype(jnp.bfloat16)
    lo = (r1 - mid.astype(jnp.float32)).astype(jnp.bfloat16)
    return hi, mid, lo


def _misc_kernel(m_ref, c_ref, ci_ref, sa_ref, sb_ref, tri_ref, kidx_ref, wf_ref, wt_ref, frow_ref, carry_ref, *,
                 tiles_per_seq):
    i = pl.program_id(0)

    @pl.when(i % tiles_per_seq == 0)
    def _():
        carry_ref[...] = jnp.zeros_like(carry_ref)

    x = m_ref[...]
    lane = lax.broadcasted_iota(jnp.int32, x.shape, 1)
    is_k = lane < IDX_DIM
    xk = jnp.where(is_k, x, 0.0)
    mu = jnp.sum(xk, axis=-1, keepdims=True) * (1.0 / IDX_DIM)
    dk = jnp.where(is_k, x - mu, 0.0)
    var = jnp.sum(dk * dk, axis=-1, keepdims=True) * (1.0 / IDX_DIM)
    y = dk * lax.rsqrt(var + EPS) * c_ref[0:1, :] + c_ref[1:2, :]
    y = _rope_idx(y, ci_ref[...], sa_ref[...], sb_ref[...])
    kidx_ref[...] = jnp.where(is_k, y, 0.0).astype(kidx_ref.dtype)
    f = x + c_ref[2:3, :]
    log_f = jnp.minimum(f, 0.0) - jnp.log1p(jnp.exp(-jnp.abs(f)))
    hi, mid, lo = _split3_bf16(log_f)
    tri = tri_ref[...]
    csum = (jnp.dot(tri, hi, preferred_element_type=jnp.float32)
            + jnp.dot(tri, mid, preferred_element_type=jnp.float32)
            + jnp.dot(tri, lo, preferred_element_type=jnp.float32))
    csum = csum + carry_ref[0:1, :]
    carry_ref[...] = jnp.broadcast_to(csum[-1:, :], carry_ref.shape)
    is_w = (lane >= MISC_W_LANE) & (lane < MISC_F_LANE)
    wf = jnp.where(is_w, x * c_ref[3:4, :], csum * LOG2E)
    wf_ref[...] = wf
    wf_t = wf.T
    wt_ref[...] = wf_t[MISC_W_LANE:MISC_F_LANE, :]
    frow_ref[0] = wf_t[MISC_F_LANE:MISC_F_LANE + N_HEADS_B, :]


def _misc_post(misc, g_kidx, b_kidx, b_forget, tabs, seq):
    t = misc.shape[0]
    tm = min(KV_TILE, seq)
    idx_w_scale = (N_IDX_HEADS ** -0.5) * (IDX_DIM ** -0.5)
    pad = lambda v, off: jnp.zeros((LANES,), jnp.float32).at[off:off + v.shape[0]].set(v)
    consts = jnp.stack([
        pad(g_kidx, 0), pad(b_kidx, 0), pad(b_forget, MISC_F_LANE),
        pad(jnp.full((N_IDX_HEADS,), idx_w_scale, jnp.float32), MISC_W_LANE),
    ] + [jnp.zeros((LANES,), jnp.float32)] * 4)
    tri = jnp.asarray(np.tril(np.ones((tm, tm), np.float32)), jnp.bfloat16)
    _, _, ci, sa, sb = tabs
    tab_spec = pl.BlockSpec((tm, LANES), lambda i: (i, 0))
    return pl.pallas_call(
        functools.partial(_misc_kernel, tiles_per_seq=seq // tm),
        grid=(t // tm,),
        in_specs=[
            tab_spec,
            pl.BlockSpec((8, LANES), lambda i: (0, 0)),
            tab_spec, tab_spec, tab_spec,
            pl.BlockSpec((tm, tm), lambda i: (0, 0)),
        ],
        out_specs=[
            tab_spec, tab_spec,
            pl.BlockSpec((N_IDX_HEADS, tm), lambda i: (0, i)),
            pl.BlockSpec((1, N_HEADS_B, tm), lambda i: (i, 0, 0)),
        ],
        out_shape=[
            jax.ShapeDtypeStruct((t, LANES), jnp.bfloat16),
            jax.ShapeDtypeStruct((t, LANES), jnp.float32),
            jax.ShapeDtypeStruct((N_IDX_HEADS, t), jnp.float32),
            jax.ShapeDtypeStruct((t // tm, N_HEADS_B, tm), jnp.float32),
        ],
        scratch_shapes=[pltpu.VMEM((8, LANES), jnp.float32)],
        compiler_params=_cparams(("arbitrary",)),
        name="misc_post",
    )(misc, consts, ci, sa, sb, tri)


IDX_HEAD_GROUP = 8
IDX_SLAB = 128
COUNT_ROWS = 64


def _indexer_kernel(q_ref, k_ref, w_ref, o_ref, sc_ref, *, tq, tk, n_kt, top_k):
    qi = pl.program_id(1)
    t0 = qi * tq
    n_vis = (t0 + tq + tk - 1) // tk
    t_row = lax.broadcasted_iota(jnp.int32, (1, tq), 1) + t0
    vis_end = (jnp.right_shift(t_row, CHUNK_SHIFT) + 1) * CHUNK

    def score_tile(c, carry):
        mx, mn = carry
        kt = k_ref[pl.ds(pl.multiple_of(c * tk, tk), tk), :]
        for g in range(N_IDX_HEADS // IDX_HEAD_GROUP):
            qg = q_ref[g * IDX_HEAD_GROUP:(g + 1) * IDX_HEAD_GROUP].reshape(IDX_HEAD_GROUP * tq, LANES)
            st = _nt_dot(kt, qg)
            for r in range(tk // IDX_SLAB):
                rows = slice(r * IDX_SLAB, (r + 1) * IDX_SLAB)
                part = jnp.zeros((IDX_SLAB, tq), jnp.float32)
                for j in range(IDX_HEAD_GROUP):
                    h = g * IDX_HEAD_GROUP + j
                    part = part + jnp.maximum(st[rows, j * tq:(j + 1) * tq], 0.0) * w_ref[h:h + 1, :]
                if g == 0:
                    sc_ref[c, rows, :] = part
                else:
                    sc_ref[c, rows, :] += part
        s_col = lax.broadcasted_iota(jnp.int32, (tk, 1), 0) + c * tk
        adm = s_col < vis_end
        sc = sc_ref[c]
        sc_ref[c] = jnp.where(adm, sc, SCORE_NEG)
        mx = jnp.maximum(mx, jnp.max(jnp.where(adm, sc, SCORE_NEG), axis=0, keepdims=True))
        mn = jnp.minimum(mn, jnp.min(jnp.where(adm, sc, -SCORE_NEG), axis=0, keepdims=True))
        return mx, mn

    hi0, lo0 = lax.fori_loop(
        0, n_vis, score_tile,
        (jnp.full((1, tq), SCORE_NEG, jnp.float32), jnp.full((1, tq), -SCORE_NEG, jnp.float32)))

    def unsettled(state):
        it, _, _, active = state
        return jnp.logical_and(it < SELECT_ITERS, jnp.max(active) > 0.0)

    def bisect(state):
        it, lo, hi, active = state
        mid = 0.5 * (lo + hi)

        def count_tile(c, cnt):
            ge = jnp.where(sc_ref[c] >= mid, 1.0, 0.0)
            return cnt + jnp.sum(ge.reshape(tk // COUNT_ROWS, COUNT_ROWS, tq), axis=0)

        cnt = lax.fori_loop(0, n_vis, count_tile, jnp.zeros((COUNT_ROWS, tq), jnp.float32))
        cnt = jnp.sum(cnt, axis=0, keepdims=True)
        enough = cnt >= float(top_k)
        active = jnp.where(cnt == float(top_k), 0.0, active)
        return it + 1, jnp.where(enough, mid, lo), jnp.where(enough, hi, mid), active

    active0 = jnp.where(vis_end > top_k, 1.0, 0.0)
    _, thr, _, _ = lax.while_loop(unsettled, bisect, (jnp.int32(0), lo0, hi0, active0))

    def write_tile(c, carry):
        sel = jnp.where(sc_ref[c] >= thr, 0.0, MASK_NEG)
        o_ref[0, c] = sel.T.astype(o_ref.dtype)
        return carry

    lax.fori_loop(0, n_vis, write_tile, 0)

    def fill_tile(c, carry):
        o_ref[0, c] = jnp.full((tq, tk), MASK_NEG, o_ref.dtype)
        return carry

    lax.fori_loop(n_vis, n_kt, fill_tile, 0)


def _indexer_mask(q_idx, k_idx, w_t, batch, seq, top_k, tq=256):
    tq, tk = min(tq, seq), min(KV_TILE, seq)
    n_qt, n_kt = seq // tq, seq // tk
    return pl.pallas_call(
        functools.partial(_indexer_kernel, tq=tq, tk=tk, n_kt=n_kt, top_k=top_k),
        grid=(batch, n_qt),
        in_specs=[
            pl.BlockSpec((N_IDX_HEADS, tq, LANES), lambda b, i: (0, b * n_qt + i, 0)),
            pl.BlockSpec((seq, LANES), lambda b, i: (b, 0)),
            pl.BlockSpec((N_IDX_HEADS, tq), lambda b, i: (0, b * n_qt + i)),
        ],
        out_specs=pl.BlockSpec((1, n_kt, tq, tk), lambda b, i: (b, 0, i, 0)),
        out_shape=jax.ShapeDtypeStruct((batch, n_kt, seq, tk), jnp.bfloat16),
        scratch_shapes=[pltpu.VMEM((n_kt, tk, tq), jnp.float32)],
        compiler_params=_cparams(("parallel", "parallel")),
        name="indexer_mask",
    )(q_idx, k_idx, w_t)


def _softmax_update(s, row_shift, v, m_ref, l_ref, acc_ref, e):
    rows, tk = s.shape
    chunks = [s[:, c * LANES:(c + 1) * LANES] for c in range(tk // LANES)]
    mx = functools.reduce(jnp.maximum, chunks)
    mx = jnp.broadcast_to(jnp.max(mx, axis=-1, keepdims=True), (rows, LANES))
    m_prev = m_ref[e]
    m_new = jnp.maximum(m_prev, mx + row_shift)
    alpha = jnp.exp2(m_prev - m_new)
    r = m_new - row_shift
    p = jnp.concatenate([jnp.exp2(c - r) for c in chunks], axis=1).astype(v.dtype)
    v_ones = jnp.concatenate([v, jnp.ones((tk, LANES), v.dtype)], axis=1)
    pv = jnp.dot(p, v_ones, preferred_element_type=jnp.float32)
    acc_ref[e] = alpha * acc_ref[e] + pv[:, :HEAD_DIM]
    l_ref[e] = alpha * l_ref[e] + pv[:, HEAD_DIM:]
    m_ref[e] = m_new


def _softmax_init(m_ref, l_ref, acc_ref):
    m_ref[...] = jnp.full(m_ref.shape, MASK_NEG, jnp.float32)
    l_ref[...] = jnp.zeros(l_ref.shape, jnp.float32)
    acc_ref[...] = jnp.zeros(acc_ref.shape, jnp.float32)


def _dsa_attn_kernel(q_ref, k_ref, v_ref, b_ref, o_ref, m_ref, l_ref, acc_ref, *, tq, tk):
    qi = pl.program_id(1)
    n_need = ((qi + 1) * tq + tk - 1) // tk
    rows = GROUP_A * tq
    _softmax_init(m_ref, l_ref, acc_ref)

    def kv_step(j, carry):
        off = pl.multiple_of(j * tk, tk)
        bias = b_ref[0, j].astype(jnp.float32)[None]
        for g in range(N_KV_A):
            cols = slice(g * HEAD_DIM, (g + 1) * HEAD_DIM)
            q = q_ref[g * GROUP_A:(g + 1) * GROUP_A].reshape(rows, HEAD_DIM)
            s = _nt_dot(q, k_ref[pl.ds(off, tk), cols])
            s = (s.reshape(GROUP_A, tq, tk) + bias).reshape(rows, tk)
            _softmax_update(s, 0.0, v_ref[pl.ds(off, tk), cols], m_ref, l_ref, acc_ref, g)
        return carry

    lax.fori_loop(0, n_need, kv_step, 0)

    for g in range(N_KV_A):
        out = acc_ref[g] / l_ref[g]
        for h in range(GROUP_A):
            c0 = (g * GROUP_A + h) * HEAD_DIM
            o_ref[:, c0:c0 + HEAD_DIM] = out[h * tq:(h + 1) * tq].astype(o_ref.dtype)


def _dsa_attention(q_a, k_a, proj, v_col, bias, batch, seq, tq=512):
    tk = bias.shape[-1]
    tq = min(tq, seq)
    n_qt, n_kt = seq // tq, seq // tk
    rows = GROUP_A * tq
    return pl.pallas_call(
        functools.partial(_dsa_attn_kernel, tq=tq, tk=tk),
        grid=(batch, n_qt),
        in_specs=[
            pl.BlockSpec((N_HEADS_A, tq, HEAD_DIM), lambda b, i: (0, b * n_qt + i, 0)),
            pl.BlockSpec((seq, KV_A_W), lambda b, i: (b, 0)),
            pl.BlockSpec((seq, KV_A_W), lambda b, i: (b, v_col // KV_A_W)),
            pl.BlockSpec((1, n_kt, tq, tk), lambda b, i: (b, 0, i, 0)),
        ],
        out_specs=pl.BlockSpec((tq, W_A), lambda b, i: (b * n_qt + i, 0)),
        out_shape=jax.ShapeDtypeStruct((batch * seq, W_A), jnp.bfloat16),
        scratch_shapes=[
            pltpu.VMEM((N_KV_A, rows, LANES), jnp.float32),
            pltpu.VMEM((N_KV_A, rows, LANES), jnp.float32),
            pltpu.VMEM((N_KV_A, rows, HEAD_DIM), jnp.float32),
        ],
        compiler_params=_cparams(("parallel", "parallel")),
        name="dsa_attention",
    )(q_a, k_a, proj, bias)


FOX_HEADS_PER_STEP = 4


def _fox_kernel(q_ref, k_ref, v_ref, gq_ref, gk_ref, fq_ref, fk_ref, o_ref, kn_ref, m_ref, l_ref, acc_ref, *, tile):
    hp, qi = pl.program_id(1), pl.program_id(2)
    n_t = kn_ref.shape[0] // tile

    @pl.when(qi == 0)
    def _():
        def norm_keys(j, carry):
            off = pl.multiple_of(j * tile, tile)
            for e in range(FOX_HEADS_PER_STEP):
                cols = slice(e * HEAD_DIM, (e + 1) * HEAD_DIM)
                k = k_ref[pl.ds(off, tile), cols].astype(jnp.float32)
                kn_ref[pl.ds(off, tile), cols] = _head_rms(k, gk_ref[...]).astype(kn_ref.dtype)
            return carry

        lax.fori_loop(0, n_t, norm_keys, 0)

    lane = lax.broadcasted_iota(jnp.int32, (tile, LANES), 1)
    heads = [hp * FOX_HEADS_PER_STEP + e for e in range(FOX_HEADS_PER_STEP)]
    fq = [jnp.broadcast_to(jnp.sum(jnp.where(lane == MISC_F_LANE + h, fq_ref[...], 0.0), axis=1, keepdims=True),
                           (tile, LANES)) for h in heads]
    qn = [_head_rms(q_ref[:, e * HEAD_DIM:(e + 1) * HEAD_DIM].astype(jnp.float32), gq_ref[...]).astype(kn_ref.dtype)
          for e in range(FOX_HEADS_PER_STEP)]
    _softmax_init(m_ref, l_ref, acc_ref)

    def kv_step(j, diagonal):
        off = pl.multiple_of(j * tile, tile)
        for e, h in enumerate(heads):
            cols = slice(e * HEAD_DIM, (e + 1) * HEAD_DIM)
            s = _nt_dot(qn[e], kn_ref[pl.ds(off, tile), cols]) - fk_ref[j, pl.ds(h, 1), :]
            if diagonal:
                t_pos = lax.broadcasted_iota(jnp.int32, (tile, tile), 0)
                s_pos = lax.broadcasted_iota(jnp.int32, (tile, tile), 1)
                s = jnp.where(s_pos <= t_pos, s, MASK_NEG)
            _softmax_update(s, fq[e], v_ref[pl.ds(off, tile), cols], m_ref, l_ref, acc_ref, e)

    def full_step(j, carry):
        kv_step(j, False)
        return carry

    lax.fori_loop(0, qi, full_step, 0)
    kv_step(qi, True)

    for e in range(FOX_HEADS_PER_STEP):
        o_ref[:, e * HEAD_DIM:(e + 1) * HEAD_DIM] = (acc_ref[e] / l_ref[e]).astype(o_ref.dtype)


def _fox_attention(proj, q_col, g_q, g_k, wf, f_row, batch, seq):
    tile = min(KV_TILE, seq)
    n_t = seq // tile
    wb = FOX_HEADS_PER_STEP * HEAD_DIM
    q_blk, k_blk, v_blk = q_col // wb, (q_col + W_B) // wb, (q_col + 2 * W_B) // wb
    gain_spec = pl.BlockSpec((1, HEAD_DIM), lambda b, h, i: (0, 0))
    return pl.pallas_call(
        functools.partial(_fox_kernel, tile=tile),
        grid=(batch, N_HEADS_B // FOX_HEADS_PER_STEP, n_t),
        in_specs=[
            pl.BlockSpec((tile, wb), lambda b, h, i: (b * n_t + i, q_blk + h)),
            pl.BlockSpec((seq, wb), lambda b, h, i: (b, k_blk + h)),
            pl.BlockSpec((seq, wb), lambda b, h, i: (b, v_blk + h)),
            gain_spec, gain_spec,
            pl.BlockSpec((tile, LANES), lambda b, h, i: (b * n_t + i, 0)),
            pl.BlockSpec((n_t, N_HEADS_B, tile), lambda b, h, i: (b, 0, 0)),
        ],
        out_specs=pl.BlockSpec((tile, wb), lambda b, h, i: (b * n_t + i, h)),
        out_shape=jax.ShapeDtypeStruct((batch * seq, W_B), jnp.bfloat16),
        scratch_shapes=[
            pltpu.VMEM((seq, wb), jnp.bfloat16),
            pltpu.VMEM((FOX_HEADS_PER_STEP, tile, LANES), jnp.float32),
            pltpu.VMEM((FOX_HEADS_PER_STEP, tile, LANES), jnp.float32),
            pltpu.VMEM((FOX_HEADS_PER_STEP, tile, HEAD_DIM), jnp.float32),
        ],
        compiler_params=_cparams(("parallel", "parallel", "arbitrary")),
        name="fox_attention",
    )(proj, proj, proj, g_q.reshape(1, HEAD_DIM), g_k.reshape(1, HEAD_DIM), wf, f_row)


def _layer(x, p, pos_col, batch, seq, g_attn, w_in, g_cq, w_uq, w_uq_idx, g_kidx, b_kidx, g_q_a, g_k_a, b_forget,
           g_q_b, g_k_b, w_up_a, w_up_b, w_o, g_ffn, w_ffn_gate, w_ffn_up, w_ffn_down, g_ple, w_ple, w_ple_gate):
    d = x.shape[1]
    bf = jnp.bfloat16
    top_k = min(INDEX_TOPK, seq // 4)

    o_kidx = Q_RANK + 2 * KV_A_W
    o_qb = o_kidx + IDX_DIM + N_IDX_HEADS
    o_fb = o_qb + 3 * W_B
    o_ga = o_fb + N_HEADS_B
    w_in_t = jnp.swapaxes(w_in, 0, 1)
    w_main = _stage_regroup(w_in_t, [(0, o_kidx), (o_qb, o_fb - o_qb), (o_ga, 2 * d)])
    w_misc = _stage_misc(w_in_t, o_kidx, o_qb - o_kidx, o_fb, o_ga - o_fb)
    col_ka, col_va = Q_RANK, Q_RANK + KV_A_W
    col_qb = Q_RANK + 2 * KV_A_W
    col_vb = col_qb + 2 * W_B
    col_ga = col_vb + W_B
    col_gb = col_ga + d

    tabs = _rope_tables(pos_col)
    cf, sf = tabs[0], tabs[1]

    h = _rmsnorm(x, g_attn)
    proj = _matmul(h, w_main, bf, name="proj_main")
    misc = _matmul(h, w_misc, jnp.float32, tn=LANES, name="proj_misc")

    q_a, q_idx = _q_proj(proj, g_cq, w_uq.astype(bf), w_uq_idx.astype(bf), g_q_a, tabs)
    k_a = _ka_norm_rope(proj, col_ka, g_k_a, cf, sf)
    k_idx, wf, w_t, f_row = _misc_post(misc, g_kidx, b_kidx, b_forget, tabs, seq)
    bias = _indexer_mask(q_idx, k_idx, w_t, batch, seq, top_k)
    o_a = _dsa_attention(q_a, k_a, proj, col_va, bias, batch, seq)

    o_b = _fox_attention(proj, col_qb, g_q_b * Q_SCALE, g_k_b, wf, f_row, batch, seq)

    merged = _merge(o_a, o_b, w_up_a.astype(bf), w_up_b.astype(bf), proj, col_ga, col_gb)
    x = _matmul_residual(merged, w_o.astype(bf), x, name="out_proj")

    h = _rmsnorm(x, g_ffn)
    u = _swiglu(h, w_ffn_gate.astype(bf), w_ffn_up.astype(bf), tn=FFN_TILE)
    x = _matmul_residual(u, w_ffn_down.astype(bf), x, tm=512, name="ffn_down")

    h = _rmsnorm(x, g_ple)
    x = _ple(h, w_ple_gate.astype(bf), p.astype(bf), w_ple.astype(bf), x)
    return x


def kernel(x, p, positions, g_attn, w_in, g_cq, w_uq, w_uq_idx, g_kidx, b_kidx, g_q_a, g_k_a, b_forget, g_q_b, g_k_b,
           w_up_a, w_up_b, w_o, g_ffn, w_ffn_gate, w_ffn_up, w_ffn_down, g_ple, w_ple, w_ple_gate):
    batch, seq, d = x.shape
    depth = w_in.shape[0]
    xf = x.reshape(batch * seq, d)
    pos_col = positions.reshape(batch * seq, 1)
    for i in range(depth):
        xf = _layer(xf, p[i].reshape(batch * seq, -1), pos_col, batch, seq, g_attn[i], w_in[i], g_cq[i], w_uq[i],
                    w_uq_idx[i], g_kidx[i], b_kidx[i], g_q_a[i], g_k_a[i], b_forget[i], g_q_b[i], g_k_b[i],
                    w_up_a[i], w_up_b[i], w_o[i], g_ffn[i], w_ffn_gate[i], w_ffn_up[i], w_ffn_down[i], g_ple[i],
                    w_ple[i], w_ple_gate[i])
    return xf.reshape(batch, seq, d)
```

```python
import functools

import numpy as np
import jax
import jax.numpy as jnp
from jax import lax
from jax.experimental import pallas as pl
from jax.experimental.pallas import tpu as pltpu

CHUNK = 64
CHUNK_SHIFT = CHUNK.bit_length() - 1
HEAD_DIM = 128
ROPE_THETA = 10000.0
EPS = 1e-6
N_HEADS_A = 16
N_KV_A = 2
Q_RANK = 1024
N_IDX_HEADS = 32
IDX_DIM = 64
IDX_ROPE_DIM = 32
INDEX_TOPK = 256
N_HEADS_B = 16
PLE_DIM = 256

LANES = 128
KV_A_W = N_KV_A * HEAD_DIM
W_A = N_HEADS_A * HEAD_DIM
W_B = N_HEADS_B * HEAD_DIM
GROUP_A = N_HEADS_A // N_KV_A

MISC_W_LANE = IDX_DIM
MISC_F_LANE = IDX_DIM + N_IDX_HEADS

MASK_NEG = -1e30
SCORE_NEG = -3e38
SELECT_ITERS = 32
LOG2E = 1.4426950408889634
Q_SCALE = HEAD_DIM ** -0.5 * LOG2E
KV_TILE = 512
FFN_TILE = 256
VMEM_LIMIT = 56 * 1024 * 1024


def _cparams(sem):
    return pltpu.CompilerParams(dimension_semantics=sem, vmem_limit_bytes=VMEM_LIMIT)


def _sigmoid(x):
    return 1.0 / (1.0 + jnp.exp(-x))


def _nt_dot(a, b):
    return lax.dot_general(a, b, (((1,), (1,)), ((), ())), preferred_element_type=jnp.float32)


def _wdot(a, w_ref):
    return jnp.dot(a, w_ref[...].astype(a.dtype), preferred_element_type=jnp.float32)


def _stage_regroup_kernel(w_ref, o_ref):
    o_ref[...] = w_ref[...].T.astype(o_ref.dtype)


def _stage_regroup(w_t, segments, tn=512, tk=1024):
    n_src, k = w_t.shape
    lags, dst = [], 0
    for src, width in segments:
        assert width % tn == 0 and src % 16 == 0 and src + width <= n_src
        lags.append((dst // tn, src - dst))
        dst += width
    tk = min(tk, k)

    def src_row(j):
        lag = lags[0][1]
        for first_tile, seg_lag in lags[1:]:
            lag = jnp.where(j >= first_tile, seg_lag, lag)
        return pl.multiple_of(j * tn + lag, 16)

    return pl.pallas_call(
        _stage_regroup_kernel,
        grid=(dst // tn, k // tk),
        in_specs=[pl.BlockSpec((pl.Element(tn), pl.Element(tk)),
                               lambda j, kk: (src_row(j), pl.multiple_of(kk * tk, tk)))],
        out_specs=pl.BlockSpec((tk, tn), lambda j, kk: (kk, j)),
        out_shape=jax.ShapeDtypeStruct((k, dst), jnp.bfloat16),
        compiler_params=_cparams(("parallel", "parallel")),
        name="stage_regroup",
    )(w_t)


def _stage_misc_kernel(a_ref, b_ref, o_ref):
    a, b = a_ref[...], b_ref[...]
    pad = jnp.zeros((LANES - a.shape[0] - b.shape[0], a.shape[1]), a.dtype)
    o_ref[...] = jnp.concatenate([a, b, pad], axis=0).T.astype(o_ref.dtype)


def _stage_misc(w_t, start_a, n_a, start_b, n_b, tk=1024):
    k = w_t.shape[1]
    tk = min(tk, k)
    col = lambda kk: pl.multiple_of(kk * tk, tk)
    return pl.pallas_call(
        _stage_misc_kernel,
        grid=(k // tk,),
        in_specs=[
            pl.BlockSpec((pl.Element(n_a), pl.Element(tk)), lambda kk: (start_a, col(kk))),
            pl.BlockSpec((pl.Element(n_b), pl.Element(tk)), lambda kk: (start_b, col(kk))),
        ],
        out_specs=pl.BlockSpec((tk, LANES), lambda kk: (kk, 0)),
        out_shape=jax.ShapeDtypeStruct((k, LANES), jnp.bfloat16),
        compiler_params=_cparams(("parallel",)),
        name="stage_misc",
    )(w_t, w_t)


def _rmsnorm_kernel(x_ref, g_ref, o_ref):
    x = x_ref[...]
    ms = jnp.mean(x * x, axis=-1, keepdims=True)
    o_ref[...] = (x * lax.rsqrt(ms + EPS) * g_ref[...]).astype(o_ref.dtype)


def _rmsnorm(x, g, tm=256):
    t, d = x.shape
    tm = min(tm, t)
    return pl.pallas_call(
        _rmsnorm_kernel,
        grid=(t // tm,),
        in_specs=[pl.BlockSpec((tm, d), lambda i: (i, 0)), pl.BlockSpec((1, d), lambda i: (0, 0))],
        out_specs=pl.BlockSpec((tm, d), lambda i: (i, 0)),
        out_shape=jax.ShapeDtypeStruct((t, d), jnp.bfloat16),
        compiler_params=_cparams(("parallel",)),
        name="rmsnorm",
    )(x, g.reshape(1, d))


def _mm_kernel(a_ref, w_ref, o_ref):
    o_ref[...] = jnp.dot(a_ref[...], w_ref[...], preferred_element_type=jnp.float32).astype(o_ref.dtype)


def _matmul(a, w, out_dtype, tm=1024, tn=512, name="matmul"):
    t, k = a.shape
    n = w.shape[1]
    tm, tn = min(tm, t), min(tn, n)
    return pl.pallas_call(
        _mm_kernel,
        grid=(t // tm, n // tn),
        in_specs=[pl.BlockSpec((tm, k), lambda i, j: (i, 0)), pl.BlockSpec((k, tn), lambda i, j: (0, j))],
        out_specs=pl.BlockSpec((tm, tn), lambda i, j: (i, j)),
        out_shape=jax.ShapeDtypeStruct((t, n), out_dtype),
        compiler_params=_cparams(("parallel", "parallel")),
        name=name,
    )(a, w)


def _mm_res_kernel(a_ref, w_ref, r_ref, o_ref):
    o_ref[...] = r_ref[...] + _wdot(a_ref[...], w_ref)


def _matmul_residual(a, w, r, tm=1024, tn=512, name="matmul_residual"):
    t, k = a.shape
    n = w.shape[1]
    tm, tn = min(tm, t), min(tn, n)
    return pl.pallas_call(
        _mm_res_kernel,
        grid=(t // tm, n // tn),
        in_specs=[
            pl.BlockSpec((tm, k), lambda i, j: (i, 0)),
            pl.BlockSpec((k, tn), lambda i, j: (0, j)),
            pl.BlockSpec((tm, tn), lambda i, j: (i, j)),
        ],
        out_specs=pl.BlockSpec((tm, tn), lambda i, j: (i, j)),
        out_shape=jax.ShapeDtypeStruct((t, n), jnp.float32),
        compiler_params=_cparams(("parallel", "parallel")),
        name=name,
    )(a, w, r)


def _swiglu_kernel(a_ref, wg_ref, wu_ref, o_ref):
    a = a_ref[...]
    g = _wdot(a, wg_ref)
    u = _wdot(a, wu_ref)
    o_ref[...] = (g * _sigmoid(g) * u).astype(o_ref.dtype)


def _swiglu(a, wg, wu, tm=1024, tn=512):
    t, k = a.shape
    n = wg.shape[1]
    tm, tn = min(tm, t), min(tn, n)
    return pl.pallas_call(
        _swiglu_kernel,
        grid=(t // tm, n // tn),
        in_specs=[
            pl.BlockSpec((tm, k), lambda i, j: (i, 0)),
            pl.BlockSpec((k, tn), lambda i, j: (0, j)),
            pl.BlockSpec((k, tn), lambda i, j: (0, j)),
        ],
        out_specs=pl.BlockSpec((tm, tn), lambda i, j: (i, j)),
        out_shape=jax.ShapeDtypeStruct((t, n), jnp.bfloat16),
        compiler_params=_cparams(("parallel", "parallel")),
        name="swiglu",
    )(a, wg, wu)


def _merge_kernel(oa_ref, ob_ref, wa_ref, wb_ref, ga_ref, gb_ref, o_ref):
    a = _wdot(oa_ref[...], wa_ref)
    b = _wdot(ob_ref[...], wb_ref)
    ga = _sigmoid(ga_ref[...].astype(jnp.float32))
    gb = _sigmoid(gb_ref[...].astype(jnp.float32))
    o_ref[...] = (ga * a + gb * b).astype(o_ref.dtype)


def _merge(o_a, o_b, w_up_a, w_up_b, proj, ga_col, gb_col, tm=1024, tn=512):
    t, ka = o_a.shape
    kb = o_b.shape[1]
    n = w_up_a.shape[1]
    tm, tn = min(tm, t), min(tn, n)
    ga_blk, gb_blk = ga_col // tn, gb_col // tn
    return pl.pallas_call(
        _merge_kernel,
        grid=(t // tm, n // tn),
        in_specs=[
            pl.BlockSpec((tm, ka), lambda i, j: (i, 0)),
            pl.BlockSpec((tm, kb), lambda i, j: (i, 0)),
            pl.BlockSpec((ka, tn), lambda i, j: (0, j)),
            pl.BlockSpec((kb, tn), lambda i, j: (0, j)),
            pl.BlockSpec((tm, tn), lambda i, j: (i, ga_blk + j)),
            pl.BlockSpec((tm, tn), lambda i, j: (i, gb_blk + j)),
        ],
        out_specs=pl.BlockSpec((tm, tn), lambda i, j: (i, j)),
        out_shape=jax.ShapeDtypeStruct((t, n), jnp.bfloat16),
        compiler_params=_cparams(("parallel", "parallel")),
        name="merge",
    )(o_a, o_b, w_up_a, w_up_b, proj, proj)


def _ple_kernel(h_ref, wg_ref, p_ref, wp_ref, r_ref, o_ref):
    g = _wdot(h_ref[...], wg_ref)
    e = _wdot(p_ref[...], wp_ref)
    o_ref[...] = r_ref[...] + _sigmoid(g) * e


def _ple(h, w_gate, p, w_ple, r, tm=1024, tn=512):
    t, k = h.shape
    kp = p.shape[1]
    n = w_gate.shape[1]
    tm, tn = min(tm, t), min(tn, n)
    return pl.pallas_call(
        _ple_kernel,
        grid=(t // tm, n // tn),
        in_specs=[
            pl.BlockSpec((tm, k), lambda i, j: (i, 0)),
            pl.BlockSpec((k, tn), lambda i, j: (0, j)),
            pl.BlockSpec((tm, kp), lambda i, j: (i, 0)),
            pl.BlockSpec((kp, tn), lambda i, j: (0, j)),
            pl.BlockSpec((tm, tn), lambda i, j: (i, j)),
        ],
        out_specs=pl.BlockSpec((tm, tn), lambda i, j: (i, j)),
        out_shape=jax.ShapeDtypeStruct((t, n), jnp.float32),
        compiler_params=_cparams(("parallel", "parallel")),
        name="ple",
    )(h, w_gate, p, w_ple, r)


def _rope_consts():
    half = HEAD_DIM // 2
    inv_full = jnp.power(ROPE_THETA, -jnp.arange(half, dtype=jnp.float32) * (2.0 / HEAD_DIM))
    half_i = IDX_ROPE_DIM // 2
    inv_idx = jnp.power(ROPE_THETA, -jnp.arange(half_i, dtype=jnp.float32) * (2.0 / IDX_ROPE_DIM))
    zeros = jnp.zeros((IDX_DIM - IDX_ROPE_DIM,), jnp.float32)
    sign = np.concatenate([-np.ones(half, np.float32), np.ones(half, np.float32)])
    mask_a = np.zeros(IDX_DIM, np.float32)
    mask_a[:half_i] = -1.0
    mask_b = np.zeros(IDX_DIM, np.float32)
    mask_b[half_i:IDX_ROPE_DIM] = 1.0
    reps = LANES // IDX_DIM
    rows = [
        jnp.concatenate([inv_full, inv_full]),
        jnp.asarray(sign),
        jnp.tile(jnp.concatenate([inv_idx, inv_idx, zeros]), reps),
        jnp.asarray(np.tile(mask_a, reps)),
        jnp.asarray(np.tile(mask_b, reps)),
    ]
    rows += [jnp.zeros((LANES,), jnp.float32)] * 3
    return jnp.stack(rows)


def _rope_tables_kernel(pos_ref, c_ref, cf_ref, sf_ref, ci_ref, sa_ref, sb_ref):
    pos = pos_ref[...].astype(jnp.float32)
    ang = pos * c_ref[0:1, :]
    cf_ref[...] = jnp.cos(ang)
    sf_ref[...] = jnp.sin(ang) * c_ref[1:2, :]
    ang_i = pos * c_ref[2:3, :]
    ci_ref[...] = jnp.cos(ang_i)
    s_i = jnp.sin(ang_i)
    sa_ref[...] = s_i * c_ref[3:4, :]
    sb_ref[...] = s_i * c_ref[4:5, :]


def _rope_tables(pos_col, tm=512):
    t = pos_col.shape[0]
    tm = min(tm, t)
    tab = jax.ShapeDtypeStruct((t, LANES), jnp.float32)
    spec = pl.BlockSpec((tm, LANES), lambda i: (i, 0))
    return pl.pallas_call(
        _rope_tables_kernel,
        grid=(t // tm,),
        in_specs=[pl.BlockSpec((tm, 1), lambda i: (i, 0)), pl.BlockSpec((8, LANES), lambda i: (0, 0))],
        out_specs=[spec] * 5,
        out_shape=[tab] * 5,
        compiler_params=_cparams(("parallel",)),
        name="rope_tables",
    )(pos_col, _rope_consts())


def _rope_full(x, cf, sf):
    return x * cf + pltpu.roll(x, HEAD_DIM // 2, axis=1) * sf


def _rope_idx(x, ci, sa, sb):
    half = IDX_ROPE_DIM // 2
    return x * ci + pltpu.roll(x, LANES - half, axis=1) * sa + pltpu.roll(x, half, axis=1) * sb


def _head_rms(x, g):
    ms = jnp.mean(x * x, axis=-1, keepdims=True)
    return x * lax.rsqrt(ms + EPS) * g


def _q_proj_kernel(cq_ref, gcq_ref, wa_ref, wi_ref, gqa_ref, cf_ref, sf_ref, ci_ref, sa_ref, sb_ref, qa_ref, qi_ref):
    cq = cq_ref[...].astype(jnp.float32)
    ms = jnp.mean(cq * cq, axis=-1, keepdims=True)
    cq = (cq * lax.rsqrt(ms + EPS) * gcq_ref[...]).astype(jnp.bfloat16)
    cf, sf = cf_ref[...], sf_ref[...]
    gqa = gqa_ref[...]
    for j in range(N_HEADS_A // 2):
        acc = jnp.dot(cq, wa_ref[:, j * 2 * HEAD_DIM:(j + 1) * 2 * HEAD_DIM], preferred_element_type=jnp.float32)
        for e in range(2):
            hq = _head_rms(acc[:, e * HEAD_DIM:(e + 1) * HEAD_DIM], gqa)
            qa_ref[2 * j + e] = (_rope_full(hq, cf, sf) * Q_SCALE).astype(qa_ref.dtype)
    ci, sa, sb = ci_ref[...], sa_ref[...], sb_ref[...]
    low_half = lax.broadcasted_iota(jnp.int32, (cq.shape[0], LANES), 1) < IDX_DIM
    per_dot = 2 * LANES // IDX_DIM
    for j in range(N_IDX_HEADS // per_dot):
        acc = jnp.dot(cq, wi_ref[:, j * 2 * LANES:(j + 1) * 2 * LANES], preferred_element_type=jnp.float32)
        for c in range(2):
            pair = _rope_idx(acc[:, c * LANES:(c + 1) * LANES], ci, sa, sb)
            for e, head in enumerate((pair, pltpu.roll(pair, IDX_DIM, axis=1))):
                qi_ref[per_dot * j + 2 * c + e] = jnp.where(low_half, head, 0.0).astype(qi_ref.dtype)


def _q_proj(proj, g_cq, w_uq, w_uq_idx, g_q_a, tabs, tm=512):
    t = proj.shape[0]
    tm = min(tm, t)
    cf, sf, ci, sa, sb = tabs
    tab_spec = pl.BlockSpec((tm, LANES), lambda i: (i, 0))
    return pl.pallas_call(
        _q_proj_kernel,
        grid=(t // tm,),
        in_specs=[
            pl.BlockSpec((tm, Q_RANK), lambda i: (i, 0)),
            pl.BlockSpec((1, Q_RANK), lambda i: (0, 0)),
            pl.BlockSpec((Q_RANK, W_A), lambda i: (0, 0)),
            pl.BlockSpec((Q_RANK, N_IDX_HEADS * IDX_DIM), lambda i: (0, 0)),
            pl.BlockSpec((1, HEAD_DIM), lambda i: (0, 0)),
            tab_spec, tab_spec, tab_spec, tab_spec, tab_spec,
        ],
        out_specs=[
            pl.BlockSpec((N_HEADS_A, tm, HEAD_DIM), lambda i: (0, i, 0)),
            pl.BlockSpec((N_IDX_HEADS, tm, LANES), lambda i: (0, i, 0)),
        ],
        out_shape=[
            jax.ShapeDtypeStruct((N_HEADS_A, t, HEAD_DIM), jnp.bfloat16),
            jax.ShapeDtypeStruct((N_IDX_HEADS, t, LANES), jnp.bfloat16),
        ],
        compiler_params=_cparams(("parallel",)),
        name="q_proj",
    )(proj, g_cq.reshape(1, Q_RANK), w_uq, w_uq_idx, g_q_a.reshape(1, HEAD_DIM), cf, sf, ci, sa, sb)


def _ka_kernel(x_ref, g_ref, cf_ref, sf_ref, o_ref):
    x = x_ref[...].astype(jnp.float32)
    g, cf, sf = g_ref[...], cf_ref[...], sf_ref[...]
    for h in range(N_KV_A):
        hk = _head_rms(x[:, h * HEAD_DIM:(h + 1) * HEAD_DIM], g)
        o_ref[:, h * HEAD_DIM:(h + 1) * HEAD_DIM] = _rope_full(hk, cf, sf).astype(o_ref.dtype)


def _ka_norm_rope(proj, col, g_k_a, cf, sf, tm=512):
    t = proj.shape[0]
    tm = min(tm, t)
    tab_spec = pl.BlockSpec((tm, LANES), lambda i: (i, 0))
    return pl.pallas_call(
        _ka_kernel,
        grid=(t // tm,),
        in_specs=[
            pl.BlockSpec((tm, KV_A_W), lambda i: (i, col // KV_A_W)),
            pl.BlockSpec((1, HEAD_DIM), lambda i: (0, 0)),
            tab_spec, tab_spec,
        ],
        out_specs=pl.BlockSpec((tm, KV_A_W), lambda i: (i, 0)),
        out_shape=jax.ShapeDtypeStruct((t, KV_A_W), jnp.bfloat16),
        compiler_params=_cparams(("parallel",)),
        name="ka_norm_rope",
    )(proj, g_k_a.reshape(1, HEAD_DIM), cf, sf)


def _split3_bf16(x):
    hi = x.astype(jnp.bfloat16)
    r1 = x - hi.astype(jnp.float32)
    mid = r1.astype(jnp.bfloat16)
    lo = (r1 - mid.astype(jnp.float32)).astype(jnp.bfloat16)
    return hi, mid, lo


def _misc_kernel(m_ref, c_ref, ci_ref, sa_ref, sb_ref, tri_ref, kidx_ref, wf_ref, wt_ref, frow_ref, carry_ref, *,
                 tiles_per_seq):
    i = pl.program_id(0)

    @pl.when(i % tiles_per_seq == 0)
    def _():
        carry_ref[...] = jnp.zeros_like(carry_ref)

    x = m_ref[...]
    lane = lax.broadcasted_iota(jnp.int32, x.shape, 1)
    is_k = lane < IDX_DIM
    xk = jnp.where(is_k, x, 0.0)
    mu = jnp.sum(xk, axis=-1, keepdims=True) * (1.0 / IDX_DIM)
    dk = jnp.where(is_k, x - mu, 0.0)
    var = jnp.sum(dk * dk, axis=-1, keepdims=True) * (1.0 / IDX_DIM)
    y = dk * lax.rsqrt(var + EPS) * c_ref[0:1, :] + c_ref[1:2, :]
    y = _rope_idx(y, ci_ref[...], sa_ref[...], sb_ref[...])
    kidx_ref[...] = jnp.where(is_k, y, 0.0).astype(kidx_ref.dtype)
    f = x + c_ref[2:3, :]
    log_f = jnp.minimum(f, 0.0) - jnp.log1p(jnp.exp(-jnp.abs(f)))
    hi, mid, lo = _split3_bf16(log_f)
    tri = tri_ref[...]
    csum = (jnp.dot(tri, hi, preferred_element_type=jnp.float32)
            + jnp.dot(tri, mid, preferred_element_type=jnp.float32)
            + jnp.dot(tri, lo, preferred_element_type=jnp.float32))
    csum = csum + carry_ref[0:1, :]
    carry_ref[...] = jnp.broadcast_to(csum[-1:, :], carry_ref.shape)
    is_w = (lane >= MISC_W_LANE) & (lane < MISC_F_LANE)
    wf = jnp.where(is_w, x * c_ref[3:4, :], csum * LOG2E)
    wf_ref[...] = wf
    wf_t = wf.T
    wt_ref[...] = wf_t[MISC_W_LANE:MISC_F_LANE, :]
    frow_ref[0] = wf_t[MISC_F_LANE:MISC_F_LANE + N_HEADS_B, :]


def _misc_post(misc, g_kidx, b_kidx, b_forget, tabs, seq):
    t = misc.shape[0]
    tm = min(KV_TILE, seq)
    idx_w_scale = (N_IDX_HEADS ** -0.5) * (IDX_DIM ** -0.5)
    pad = lambda v, off: jnp.zeros((LANES,), jnp.float32).at[off:off + v.shape[0]].set(v)
    consts = jnp.stack([
        pad(g_kidx, 0), pad(b_kidx, 0), pad(b_forget, MISC_F_LANE),
        pad(jnp.full((N_IDX_HEADS,), idx_w_scale, jnp.float32), MISC_W_LANE),
    ] + [jnp.zeros((LANES,), jnp.float32)] * 4)
    tri = jnp.asarray(np.tril(np.ones((tm, tm), np.float32)), jnp.bfloat16)
    _, _, ci, sa, sb = tabs
    tab_spec = pl.BlockSpec((tm, LANES), lambda i: (i, 0))
    return pl.pallas_call(
        functools.partial(_misc_kernel, tiles_per_seq=seq // tm),
        grid=(t // tm,),
        in_specs=[
            tab_spec,
            pl.BlockSpec((8, LANES), lambda i: (0, 0)),
            tab_spec, tab_spec, tab_spec,
            pl.BlockSpec((tm, tm), lambda i: (0, 0)),
        ],
        out_specs=[
            tab_spec, tab_spec,
            pl.BlockSpec((N_IDX_HEADS, tm), lambda i: (0, i)),
            pl.BlockSpec((1, N_HEADS_B, tm), lambda i: (i, 0, 0)),
        ],
        out_shape=[
            jax.ShapeDtypeStruct((t, LANES), jnp.bfloat16),
            jax.ShapeDtypeStruct((t, LANES), jnp.float32),
            jax.ShapeDtypeStruct((N_IDX_HEADS, t), jnp.float32),
            jax.ShapeDtypeStruct((t // tm, N_HEADS_B, tm), jnp.float32),
        ],
        scratch_shapes=[pltpu.VMEM((8, LANES), jnp.float32)],
        compiler_params=_cparams(("arbitrary",)),
        name="misc_post",
    )(misc, consts, ci, sa, sb, tri)


IDX_HEAD_GROUP = 8
IDX_SLAB = 128
COUNT_ROWS = 64


def _indexer_kernel(q_ref, k_ref, w_ref, o_ref, sc_ref, *, tq, tk, n_kt, top_k):
    qi = pl.program_id(1)
    t0 = qi * tq
    n_vis = (t0 + tq + tk - 1) // tk
    t_row = lax.broadcasted_iota(jnp.int32, (1, tq), 1) + t0
    vis_end = (jnp.right_shift(t_row, CHUNK_SHIFT) + 1) * CHUNK

    def score_tile(c, carry):
        mx, mn = carry
        kt = k_ref[pl.ds(pl.multiple_of(c * tk, tk), tk), :]
        for g in range(N_IDX_HEADS // IDX_HEAD_GROUP):
            qg = q_ref[g * IDX_HEAD_GROUP:(g + 1) * IDX_HEAD_GROUP].reshape(IDX_HEAD_GROUP * tq, LANES)
            st = _nt_dot(kt, qg)
            for r in range(tk // IDX_SLAB):
                rows = slice(r * IDX_SLAB, (r + 1) * IDX_SLAB)
                part = jnp.zeros((IDX_SLAB, tq), jnp.float32)
                for j in range(IDX_HEAD_GROUP):
                    h = g * IDX_HEAD_GROUP + j
                    part = part + jnp.maximum(st[rows, j * tq:(j + 1) * tq], 0.0) * w_ref[h:h + 1, :]
                if g == 0:
                    sc_ref[c, rows, :] = part
                else:
                    sc_ref[c, rows, :] += part
        s_col = lax.broadcasted_iota(jnp.int32, (tk, 1), 0) + c * tk
        adm = s_col < vis_end
        sc = sc_ref[c]
        sc_ref[c] = jnp.where(adm, sc, SCORE_NEG)
        mx = jnp.maximum(mx, jnp.max(jnp.where(adm, sc, SCORE_NEG), axis=0, keepdims=True))
        mn = jnp.minimum(mn, jnp.min(jnp.where(adm, sc, -SCORE_NEG), axis=0, keepdims=True))
        return mx, mn

    hi0, lo0 = lax.fori_loop(
        0, n_vis, score_tile,
        (jnp.full((1, tq), SCORE_NEG, jnp.float32), jnp.full((1, tq), -SCORE_NEG, jnp.float32)))

    def unsettled(state):
        it, _, _, active = state
        return jnp.logical_and(it < SELECT_ITERS, jnp.max(active) > 0.0)

    def bisect(state):
        it, lo, hi, active = state
        mid = 0.5 * (lo + hi)

        def count_tile(c, cnt):
            ge = jnp.where(sc_ref[c] >= mid, 1.0, 0.0)
            return cnt + jnp.sum(ge.reshape(tk // COUNT_ROWS, COUNT_ROWS, tq), axis=0)

        cnt = lax.fori_loop(0, n_vis, count_tile, jnp.zeros((COUNT_ROWS, tq), jnp.float32))
        cnt = jnp.sum(cnt, axis=0, keepdims=True)
        enough = cnt >= float(top_k)
        active = jnp.where(cnt == float(top_k), 0.0, active)
        return it + 1, jnp.where(enough, mid, lo), jnp.where(enough, hi, mid), active

    active0 = jnp.where(vis_end > top_k, 1.0, 0.0)
    _, thr, _, _ = lax.while_loop(unsettled, bisect, (jnp.int32(0), lo0, hi0, active0))

    def write_tile(c, carry):
        sel = jnp.where(sc_ref[c] >= thr, 0.0, MASK_NEG)
        o_ref[0, c] = sel.T.astype(o_ref.dtype)
        return carry

    lax.fori_loop(0, n_vis, write_tile, 0)

    def fill_tile(c, carry):
        o_ref[0, c] = jnp.full((tq, tk), MASK_NEG, o_ref.dtype)
        return carry

    lax.fori_loop(n_vis, n_kt, fill_tile, 0)


def _indexer_mask(q_idx, k_idx, w_t, batch, seq, top_k, tq=256):
    tq, tk = min(tq, seq), min(KV_TILE, seq)
    n_qt, n_kt = seq // tq, seq // tk
    return pl.pallas_call(
        functools.partial(_indexer_kernel, tq=tq, tk=tk, n_kt=n_kt, top_k=top_k),
        grid=(batch, n_qt),
        in_specs=[
            pl.BlockSpec((N_IDX_HEADS, tq, LANES), lambda b, i: (0, b * n_qt + i, 0)),
            pl.BlockSpec((seq, LANES), lambda b, i: (b, 0)),
            pl.BlockSpec((N_IDX_HEADS, tq), lambda b, i: (0, b * n_qt + i)),
        ],
        out_specs=pl.BlockSpec((1, n_kt, tq, tk), lambda b, i: (b, 0, i, 0)),
        out_shape=jax.ShapeDtypeStruct((batch, n_kt, seq, tk), jnp.bfloat16),
        scratch_shapes=[pltpu.VMEM((n_kt, tk, tq), jnp.float32)],
        compiler_params=_cparams(("parallel", "parallel")),
        name="indexer_mask",
    )(q_idx, k_idx, w_t)


def _softmax_update(s, row_shift, v, m_ref, l_ref, acc_ref, e):
    rows, tk = s.shape
    chunks = [s[:, c * LANES:(c + 1) * LANES] for c in range(tk // LANES)]
    mx = functools.reduce(jnp.maximum, chunks)
    mx = jnp.broadcast_to(jnp.max(mx, axis=-1, keepdims=True), (rows, LANES))
    m_prev = m_ref[e]
    m_new = jnp.maximum(m_prev, mx + row_shift)
    alpha = jnp.exp2(m_prev - m_new)
    r = m_new - row_shift
    p = jnp.concatenate([jnp.exp2(c - r) for c in chunks], axis=1).astype(v.dtype)
    v_ones = jnp.concatenate([v, jnp.ones((tk, LANES), v.dtype)], axis=1)
    pv = jnp.dot(p, v_ones, preferred_element_type=jnp.float32)
    acc_ref[e] = alpha * acc_ref[e] + pv[:, :HEAD_DIM]
    l_ref[e] = alpha * l_ref[e] + pv[:, HEAD_DIM:]
    m_ref[e] = m_new


def _softmax_init(m_ref, l_ref, acc_ref):
    m_ref[...] = jnp.full(m_ref.shape, MASK_NEG, jnp.float32)
    l_ref[...] = jnp.zeros(l_ref.shape, jnp.float32)
    acc_ref[...] = jnp.zeros(acc_ref.shape, jnp.float32)


def _dsa_attn_kernel(q_ref, k_ref, v_ref, b_ref, o_ref, m_ref, l_ref, acc_ref, *, tq, tk):
    qi = pl.program_id(1)
    n_need = ((qi + 1) * tq + tk - 1) // tk
    rows = GROUP_A * tq
    _softmax_init(m_ref, l_ref, acc_ref)

    def kv_step(j, carry):
        off = pl.multiple_of(j * tk, tk)
        bias = b_ref[0, j].astype(jnp.float32)[None]
        for g in range(N_KV_A):
            cols = slice(g * HEAD_DIM, (g + 1) * HEAD_DIM)
            q = q_ref[g * GROUP_A:(g + 1) * GROUP_A].reshape(rows, HEAD_DIM)
            s = _nt_dot(q, k_ref[pl.ds(off, tk), cols])
            s = (s.reshape(GROUP_A, tq, tk) + bias).reshape(rows, tk)
            _softmax_update(s, 0.0, v_ref[pl.ds(off, tk), cols], m_ref, l_ref, acc_ref, g)
        return carry

    lax.fori_loop(0, n_need, kv_step, 0)

    for g in range(N_KV_A):
        out = acc_ref[g] / l_ref[g]
        for h in range(GROUP_A):
            c0 = (g * GROUP_A + h) * HEAD_DIM
            o_ref[:, c0:c0 + HEAD_DIM] = out[h * tq:(h + 1) * tq].astype(o_ref.dtype)


def _dsa_attention(q_a, k_a, proj, v_col, bias, batch, seq, tq=512):
    tk = bias.shape[-1]
    tq = min(tq, seq)
    n_qt, n_kt = seq // tq, seq // tk
    rows = GROUP_A * tq
    return pl.pallas_call(
        functools.partial(_dsa_attn_kernel, tq=tq, tk=tk),
        grid=(batch, n_qt),
        in_specs=[
            pl.BlockSpec((N_HEADS_A, tq, HEAD_DIM), lambda b, i: (0, b * n_qt + i, 0)),
            pl.BlockSpec((seq, KV_A_W), lambda b, i: (b, 0)),
            pl.BlockSpec((seq, KV_A_W), lambda b, i: (b, v_col // KV_A_W)),
            pl.BlockSpec((1, n_kt, tq, tk), lambda b, i: (b, 0, i, 0)),
        ],
        out_specs=pl.BlockSpec((tq, W_A), lambda b, i: (b * n_qt + i, 0)),
        out_shape=jax.ShapeDtypeStruct((batch * seq, W_A), jnp.bfloat16),
        scratch_shapes=[
            pltpu.VMEM((N_KV_A, rows, LANES), jnp.float32),
            pltpu.VMEM((N_KV_A, rows, LANES), jnp.float32),
            pltpu.VMEM((N_KV_A, rows, HEAD_DIM), jnp.float32),
        ],
        compiler_params=_cparams(("parallel", "parallel")),
        name="dsa_attention",
    )(q_a, k_a, proj, bias)


FOX_HEADS_PER_STEP = 4


def _fox_kernel(q_ref, k_ref, v_ref, gq_ref, gk_ref, fq_ref, fk_ref, o_ref, kn_ref, m_ref, l_ref, acc_ref, *, tile):
    hp, qi = pl.program_id(1), pl.program_id(2)
    n_t = kn_ref.shape[0] // tile

    @pl.when(qi == 0)
    def _():
        def norm_keys(j, carry):
            off = pl.multiple_of(j * tile, tile)
            for e in range(FOX_HEADS_PER_STEP):
                cols = slice(e * HEAD_DIM, (e + 1) * HEAD_DIM)
                k = k_ref[pl.ds(off, tile), cols].astype(jnp.float32)
                kn_ref[pl.ds(off, tile), cols] = _head_rms(k, gk_ref[...]).astype(kn_ref.dtype)
            return carry

        lax.fori_loop(0, n_t, norm_keys, 0)

    lane = lax.broadcasted_iota(jnp.int32, (tile, LANES), 1)
    heads = [hp * FOX_HEADS_PER_STEP + e for e in range(FOX_HEADS_PER_STEP)]
    fq = [jnp.broadcast_to(jnp.sum(jnp.where(lane == MISC_F_LANE + h, fq_ref[...], 0.0), axis=1, keepdims=True),
                           (tile, LANES)) for h in heads]
    qn = [_head_rms(q_ref[:, e * HEAD_DIM:(e + 1) * HEAD_DIM].astype(jnp.float32), gq_ref[...]).astype(kn_ref.dtype)
          for e in range(FOX_HEADS_PER_STEP)]
    _softmax_init(m_ref, l_ref, acc_ref)

    def kv_step(j, diagonal):
        off = pl.multiple_of(j * tile, tile)
        for e, h in enumerate(heads):
            cols = slice(e * HEAD_DIM, (e + 1) * HEAD_DIM)
            s = _nt_dot(qn[e], kn_ref[pl.ds(off, tile), cols]) - fk_ref[j, pl.ds(h, 1), :]
            if diagonal:
                t_pos = lax.broadcasted_iota(jnp.int32, (tile, tile), 0)
                s_pos = lax.broadcasted_iota(jnp.int32, (tile, tile), 1)
                s = jnp.where(s_pos <= t_pos, s, MASK_NEG)
            _softmax_update(s, fq[e], v_ref[pl.ds(off, tile), cols], m_ref, l_ref, acc_ref, e)

    def full_step(j, carry):
        kv_step(j, False)
        return carry

    lax.fori_loop(0, qi, full_step, 0)
    kv_step(qi, True)

    for e in range(FOX_HEADS_PER_STEP):
        o_ref[:, e * HEAD_DIM:(e + 1) * HEAD_DIM] = (acc_ref[e] / l_ref[e]).astype(o_ref.dtype)


def _fox_attention(proj, q_col, g_q, g_k, wf, f_row, batch, seq):
    tile = min(KV_TILE, seq)
    n_t = seq // tile
    wb = FOX_HEADS_PER_STEP * HEAD_DIM
    q_blk, k_blk, v_blk = q_col // wb, (q_col + W_B) // wb, (q_col + 2 * W_B) // wb
    gain_spec = pl.BlockSpec((1, HEAD_DIM), lambda b, h, i: (0, 0))
    return pl.pallas_call(
        functools.partial(_fox_kernel, tile=tile),
        grid=(batch, N_HEADS_B // FOX_HEADS_PER_STEP, n_t),
        in_specs=[
            pl.BlockSpec((tile, wb), lambda b, h, i: (b * n_t + i, q_blk + h)),
            pl.BlockSpec((seq, wb), lambda b, h, i: (b, k_blk + h)),
            pl.BlockSpec((seq, wb), lambda b, h, i: (b, v_blk + h)),
            gain_spec, gain_spec,
            pl.BlockSpec((tile, LANES), lambda b, h, i: (b * n_t + i, 0)),
            pl.BlockSpec((n_t, N_HEADS_B, tile), lambda b, h, i: (b, 0, 0)),
        ],
        out_specs=pl.BlockSpec((tile, wb), lambda b, h, i: (b * n_t + i, h)),
        out_shape=jax.ShapeDtypeStruct((batch * seq, W_B), jnp.bfloat16),
        scratch_shapes=[
            pltpu.VMEM((seq, wb), jnp.bfloat16),
            pltpu.VMEM((FOX_HEADS_PER_STEP, tile, LANES), jnp.float32),
            pltpu.VMEM((FOX_HEADS_PER_STEP, tile, LANES), jnp.float32),
            pltpu.VMEM((FOX_HEADS_PER_STEP, tile, HEAD_DIM), jnp.float32),
        ],
        compiler_params=_cparams(("parallel", "parallel", "arbitrary")),
        name="fox_attention",
    )(proj, proj, proj, g_q.reshape(1, HEAD_DIM), g_k.reshape(1, HEAD_DIM), wf, f_row)


def _layer(x, p, pos_col, batch, seq, g_attn, w_in, g_cq, w_uq, w_uq_idx, g_kidx, b_kidx, g_q_a, g_k_a, b_forget,
           g_q_b, g_k_b, w_up_a, w_up_b, w_o, g_ffn, w_ffn_gate, w_ffn_up, w_ffn_down, g_ple, w_ple, w_ple_gate):
    d = x.shape[1]
    bf = jnp.bfloat16
    top_k = min(INDEX_TOPK, seq // 4)

    o_kidx = Q_RANK + 2 * KV_A_W
    o_qb = o_kidx + IDX_DIM + N_IDX_HEADS
    o_fb = o_qb + 3 * W_B
    o_ga = o_fb + N_HEADS_B
    w_in_t = jnp.swapaxes(w_in, 0, 1)
    w_main = _stage_regroup(w_in_t, [(0, o_kidx), (o_qb, o_fb - o_qb), (o_ga, 2 * d)])
    w_misc = _stage_misc(w_in_t, o_kidx, o_qb - o_kidx, o_fb, o_ga - o_fb)
    col_ka, col_va = Q_RANK, Q_RANK + KV_A_W
    col_qb = Q_RANK + 2 * KV_A_W
    col_vb = col_qb + 2 * W_B
    col_ga = col_vb + W_B
    col_gb = col_ga + d

    tabs = _rope_tables(pos_col)
    cf, sf = tabs[0], tabs[1]

    h = _rmsnorm(x, g_attn)
    proj = _matmul(h, w_main, bf, name="proj_main")
    misc = _matmul(h, w_misc, jnp.float32, tn=LANES, name="proj_misc")

    q_a, q_idx = _q_proj(proj, g_cq, w_uq.astype(bf), w_uq_idx.astype(bf), g_q_a, tabs)
    k_a = _ka_norm_rope(proj, col_ka, g_k_a, cf, sf)
    k_idx, wf, w_t, f_row = _misc_post(misc, g_kidx, b_kidx, b_forget, tabs, seq)
    bias = _indexer_mask(q_idx, k_idx, w_t, batch, seq, top_k)
    o_a = _dsa_attention(q_a, k_a, proj, col_va, bias, batch, seq)

    o_b = _fox_attention(proj, col_qb, g_q_b * Q_SCALE, g_k_b, wf, f_row, batch, seq)

    merged = _merge(o_a, o_b, w_up_a, w_up_b, proj, col_ga, col_gb)
    x = _matmul_residual(merged, w_o, x, name="out_proj")

    h = _rmsnorm(x, g_ffn)
    u = _swiglu(h, w_ffn_gate, w_ffn_up, tn=FFN_TILE)
    x = _matmul_residual(u, w_ffn_down.astype(bf), x, tm=512, name="ffn_down")

    h = _rmsnorm(x, g_ple)
    x = _ple(h, w_ple_gate, p.astype(bf), w_ple, x)
    return x


def kernel(x, p, positions, g_attn, w_in, g_cq, w_uq, w_uq_idx, g_kidx, b_kidx, g_q_a, g_k_a, b_forget, g_q_b, g_k_b,
           w_up_a, w_up_b, w_o, g_ffn, w_ffn_gate, w_ffn_up, w_ffn_down, g_ple, w_ple, w_ple_gate):
    batch, seq, d = x.shape
    depth = w_in.shape[0]
    xf = x.reshape(batch * seq, d)
    pos_col = positions.reshape(batch * seq, 1)
    for i in range(depth):
        xf = _layer(xf, p[i].reshape(batch * seq, -1), pos_col, batch, seq, g_attn[i], w_in[i], g_cq[i], w_uq[i],
                    w_uq_idx[i], g_kidx[i], b_kidx[i], g_q_a[i], g_k_a[i], b_forget[i], g_q_b[i], g_k_b[i],
                    w_up_a[i], w_up_b[i], w_o[i], g_ffn[i], w_ffn_gate[i], w_ffn_up[i], w_ffn_down[i], g_ple[i],
                    w_ple[i], w_ple_gate[i])
    return xf.reshape(batch, seq, d)
```

```python
import functools

import numpy as np
import jax
import jax.numpy as jnp
from jax import lax
from jax.experimental import pallas as pl
from jax.experimental.pallas import tpu as pltpu

CHUNK = 64
CHUNK_SHIFT = CHUNK.bit_length() - 1
HEAD_DIM = 128
ROPE_THETA = 10000.0
EPS = 1e-6
N_HEADS_A = 16
N_KV_A = 2
Q_RANK = 1024
N_IDX_HEADS = 32
IDX_DIM = 64
IDX_ROPE_DIM = 32
INDEX_TOPK = 256
N_HEADS_B = 16
PLE_DIM = 256

LANES = 128
KV_A_W = N_KV_A * HEAD_DIM
W_A = N_HEADS_A * HEAD_DIM
W_B = N_HEADS_B * HEAD_DIM
GROUP_A = N_HEADS_A // N_KV_A

MISC_W_LANE = IDX_DIM
MISC_F_LANE = IDX_DIM + N_IDX_HEADS

MASK_NEG = -1e30
SCORE_NEG = -3e38
SELECT_ITERS = 32
LOG2E = 1.4426950408889634
Q_SCALE = HEAD_DIM ** -0.5 * LOG2E
KV_TILE = 512
FFN_TILE = 256
VMEM_LIMIT = 56 * 1024 * 1024


def _cparams(sem):
    return pltpu.CompilerParams(dimension_semantics=sem, vmem_limit_bytes=VMEM_LIMIT)


def _sigmoid(x):
    return 1.0 / (1.0 + jnp.exp(-x))


def _nt_dot(a, b):
    return lax.dot_general(a, b, (((1,), (1,)), ((), ())), preferred_element_type=jnp.float32)


def _wdot(a, w_ref):
    return jnp.dot(a, w_ref[...].astype(a.dtype), preferred_element_type=jnp.float32)


def _stage_regroup_kernel(w_ref, o_ref):
    o_ref[...] = w_ref[...].T.astype(o_ref.dtype)


def _stage_regroup(w_t, segments, tn=512, tk=1024):
    n_src, k = w_t.shape
    lags, dst = [], 0
    for src, width in segments:
        assert width % tn == 0 and src % 16 == 0 and src + width <= n_src
        lags.append((dst // tn, src - dst))
        dst += width
    tk = min(tk, k)

    def src_row(j):
        lag = lags[0][1]
        for first_tile, seg_lag in lags[1:]:
            lag = jnp.where(j >= first_tile, seg_lag, lag)
        return pl.multiple_of(j * tn + lag, 16)

    return pl.pallas_call(
        _stage_regroup_kernel,
        grid=(dst // tn, k // tk),
        in_specs=[pl.BlockSpec((pl.Element(tn), pl.Element(tk)),
                               lambda j, kk: (src_row(j), pl.multiple_of(kk * tk, tk)))],
        out_specs=pl.BlockSpec((tk, tn), lambda j, kk: (kk, j)),
        out_shape=jax.ShapeDtypeStruct((k, dst), jnp.bfloat16),
        compiler_params=_cparams(("parallel", "parallel")),
        name="stage_regroup",
    )(w_t)


def _stage_misc_kernel(a_ref, b_ref, o_ref):
    a, b = a_ref[...], b_ref[...]
    pad = jnp.zeros((LANES - a.shape[0] - b.shape[0], a.shape[1]), a.dtype)
    o_ref[...] = jnp.concatenate([a, b, pad], axis=0).T.astype(o_ref.dtype)


def _stage_misc(w_t, start_a, n_a, start_b, n_b, tk=1024):
    k = w_t.shape[1]
    tk = min(tk, k)
    col = lambda kk: pl.multiple_of(kk * tk, tk)
    return pl.pallas_call(
        _stage_misc_kernel,
        grid=(k // tk,),
        in_specs=[
            pl.BlockSpec((pl.Element(n_a), pl.Element(tk)), lambda kk: (start_a, col(kk))),
            pl.BlockSpec((pl.Element(n_b), pl.Element(tk)), lambda kk: (start_b, col(kk))),
        ],
        out_specs=pl.BlockSpec((tk, LANES), lambda kk: (kk, 0)),
        out_shape=jax.ShapeDtypeStruct((k, LANES), jnp.bfloat16),
        compiler_params=_cparams(("parallel",)),
        name="stage_misc",
    )(w_t, w_t)


def _rmsnorm_kernel(x_ref, g_ref, o_ref):
    x = x_ref[...]
    ms = jnp.mean(x * x, axis=-1, keepdims=True)
    o_ref[...] = (x * lax.rsqrt(ms + EPS) * g_ref[...]).astype(o_ref.dtype)


def _rmsnorm(x, g, tm=256):
    t, d = x.shape
    tm = min(tm, t)
    return pl.pallas_call(
        _rmsnorm_kernel,
        grid=(t // tm,),
        in_specs=[pl.BlockSpec((tm, d), lambda i: (i, 0)), pl.BlockSpec((1, d), lambda i: (0, 0))],
        out_specs=pl.BlockSpec((tm, d), lambda i: (i, 0)),
        out_shape=jax.ShapeDtypeStruct((t, d), jnp.bfloat16),
        compiler_params=_cparams(("parallel",)),
        name="rmsnorm",
    )(x, g.reshape(1, d))


def _mm_kernel(a_ref, w_ref, o_ref):
    o_ref[...] = jnp.dot(a_ref[...], w_ref[...], preferred_element_type=jnp.float32).astype(o_ref.dtype)


def _matmul(a, w, out_dtype, tm=1024, tn=512, name="matmul"):
    t, k = a.shape
    n = w.shape[1]
    tm, tn = min(tm, t), min(tn, n)
    return pl.pallas_call(
        _mm_kernel,
        grid=(t // tm, n // tn),
        in_specs=[pl.BlockSpec((tm, k), lambda i, j: (i, 0)), pl.BlockSpec((k, tn), lambda i, j: (0, j))],
        out_specs=pl.BlockSpec((tm, tn), lambda i, j: (i, j)),
        out_shape=jax.ShapeDtypeStruct((t, n), out_dtype),
        compiler_params=_cparams(("parallel", "parallel")),
        name=name,
    )(a, w)


def _mm_res_norm_kernel(a_ref, w_ref, r_ref, g_ref, o_ref, xg_ref, ssq_ref):
    y = r_ref[...] + _wdot(a_ref[...], w_ref)
    o_ref[...] = y
    xg_ref[...] = (y * g_ref[...]).astype(xg_ref.dtype)
    part = jnp.broadcast_to(jnp.sum(y * y, axis=-1, keepdims=True), ssq_ref.shape)

    @pl.when(pl.program_id(1) == 0)
    def _():
        ssq_ref[...] = part

    @pl.when(pl.program_id(1) > 0)
    def _():
        ssq_ref[...] += part


def _matmul_residual_norm(a, w, r, g, tm=1024, tn=512, name="matmul_residual_norm"):
    t, k = a.shape
    n = w.shape[1]
    tm, tn = min(tm, t), min(tn, n)
    tile = pl.BlockSpec((tm, tn), lambda i, j: (i, j))
    return pl.pallas_call(
        _mm_res_norm_kernel,
        grid=(t // tm, n // tn),
        in_specs=[
            pl.BlockSpec((tm, k), lambda i, j: (i, 0)),
            pl.BlockSpec((k, tn), lambda i, j: (0, j)),
            tile,
            pl.BlockSpec((1, tn), lambda i, j: (0, j)),
        ],
        out_specs=[tile, tile, pl.BlockSpec((tm, LANES), lambda i, j: (i, 0))],
        out_shape=[
            jax.ShapeDtypeStruct((t, n), jnp.float32),
            jax.ShapeDtypeStruct((t, n), jnp.bfloat16),
            jax.ShapeDtypeStruct((t, LANES), jnp.float32),
        ],
        compiler_params=_cparams(("parallel", "arbitrary")),
        name=name,
    )(a, w, r, g.reshape(1, n))


def _row_rstd(ssq_ref, width):
    return lax.rsqrt(ssq_ref[...] * (1.0 / width) + EPS)


def _scale_rows(acc, rstd):
    return jnp.concatenate([acc[:, c * LANES:(c + 1) * LANES] * rstd for c in range(acc.shape[1] // LANES)], axis=1)


def _swiglu_kernel(a_ref, ssq_ref, wg_ref, wu_ref, o_ref):
    a = a_ref[...]
    rstd = _row_rstd(ssq_ref, a.shape[1])
    g = _scale_rows(_wdot(a, wg_ref), rstd)
    u = _scale_rows(_wdot(a, wu_ref), rstd)
    o_ref[...] = (g * _sigmoid(g) * u).astype(o_ref.dtype)


def _swiglu(xg, ssq, wg, wu, tm=1024, tn=512):
    t, k = xg.shape
    n = wg.shape[1]
    tm, tn = min(tm, t), min(tn, n)
    return pl.pallas_call(
        _swiglu_kernel,
        grid=(t // tm, n // tn),
        in_specs=[
            pl.BlockSpec((tm, k), lambda i, j: (i, 0)),
            pl.BlockSpec((tm, LANES), lambda i, j: (i, 0)),
            pl.BlockSpec((k, tn), lambda i, j: (0, j)),
            pl.BlockSpec((k, tn), lambda i, j: (0, j)),
        ],
        out_specs=pl.BlockSpec((tm, tn), lambda i, j: (i, j)),
        out_shape=jax.ShapeDtypeStruct((t, n), jnp.bfloat16),
        compiler_params=_cparams(("parallel", "parallel")),
        name="swiglu",
    )(xg, ssq, wg, wu)


def _merge_kernel(oa_ref, ob_ref, wa_ref, wb_ref, ga_ref, gb_ref, o_ref):
    a = _wdot(oa_ref[...], wa_ref)
    b = _wdot(ob_ref[...], wb_ref)
    ga = _sigmoid(ga_ref[...].astype(jnp.float32))
    gb = _sigmoid(gb_ref[...].astype(jnp.float32))
    o_ref[...] = (ga * a + gb * b).astype(o_ref.dtype)


def _merge(o_a, o_b, w_up_a, w_up_b, proj, ga_col, gb_col, tm=1024, tn=512):
    t, ka = o_a.shape
    kb = o_b.shape[1]
    n = w_up_a.shape[1]
    tm, tn = min(tm, t), min(tn, n)
    ga_blk, gb_blk = ga_col // tn, gb_col // tn
    return pl.pallas_call(
        _merge_kernel,
        grid=(t // tm, n // tn),
        in_specs=[
            pl.BlockSpec((tm, ka), lambda i, j: (i, 0)),
            pl.BlockSpec((tm, kb), lambda i, j: (i, 0)),
            pl.BlockSpec((ka, tn), lambda i, j: (0, j)),
            pl.BlockSpec((kb, tn), lambda i, j: (0, j)),
            pl.BlockSpec((tm, tn), lambda i, j: (i, ga_blk + j)),
            pl.BlockSpec((tm, tn), lambda i, j: (i, gb_blk + j)),
        ],
        out_specs=pl.BlockSpec((tm, tn), lambda i, j: (i, j)),
        out_shape=jax.ShapeDtypeStruct((t, n), jnp.bfloat16),
        compiler_params=_cparams(("parallel", "parallel")),
        name="merge",
    )(o_a, o_b, w_up_a, w_up_b, proj, proj)


def _ple_kernel(xg_ref, ssq_ref, wg_ref, p_ref, wp_ref, r_ref, o_ref):
    a = xg_ref[...]
    g = _scale_rows(_wdot(a, wg_ref), _row_rstd(ssq_ref, a.shape[1]))
    e = _wdot(p_ref[...], wp_ref)
    o_ref[...] = r_ref[...] + _sigmoid(g) * e


def _ple(xg, ssq, w_gate, p, w_ple, r, tm=1024, tn=512):
    t, k = xg.shape
    kp = p.shape[1]
    n = w_gate.shape[1]
    tm, tn = min(tm, t), min(tn, n)
    return pl.pallas_call(
        _ple_kernel,
        grid=(t // tm, n // tn),
        in_specs=[
            pl.BlockSpec((tm, k), lambda i, j: (i, 0)),
            pl.BlockSpec((tm, LANES), lambda i, j: (i, 0)),
            pl.BlockSpec((k, tn), lambda i, j: (0, j)),
            pl.BlockSpec((tm, kp), lambda i, j: (i, 0)),
            pl.BlockSpec((kp, tn), lambda i, j: (0, j)),
            pl.BlockSpec((tm, tn), lambda i, j: (i, j)),
        ],
        out_specs=pl.BlockSpec((tm, tn), lambda i, j: (i, j)),
        out_shape=jax.ShapeDtypeStruct((t, n), jnp.float32),
        compiler_params=_cparams(("parallel", "parallel")),
        name="ple",
    )(xg, ssq, w_gate, p, w_ple, r)


def _rope_consts():
    half = HEAD_DIM // 2
    inv_full = jnp.power(ROPE_THETA, -jnp.arange(half, dtype=jnp.float32) * (2.0 / HEAD_DIM))
    half_i = IDX_ROPE_DIM // 2
    inv_idx = jnp.power(ROPE_THETA, -jnp.arange(half_i, dtype=jnp.float32) * (2.0 / IDX_ROPE_DIM))
    zeros = jnp.zeros((IDX_DIM - IDX_ROPE_DIM,), jnp.float32)
    sign = np.concatenate([-np.ones(half, np.float32), np.ones(half, np.float32)])
    mask_a = np.zeros(IDX_DIM, np.float32)
    mask_a[:half_i] = -1.0
    mask_b = np.zeros(IDX_DIM, np.float32)
    mask_b[half_i:IDX_ROPE_DIM] = 1.0
    reps = LANES // IDX_DIM
    rows = [
        jnp.concatenate([inv_full, inv_full]),
        jnp.asarray(sign),
        jnp.tile(jnp.concatenate([inv_idx, inv_idx, zeros]), reps),
        jnp.asarray(np.tile(mask_a, reps)),
        jnp.asarray(np.tile(mask_b, reps)),
    ]
    rows += [jnp.zeros((LANES,), jnp.float32)] * 3
    return jnp.stack(rows)


def _rope_tables_kernel(pos_ref, c_ref, cf_ref, sf_ref, ci_ref, sa_ref, sb_ref):
    pos = pos_ref[...].astype(jnp.float32)
    ang = pos * c_ref[0:1, :]
    cf_ref[...] = jnp.cos(ang)
    sf_ref[...] = jnp.sin(ang) * c_ref[1:2, :]
    ang_i = pos * c_ref[2:3, :]
    ci_ref[...] = jnp.cos(ang_i)
    s_i = jnp.sin(ang_i)
    sa_ref[...] = s_i * c_ref[3:4, :]
    sb_ref[...] = s_i * c_ref[4:5, :]


def _rope_tables(pos_col, tm=512):
    t = pos_col.shape[0]
    tm = min(tm, t)
    tab = jax.ShapeDtypeStruct((t, LANES), jnp.float32)
    spec = pl.BlockSpec((tm, LANES), lambda i: (i, 0))
    return pl.pallas_call(
        _rope_tables_kernel,
        grid=(t // tm,),
        in_specs=[pl.BlockSpec((tm, 1), lambda i: (i, 0)), pl.BlockSpec((8, LANES), lambda i: (0, 0))],
        out_specs=[spec] * 5,
        out_shape=[tab] * 5,
        compiler_params=_cparams(("parallel",)),
        name="rope_tables",
    )(pos_col, _rope_consts())


def _rope_full(x, cf, sf):
    return x * cf + pltpu.roll(x, HEAD_DIM // 2, axis=1) * sf


def _rope_idx(x, ci, sa, sb):
    half = IDX_ROPE_DIM // 2
    return x * ci + pltpu.roll(x, LANES - half, axis=1) * sa + pltpu.roll(x, half, axis=1) * sb


def _head_rms(x, g):
    ms = jnp.mean(x * x, axis=-1, keepdims=True)
    return x * lax.rsqrt(ms + EPS) * g


def _q_proj_kernel(cq_ref, gcq_ref, wa_ref, wi_ref, gqa_ref, cf_ref, sf_ref, ci_ref, sa_ref, sb_ref, qa_ref, qi_ref):
    cq = cq_ref[...].astype(jnp.float32)
    ms = jnp.mean(cq * cq, axis=-1, keepdims=True)
    cq = (cq * lax.rsqrt(ms + EPS) * gcq_ref[...]).astype(jnp.bfloat16)
    cf, sf = cf_ref[...], sf_ref[...]
    gqa = gqa_ref[...]
    for j in range(N_HEADS_A // 2):
        acc = jnp.dot(cq, wa_ref[:, j * 2 * HEAD_DIM:(j + 1) * 2 * HEAD_DIM], preferred_element_type=jnp.float32)
        for e in range(2):
            hq = _head_rms(acc[:, e * HEAD_DIM:(e + 1) * HEAD_DIM], gqa)
            qa_ref[2 * j + e] = (_rope_full(hq, cf, sf) * Q_SCALE).astype(qa_ref.dtype)
    ci, sa, sb = ci_ref[...], sa_ref[...], sb_ref[...]
    low_half = lax.broadcasted_iota(jnp.int32, (cq.shape[0], LANES), 1) < IDX_DIM
    per_dot = 2 * LANES // IDX_DIM
    for j in range(N_IDX_HEADS // per_dot):
        acc = jnp.dot(cq, wi_ref[:, j * 2 * LANES:(j + 1) * 2 * LANES], preferred_element_type=jnp.float32)
        for c in range(2):
            pair = _rope_idx(acc[:, c * LANES:(c + 1) * LANES], ci, sa, sb)
            for e, head in enumerate((pair, pltpu.roll(pair, IDX_DIM, axis=1))):
                qi_ref[per_dot * j + 2 * c + e] = jnp.where(low_half, head, 0.0).astype(qi_ref.dtype)


def _q_proj(proj, g_cq, w_uq, w_uq_idx, g_q_a, tabs, tm=512):
    t = proj.shape[0]
    tm = min(tm, t)
    cf, sf, ci, sa, sb = tabs
    tab_spec = pl.BlockSpec((tm, LANES), lambda i: (i, 0))
    return pl.pallas_call(
        _q_proj_kernel,
        grid=(t // tm,),
        in_specs=[
            pl.BlockSpec((tm, Q_RANK), lambda i: (i, 0)),
            pl.BlockSpec((1, Q_RANK), lambda i: (0, 0)),
            pl.BlockSpec((Q_RANK, W_A), lambda i: (0, 0)),
            pl.BlockSpec((Q_RANK, N_IDX_HEADS * IDX_DIM), lambda i: (0, 0)),
            pl.BlockSpec((1, HEAD_DIM), lambda i: (0, 0)),
            tab_spec, tab_spec, tab_spec, tab_spec, tab_spec,
        ],
        out_specs=[
            pl.BlockSpec((N_HEADS_A, tm, HEAD_DIM), lambda i: (0, i, 0)),
            pl.BlockSpec((N_IDX_HEADS, tm, LANES), lambda i: (0, i, 0)),
        ],
        out_shape=[
            jax.ShapeDtypeStruct((N_HEADS_A, t, HEAD_DIM), jnp.bfloat16),
            jax.ShapeDtypeStruct((N_IDX_HEADS, t, LANES), jnp.bfloat16),
        ],
        compiler_params=_cparams(("parallel",)),
        name="q_proj",
    )(proj, g_cq.reshape(1, Q_RANK), w_uq, w_uq_idx, g_q_a.reshape(1, HEAD_DIM), cf, sf, ci, sa, sb)


def _ka_kernel(x_ref, g_ref, cf_ref, sf_ref, o_ref):
    x = x_ref[...].astype(jnp.float32)
    g, cf, sf = g_ref[...], cf_ref[...], sf_ref[...]
    for h in range(N_KV_A):
        hk = _head_rms(x[:, h * HEAD_DIM:(h + 1) * HEAD_DIM], g)
        o_ref[:, h * HEAD_DIM:(h + 1) * HEAD_DIM] = _rope_full(hk, cf, sf).astype(o_ref.dtype)


def _ka_norm_rope(proj, col, g_k_a, cf, sf, tm=512):
    t = proj.shape[0]
    tm = min(tm, t)
    tab_spec = pl.BlockSpec((tm, LANES), lambda i: (i, 0))
    return pl.pallas_call(
        _ka_kernel,
        grid=(t // tm,),
        in_specs=[
            pl.BlockSpec((tm, KV_A_W), lambda i: (i, col // KV_A_W)),
            pl.BlockSpec((1, HEAD_DIM), lambda i: (0, 0)),
            tab_spec, tab_spec,
        ],
        out_specs=pl.BlockSpec((tm, KV_A_W), lambda i: (i, 0)),
        out_shape=jax.ShapeDtypeStruct((t, KV_A_W), jnp.bfloat16),
        compiler_params=_cparams(("parallel",)),
        name="ka_norm_rope",
    )(proj, g_k_a.reshape(1, HEAD_DIM), cf, sf)


def _split3_bf16(x):
    hi = x.astype(jnp.bfloat16)
    r1 = x - hi.astype(jnp.float32)
    mid = r1.astype(jnp.bfloat16)
    lo = (r1 - mid.astype(jnp.float32)).astype(jnp.bfloat16)
    return hi, mid, lo


def _misc_kernel(m_ref, c_ref, ci_ref, sa_ref, sb_ref, tri_ref, kidx_ref, wf_ref, wt_ref, frow_ref, carry_ref, *,
                 tiles_per_seq):
    i = pl.program_id(0)

    @pl.when(i % tiles_per_seq == 0)
    def _():
        carry_ref[...] = jnp.zeros_like(carry_ref)

    x = m_ref[...]
    lane = lax.broadcasted_iota(jnp.int32, x.shape, 1)
    is_k = lane < IDX_DIM
    xk = jnp.where(is_k, x, 0.0)
    mu = jnp.sum(xk, axis=-1, keepdims=True) * (1.0 / IDX_DIM)
    dk = jnp.where(is_k, x - mu, 0.0)
    var = jnp.sum(dk * dk, axis=-1, keepdims=True) * (1.0 / IDX_DIM)
    y = dk * lax.rsqrt(var + EPS) * c_ref[0:1, :] + c_ref[1:2, :]
    y = _rope_idx(y, ci_ref[...], sa_ref[...], sb_ref[...])
    kidx_ref[...] = jnp.where(is_k, y, 0.0).astype(kidx_ref.dtype)
    f = x + c_ref[2:3, :]
    log_f = jnp.minimum(f, 0.0) - jnp.log1p(jnp.exp(-jnp.abs(f)))
    hi, mid, lo = _split3_bf16(log_f)
    tri = tri_ref[...]
    csum = (jnp.dot(tri, hi, preferred_element_type=jnp.float32)
            + jnp.dot(tri, mid, preferred_element_type=jnp.float32)
            + jnp.dot(tri, lo, preferred_element_type=jnp.float32))
    csum = csum + carry_ref[0:1, :]
    carry_ref[...] = jnp.broadcast_to(csum[-1:, :], carry_ref.shape)
    is_w = (lane >= MISC_W_LANE) & (lane < MISC_F_LANE)
    wf = jnp.where(is_w, x * c_ref[3:4, :], csum * LOG2E)
    wf_ref[...] = wf
    wf_t = wf.T
    wt_ref[...] = wf_t[MISC_W_LANE:MISC_F_LANE, :]
    frow_ref[0] = wf_t[MISC_F_LANE:MISC_F_LANE + N_HEADS_B, :]


def _misc_post(misc, g_kidx, b_kidx, b_forget, tabs, seq):
    t = misc.shape[0]
    tm = min(KV_TILE, seq)
    idx_w_scale = (N_IDX_HEADS ** -0.5) * (IDX_DIM ** -0.5)
    pad = lambda v, off: jnp.zeros((LANES,), jnp.float32).at[off:off + v.shape[0]].set(v)
    consts = jnp.stack([
        pad(g_kidx, 0), pad(b_kidx, 0), pad(b_forget, MISC_F_LANE),
        pad(jnp.full((N_IDX_HEADS,), idx_w_scale, jnp.float32), MISC_W_LANE),
    ] + [jnp.zeros((LANES,), jnp.float32)] * 4)
    tri = jnp.asarray(np.tril(np.ones((tm, tm), np.float32)), jnp.bfloat16)
    _, _, ci, sa, sb = tabs
    tab_spec = pl.BlockSpec((tm, LANES), lambda i: (i, 0))
    return pl.pallas_call(
        functools.partial(_misc_kernel, tiles_per_seq=seq // tm),
        grid=(t // tm,),
        in_specs=[
            tab_spec,
            pl.BlockSpec((8, LANES), lambda i: (0, 0)),
            tab_spec, tab_spec, tab_spec,
            pl.BlockSpec((tm, tm), lambda i: (0, 0)),
        ],
        out_specs=[
            tab_spec, tab_spec,
            pl.BlockSpec((N_IDX_HEADS, tm), lambda i: (0, i)),
            pl.BlockSpec((1, N_HEADS_B, tm), lambda i: (i, 0, 0)),
        ],
        out_shape=[
            jax.ShapeDtypeStruct((t, LANES), jnp.bfloat16),
            jax.ShapeDtypeStruct((t, LANES), jnp.float32),
            jax.ShapeDtypeStruct((N_IDX_HEADS, t), jnp.float32),
            jax.ShapeDtypeStruct((t // tm, N_HEADS_B, tm), jnp.float32),
        ],
        scratch_shapes=[pltpu.VMEM((8, LANES), jnp.float32)],
        compiler_params=_cparams(("arbitrary",)),
        name="misc_post",
    )(misc, consts, ci, sa, sb, tri)


IDX_HEAD_GROUP = 8
IDX_SLAB = 128
COUNT_ROWS = 16


def _indexer_kernel(q_ref, k_ref, w_ref, o_ref, sc_ref, *, tq, tk, n_kt, top_k):
    qi = pl.program_id(1)
    t0 = qi * tq
    n_vis = (t0 + tq + tk - 1) // tk
    t_row = lax.broadcasted_iota(jnp.int32, (1, tq), 1) + t0
    vis_end = (jnp.right_shift(t_row, CHUNK_SHIFT) + 1) * CHUNK

    def score_tile(c, carry):
        mx, mn = carry
        kt = k_ref[pl.ds(pl.multiple_of(c * tk, tk), tk), :]
        for g in range(N_IDX_HEADS // IDX_HEAD_GROUP):
            qg = q_ref[g * IDX_HEAD_GROUP:(g + 1) * IDX_HEAD_GROUP].reshape(IDX_HEAD_GROUP * tq, LANES)
            st = _nt_dot(kt, qg)
            for r in range(tk // IDX_SLAB):
                rows = slice(r * IDX_SLAB, (r + 1) * IDX_SLAB)
                part = jnp.zeros((IDX_SLAB, tq), jnp.float32)
                for j in range(IDX_HEAD_GROUP):
                    h = g * IDX_HEAD_GROUP + j
                    part = part + jnp.maximum(st[rows, j * tq:(j + 1) * tq], 0.0) * w_ref[h:h + 1, :]
                if g == 0:
                    sc_ref[c, rows, :] = part
                else:
                    sc_ref[c, rows, :] += part
        s_col = lax.broadcasted_iota(jnp.int32, (tk, 1), 0) + c * tk
        adm = s_col < vis_end
        sc = sc_ref[c]
        sc_ref[c] = jnp.where(adm, sc, SCORE_NEG)
        mx = jnp.maximum(mx, jnp.max(jnp.where(adm, sc, SCORE_NEG), axis=0, keepdims=True))
        mn = jnp.minimum(mn, jnp.min(jnp.where(adm, sc, -SCORE_NEG), axis=0, keepdims=True))
        return mx, mn

    hi0, lo0 = lax.fori_loop(
        0, n_vis, score_tile,
        (jnp.full((1, tq), SCORE_NEG, jnp.float32), jnp.full((1, tq), -SCORE_NEG, jnp.float32)))

    def unsettled(state):
        it, _, _, active = state
        return jnp.logical_and(it < SELECT_ITERS, jnp.max(active) > 0.0)

    def bisect(state):
        it, lo, hi, active = state
        mid = 0.5 * (lo + hi)

        def count_tile(c, cnt):
            ge = jnp.where(sc_ref[c] >= mid, 1.0, 0.0)
            return cnt + jnp.sum(ge.reshape(tk // COUNT_ROWS, COUNT_ROWS, tq), axis=0)

        cnt = lax.fori_loop(0, n_vis, count_tile, jnp.zeros((COUNT_ROWS, tq), jnp.float32))
        cnt = jnp.sum(cnt, axis=0, keepdims=True)
        enough = cnt >= float(top_k)
        active = jnp.where(cnt == float(top_k), 0.0, active)
        return it + 1, jnp.where(enough, mid, lo), jnp.where(enough, hi, mid), active

    active0 = jnp.where(vis_end > top_k, 1.0, 0.0)
    _, thr, _, _ = lax.while_loop(unsettled, bisect, (jnp.int32(0), lo0, hi0, active0))

    def write_tile(c, carry):
        sel = jnp.where(sc_ref[c] >= thr, 0.0, MASK_NEG)
        o_ref[0, c] = sel.T.astype(o_ref.dtype)
        return carry

    lax.fori_loop(0, n_vis, write_tile, 0)

    def fill_tile(c, carry):
        o_ref[0, c] = jnp.full((tq, tk), MASK_NEG, o_ref.dtype)
        return carry

    lax.fori_loop(n_vis, n_kt, fill_tile, 0)


def _indexer_mask(q_idx, k_idx, w_t, batch, seq, top_k, tq=256):
    tq, tk = min(tq, seq), min(KV_TILE, seq)
    n_qt, n_kt = seq // tq, seq // tk
    return pl.pallas_call(
        functools.partial(_indexer_kernel, tq=tq, tk=tk, n_kt=n_kt, top_k=top_k),
        grid=(batch, n_qt),
        in_specs=[
            pl.BlockSpec((N_IDX_HEADS, tq, LANES), lambda b, i: (0, b * n_qt + i, 0)),
            pl.BlockSpec((seq, LANES), lambda b, i: (b, 0)),
            pl.BlockSpec((N_IDX_HEADS, tq), lambda b, i: (0, b * n_qt + i)),
        ],
        out_specs=pl.BlockSpec((1, n_kt, tq, tk), lambda b, i: (b, 0, i, 0)),
        out_shape=jax.ShapeDtypeStruct((batch, n_kt, seq, tk), jnp.bfloat16),
        scratch_shapes=[pltpu.VMEM((n_kt, tk, tq), jnp.float32)],
        compiler_params=_cparams(("parallel", "parallel")),
        name="indexer_mask",
    )(q_idx, k_idx, w_t)


def _softmax_update(s, row_shift, v, m_ref, l_ref, acc_ref, e):
    rows, tk = s.shape
    chunks = [s[:, c * LANES:(c + 1) * LANES] for c in range(tk // LANES)]
    mx = functools.reduce(jnp.maximum, chunks)
    mx = jnp.broadcast_to(jnp.max(mx, axis=-1, keepdims=True), (rows, LANES))
    m_prev = m_ref[e]
    m_new = jnp.maximum(m_prev, mx + row_shift)
    alpha = jnp.exp2(m_prev - m_new)
    r = m_new - row_shift
    p = jnp.concatenate([jnp.exp2(c - r) for c in chunks], axis=1).astype(v.dtype)
    v_ones = jnp.concatenate([v, jnp.ones((tk, LANES), v.dtype)], axis=1)
    pv = jnp.dot(p, v_ones, preferred_element_type=jnp.float32)
    acc_ref[e] = alpha * acc_ref[e] + pv[:, :HEAD_DIM]
    l_ref[e] = alpha * l_ref[e] + pv[:, HEAD_DIM:]
    m_ref[e] = m_new


def _softmax_init(m_ref, l_ref, acc_ref):
    m_ref[...] = jnp.full(m_ref.shape, MASK_NEG, jnp.float32)
    l_ref[...] = jnp.zeros(l_ref.shape, jnp.float32)
    acc_ref[...] = jnp.zeros(acc_ref.shape, jnp.float32)


def _dsa_attn_kernel(q_ref, k_ref, v_ref, b_ref, o_ref, m_ref, l_ref, acc_ref, *, tq, tk):
    qi = pl.program_id(1)
    n_need = ((qi + 1) * tq + tk - 1) // tk
    rows = GROUP_A * tq
    _softmax_init(m_ref, l_ref, acc_ref)

    def kv_step(j, carry):
        off = pl.multiple_of(j * tk, tk)
        bias = b_ref[0, j].astype(jnp.float32)[None]
        for g in range(N_KV_A):
            cols = slice(g * HEAD_DIM, (g + 1) * HEAD_DIM)
            q = q_ref[g * GROUP_A:(g + 1) * GROUP_A].reshape(rows, HEAD_DIM)
            s = _nt_dot(q, k_ref[pl.ds(off, tk), cols])
            s = (s.reshape(GROUP_A, tq, tk) + bias).reshape(rows, tk)
            _softmax_update(s, 0.0, v_ref[pl.ds(off, tk), cols], m_ref, l_ref, acc_ref, g)
        return carry

    lax.fori_loop(0, n_need, kv_step, 0)

    for g in range(N_KV_A):
        out = acc_ref[g] / l_ref[g]
        for h in range(GROUP_A):
            c0 = (g * GROUP_A + h) * HEAD_DIM
            o_ref[:, c0:c0 + HEAD_DIM] = out[h * tq:(h + 1) * tq].astype(o_ref.dtype)


def _dsa_attention(q_a, k_a, proj, v_col, bias, batch, seq, tq=512):
    tk = bias.shape[-1]
    tq = min(tq, seq)
    n_qt, n_kt = seq // tq, seq // tk
    rows = GROUP_A * tq
    return pl.pallas_call(
        functools.partial(_dsa_attn_kernel, tq=tq, tk=tk),
        grid=(batch, n_qt),
        in_specs=[
            pl.BlockSpec((N_HEADS_A, tq, HEAD_DIM), lambda b, i: (0, b * n_qt + i, 0)),
            pl.BlockSpec((seq, KV_A_W), lambda b, i: (b, 0)),
            pl.BlockSpec((seq, KV_A_W), lambda b, i: (b, v_col // KV_A_W)),
            pl.BlockSpec((1, n_kt, tq, tk), lambda b, i: (b, 0, i, 0)),
        ],
        out_specs=pl.BlockSpec((tq, W_A), lambda b, i: (b * n_qt + i, 0)),
        out_shape=jax.ShapeDtypeStruct((batch * seq, W_A), jnp.bfloat16),
        scratch_shapes=[
            pltpu.VMEM((N_KV_A, rows, LANES), jnp.float32),
            pltpu.VMEM((N_KV_A, rows, LANES), jnp.float32),
            pltpu.VMEM((N_KV_A, rows, HEAD_DIM), jnp.float32),
        ],
        compiler_params=_cparams(("parallel", "parallel")),
        name="dsa_attention",
    )(q_a, k_a, proj, bias)


FOX_HEADS_PER_STEP = 4


def _fox_kernel(q_ref, k_ref, v_ref, gq_ref, gk_ref, fq_ref, fk_ref, o_ref, kn_ref, m_ref, l_ref, acc_ref, *, tile):
    hp, qi = pl.program_id(1), pl.program_id(2)
    n_t = kn_ref.shape[0] // tile

    @pl.when(qi == 0)
    def _():
        def norm_keys(j, carry):
            off = pl.multiple_of(j * tile, tile)
            for e in range(FOX_HEADS_PER_STEP):
                cols = slice(e * HEAD_DIM, (e + 1) * HEAD_DIM)
                k = k_ref[pl.ds(off, tile), cols].astype(jnp.float32)
                kn_ref[pl.ds(off, tile), cols] = _head_rms(k, gk_ref[...]).astype(kn_ref.dtype)
            return carry

        lax.fori_loop(0, n_t, norm_keys, 0)

    lane = lax.broadcasted_iota(jnp.int32, (tile, LANES), 1)
    heads = [hp * FOX_HEADS_PER_STEP + e for e in range(FOX_HEADS_PER_STEP)]
    fq = [jnp.broadcast_to(jnp.sum(jnp.where(lane == MISC_F_LANE + h, fq_ref[...], 0.0), axis=1, keepdims=True),
                           (tile, LANES)) for h in heads]
    qn = [_head_rms(q_ref[:, e * HEAD_DIM:(e + 1) * HEAD_DIM].astype(jnp.float32), gq_ref[...]).astype(kn_ref.dtype)
          for e in range(FOX_HEADS_PER_STEP)]
    _softmax_init(m_ref, l_ref, acc_ref)

    def kv_step(j, diagonal):
        off = pl.multiple_of(j * tile, tile)
        for e, h in enumerate(heads):
            cols = slice(e * HEAD_DIM, (e + 1) * HEAD_DIM)
            s = _nt_dot(qn[e], kn_ref[pl.ds(off, tile), cols]) - fk_ref[j, pl.ds(h, 1), :]
            if diagonal:
                t_pos = lax.broadcasted_iota(jnp.int32, (tile, tile), 0)
                s_pos = lax.broadcasted_iota(jnp.int32, (tile, tile), 1)
                s = jnp.where(s_pos <= t_pos, s, MASK_NEG)
            _softmax_update(s, fq[e], v_ref[pl.ds(off, tile), cols], m_ref, l_ref, acc_ref, e)

    def full_step(j, carry):
        kv_step(j, False)
        return carry

    lax.fori_loop(0, qi, full_step, 0)
    kv_step(qi, True)

    for e in range(FOX_HEADS_PER_STEP):
        o_ref[:, e * HEAD_DIM:(e + 1) * HEAD_DIM] = (acc_ref[e] / l_ref[e]).astype(o_ref.dtype)


def _fox_attention(proj, q_col, g_q, g_k, wf, f_row, batch, seq):
    tile = min(KV_TILE, seq)
    n_t = seq // tile
    wb = FOX_HEADS_PER_STEP * HEAD_DIM
    q_blk, k_blk, v_blk = q_col // wb, (q_col + W_B) // wb, (q_col + 2 * W_B) // wb
    gain_spec = pl.BlockSpec((1, HEAD_DIM), lambda b, h, i: (0, 0))
    return pl.pallas_call(
        functools.partial(_fox_kernel, tile=tile),
        grid=(batch, N_HEADS_B // FOX_HEADS_PER_STEP, n_t),
        in_specs=[
            pl.BlockSpec((tile, wb), lambda b, h, i: (b * n_t + i, q_blk + h)),
            pl.BlockSpec((seq, wb), lambda b, h, i: (b, k_blk + h)),
            pl.BlockSpec((seq, wb), lambda b, h, i: (b, v_blk + h)),
            gain_spec, gain_spec,
            pl.BlockSpec((tile, LANES), lambda b, h, i: (b * n_t + i, 0)),
            pl.BlockSpec((n_t, N_HEADS_B, tile), lambda b, h, i: (b, 0, 0)),
        ],
        out_specs=pl.BlockSpec((tile, wb), lambda b, h, i: (b * n_t + i, h)),
        out_shape=jax.ShapeDtypeStruct((batch * seq, W_B), jnp.bfloat16),
        scratch_shapes=[
            pltpu.VMEM((seq, wb), jnp.bfloat16),
            pltpu.VMEM((FOX_HEADS_PER_STEP, tile, LANES), jnp.float32),
            pltpu.VMEM((FOX_HEADS_PER_STEP, tile, LANES), jnp.float32),
            pltpu.VMEM((FOX_HEADS_PER_STEP, tile, HEAD_DIM), jnp.float32),
        ],
        compiler_params=_cparams(("parallel", "parallel", "arbitrary")),
        name="fox_attention",
    )(proj, proj, proj, g_q.reshape(1, HEAD_DIM), g_k.reshape(1, HEAD_DIM), wf, f_row)


def _layer(x, p, pos_col, batch, seq, g_attn, w_in, g_cq, w_uq, w_uq_idx, g_kidx, b_kidx, g_q_a, g_k_a, b_forget,
           g_q_b, g_k_b, w_up_a, w_up_b, w_o, g_ffn, w_ffn_gate, w_ffn_up, w_ffn_down, g_ple, w_ple, w_ple_gate):
    d = x.shape[1]
    bf = jnp.bfloat16
    top_k = min(INDEX_TOPK, seq // 4)

    o_kidx = Q_RANK + 2 * KV_A_W
    o_qb = o_kidx + IDX_DIM + N_IDX_HEADS
    o_fb = o_qb + 3 * W_B
    o_ga = o_fb + N_HEADS_B
    w_in_t = jnp.swapaxes(w_in, 0, 1)
    w_main = _stage_regroup(w_in_t, [(0, o_kidx), (o_qb, o_fb - o_qb), (o_ga, 2 * d)])
    w_misc = _stage_misc(w_in_t, o_kidx, o_qb - o_kidx, o_fb, o_ga - o_fb)
    col_ka, col_va = Q_RANK, Q_RANK + KV_A_W
    col_qb = Q_RANK + 2 * KV_A_W
    col_vb = col_qb + 2 * W_B
    col_ga = col_vb + W_B
    col_gb = col_ga + d

    tabs = _rope_tables(pos_col)
    cf, sf = tabs[0], tabs[1]

    h = _rmsnorm(x, g_attn)
    proj = _matmul(h, w_main, bf, name="proj_main")
    misc = _matmul(h, w_misc, jnp.float32, tn=LANES, name="proj_misc")

    q_a, q_idx = _q_proj(proj, g_cq, w_uq.astype(bf), w_uq_idx.astype(bf), g_q_a, tabs)
    k_a = _ka_norm_rope(proj, col_ka, g_k_a, cf, sf)
    k_idx, wf, w_t, f_row = _misc_post(misc, g_kidx, b_kidx, b_forget, tabs, seq)
    bias = _indexer_mask(q_idx, k_idx, w_t, batch, seq, top_k)
    o_a = _dsa_attention(q_a, k_a, proj, col_va, bias, batch, seq)

    o_b = _fox_attention(proj, col_qb, g_q_b * Q_SCALE, g_k_b, wf, f_row, batch, seq)

    merged = _merge(o_a, o_b, w_up_a, w_up_b, proj, col_ga, col_gb)
    x, xg, ssq = _matmul_residual_norm(merged, w_o, x, g_ffn, name="out_proj")

    u = _swiglu(xg, ssq, w_ffn_gate, w_ffn_up, tn=FFN_TILE)
    x, xg, ssq = _matmul_residual_norm(u, w_ffn_down.astype(bf), x, g_ple, tm=512, name="ffn_down")

    return _ple(xg, ssq, w_ple_gate, p.astype(bf), w_ple, x)


def kernel(x, p, positions, g_attn, w_in, g_cq, w_uq, w_uq_idx, g_kidx, b_kidx, g_q_a, g_k_a, b_forget, g_q_b, g_k_b,
           w_up_a, w_up_b, w_o, g_ffn, w_ffn_gate, w_ffn_up, w_ffn_down, g_ple, w_ple, w_ple_gate):
    batch, seq, d = x.shape
    depth = w_in.shape[0]
    xf = x.reshape(batch * seq, d)
    pos_col = positions.reshape(batch * seq, 1)
    for i in range(depth):
        xf = _layer(xf, p[i].reshape(batch * seq, -1), pos_col, batch, seq, g_attn[i], w_in[i], g_cq[i], w_uq[i],
                    w_uq_idx[i], g_kidx[i], b_kidx[i], g_q_a[i], g_k_a[i], b_forget[i], g_q_b[i], g_k_b[i],
                    w_up_a[i], w_up_b[i], w_o[i], g_ffn[i], w_ffn_gate[i], w_ffn_up[i], w_ffn_down[i], g_ple[i],
                    w_ple[i], w_ple_gate[i])
    return xf.reshape(batch, seq, d)
```

```python
import functools

import numpy as np
import jax
import jax.numpy as jnp
from jax import lax
from jax.experimental import pallas as pl
from jax.experimental.pallas import tpu as pltpu

CHUNK = 64
CHUNK_SHIFT = CHUNK.bit_length() - 1
HEAD_DIM = 128
ROPE_THETA = 10000.0
EPS = 1e-6
N_HEADS_A = 16
N_KV_A = 2
Q_RANK = 1024
N_IDX_HEADS = 32
IDX_DIM = 64
IDX_ROPE_DIM = 32
INDEX_TOPK = 256
N_HEADS_B = 16

LANES = 128
V7X_VMEM_BYTES = 64 * 1024 * 1024
KV_A_W = N_KV_A * HEAD_DIM
W_A = N_HEADS_A * HEAD_DIM
W_B = N_HEADS_B * HEAD_DIM
GROUP_A = N_HEADS_A // N_KV_A

MISC_W_LANE = IDX_DIM
MISC_F_LANE = IDX_DIM + N_IDX_HEADS

MASK_NEG = -1e30
SCORE_NEG = -3e38
SELECT_ITERS = 32
LOG2E = 1.4426950408889634
Q_SCALE = HEAD_DIM ** -0.5 * LOG2E
KV_TILE = 512
FFN_TILE = 256
VMEM_LIMIT = V7X_VMEM_BYTES - V7X_VMEM_BYTES // 8


def _cparams(sem):
    return pltpu.CompilerParams(dimension_semantics=sem, vmem_limit_bytes=VMEM_LIMIT)


def _sigmoid(x):
    return 1.0 / (1.0 + jnp.exp(-x))


def _nt_dot(a, b):
    return lax.dot_general(a, b, (((1,), (1,)), ((), ())), preferred_element_type=jnp.float32)


def _wdot(a, w_ref):
    return jnp.dot(a, w_ref[...].astype(a.dtype), preferred_element_type=jnp.float32)


def _stage_regroup_kernel(w_ref, o_ref):
    o_ref[...] = w_ref[...].T.astype(o_ref.dtype)


def _stage_regroup(w_t, segments, tn=512, tk=1024):
    n_src, k = w_t.shape
    lags, dst = [], 0
    for src, width in segments:
        assert width % tn == 0 and src % 16 == 0 and src + width <= n_src
        lags.append((dst // tn, src - dst))
        dst += width
    tk = min(tk, k)

    def src_row(j):
        lag = lags[0][1]
        for first_tile, seg_lag in lags[1:]:
            lag = jnp.where(j >= first_tile, seg_lag, lag)
        return pl.multiple_of(j * tn + lag, 16)

    return pl.pallas_call(
        _stage_regroup_kernel,
        grid=(dst // tn, k // tk),
        in_specs=[pl.BlockSpec((pl.Element(tn), pl.Element(tk)),
                               lambda j, kk: (src_row(j), pl.multiple_of(kk * tk, tk)))],
        out_specs=pl.BlockSpec((tk, tn), lambda j, kk: (kk, j)),
        out_shape=jax.ShapeDtypeStruct((k, dst), jnp.bfloat16),
        compiler_params=_cparams(("parallel", "parallel")),
        name="stage_regroup",
    )(w_t)


def _stage_misc_kernel(a_ref, b_ref, o_ref):
    a, b = a_ref[...], b_ref[...]
    pad = jnp.zeros((LANES - a.shape[0] - b.shape[0], a.shape[1]), a.dtype)
    o_ref[...] = jnp.concatenate([a, b, pad], axis=0).T.astype(o_ref.dtype)


def _stage_misc(w_t, start_a, n_a, start_b, n_b, tk=1024):
    k = w_t.shape[1]
    tk = min(tk, k)
    col = lambda kk: pl.multiple_of(kk * tk, tk)
    return pl.pallas_call(
        _stage_misc_kernel,
        grid=(k // tk,),
        in_specs=[
            pl.BlockSpec((pl.Element(n_a), pl.Element(tk)), lambda kk: (start_a, col(kk))),
            pl.BlockSpec((pl.Element(n_b), pl.Element(tk)), lambda kk: (start_b, col(kk))),
        ],
        out_specs=pl.BlockSpec((tk, LANES), lambda kk: (kk, 0)),
        out_shape=jax.ShapeDtypeStruct((k, LANES), jnp.bfloat16),
        compiler_params=_cparams(("parallel",)),
        name="stage_misc",
    )(w_t, w_t)


def _rmsnorm_kernel(x_ref, g_ref, o_ref):
    x = x_ref[...]
    ms = jnp.mean(x * x, axis=-1, keepdims=True)
    o_ref[...] = (x * lax.rsqrt(ms + EPS) * g_ref[...]).astype(o_ref.dtype)


def _rmsnorm(x, g, tm=256):
    t, d = x.shape
    tm = min(tm, t)
    return pl.pallas_call(
        _rmsnorm_kernel,
        grid=(t // tm,),
        in_specs=[pl.BlockSpec((tm, d), lambda i: (i, 0)), pl.BlockSpec((1, d), lambda i: (0, 0))],
        out_specs=pl.BlockSpec((tm, d), lambda i: (i, 0)),
        out_shape=jax.ShapeDtypeStruct((t, d), jnp.bfloat16),
        compiler_params=_cparams(("parallel",)),
        name="rmsnorm",
    )(x, g.reshape(1, d))


def _mm_kernel(a_ref, w_ref, o_ref):
    o_ref[...] = jnp.dot(a_ref[...], w_ref[...], preferred_element_type=jnp.float32).astype(o_ref.dtype)


def _matmul(a, w, out_dtype, tm=1024, tn=512, name="matmul"):
    t, k = a.shape
    n = w.shape[1]
    tm, tn = min(tm, t), min(tn, n)
    return pl.pallas_call(
        _mm_kernel,
        grid=(t // tm, n // tn),
        in_specs=[pl.BlockSpec((tm, k), lambda i, j: (i, 0)), pl.BlockSpec((k, tn), lambda i, j: (0, j))],
        out_specs=pl.BlockSpec((tm, tn), lambda i, j: (i, j)),
        out_shape=jax.ShapeDtypeStruct((t, n), out_dtype),
        compiler_params=_cparams(("parallel", "parallel")),
        name=name,
    )(a, w)


def _mm_res_norm_kernel(a_ref, w_ref, r_ref, g_ref, o_ref, xg_ref, ssq_ref):
    y = r_ref[...] + _wdot(a_ref[...], w_ref)
    o_ref[...] = y
    xg_ref[...] = (y * g_ref[...]).astype(xg_ref.dtype)
    part = jnp.broadcast_to(jnp.sum(y * y, axis=-1, keepdims=True), ssq_ref.shape)

    @pl.when(pl.program_id(1) == 0)
    def _():
        ssq_ref[...] = part

    @pl.when(pl.program_id(1) > 0)
    def _():
        ssq_ref[...] += part


def _matmul_residual_norm(a, w, r, g, tm=1024, tn=512, name="matmul_residual_norm"):
    t, k = a.shape
    n = w.shape[1]
    tm, tn = min(tm, t), min(tn, n)
    tile = pl.BlockSpec((tm, tn), lambda i, j: (i, j))
    return pl.pallas_call(
        _mm_res_norm_kernel,
        grid=(t // tm, n // tn),
        in_specs=[
            pl.BlockSpec((tm, k), lambda i, j: (i, 0)),
            pl.BlockSpec((k, tn), lambda i, j: (0, j)),
            tile,
            pl.BlockSpec((1, tn), lambda i, j: (0, j)),
        ],
        out_specs=[tile, tile, pl.BlockSpec((tm, LANES), lambda i, j: (i, 0))],
        out_shape=[
            jax.ShapeDtypeStruct((t, n), jnp.float32),
            jax.ShapeDtypeStruct((t, n), jnp.bfloat16),
            jax.ShapeDtypeStruct((t, LANES), jnp.float32),
        ],
        compiler_params=_cparams(("parallel", "arbitrary")),
        name=name,
    )(a, w, r, g.reshape(1, n))


def _row_rstd(ssq_ref, width):
    return lax.rsqrt(ssq_ref[...] * (1.0 / width) + EPS)


def _scale_rows(acc, rstd):
    return jnp.concatenate([acc[:, c * LANES:(c + 1) * LANES] * rstd for c in range(acc.shape[1] // LANES)], axis=1)


def _swiglu_kernel(a_ref, ssq_ref, wg_ref, wu_ref, o_ref):
    a = a_ref[...]
    rstd = _row_rstd(ssq_ref, a.shape[1])
    g = _scale_rows(_wdot(a, wg_ref), rstd)
    u = _scale_rows(_wdot(a, wu_ref), rstd)
    o_ref[...] = (g * _sigmoid(g) * u).astype(o_ref.dtype)


def _swiglu(xg, ssq, wg, wu, tm=1024, tn=512):
    t, k = xg.shape
    n = wg.shape[1]
    tm, tn = min(tm, t), min(tn, n)
    return pl.pallas_call(
        _swiglu_kernel,
        grid=(t // tm, n // tn),
        in_specs=[
            pl.BlockSpec((tm, k), lambda i, j: (i, 0)),
            pl.BlockSpec((tm, LANES), lambda i, j: (i, 0)),
            pl.BlockSpec((k, tn), lambda i, j: (0, j)),
            pl.BlockSpec((k, tn), lambda i, j: (0, j)),
        ],
        out_specs=pl.BlockSpec((tm, tn), lambda i, j: (i, j)),
        out_shape=jax.ShapeDtypeStruct((t, n), jnp.bfloat16),
        compiler_params=_cparams(("parallel", "parallel")),
        name="swiglu",
    )(xg, ssq, wg, wu)


def _merge_kernel(oa_ref, ob_ref, wa_ref, wb_ref, ga_ref, gb_ref, o_ref):
    a = _wdot(oa_ref[...], wa_ref)
    b = _wdot(ob_ref[...], wb_ref)
    ga = _sigmoid(ga_ref[...].astype(jnp.float32))
    gb = _sigmoid(gb_ref[...].astype(jnp.float32))
    o_ref[...] = (ga * a + gb * b).astype(o_ref.dtype)


def _merge(o_a, o_b, w_up_a, w_up_b, proj, ga_col, gb_col, tm=1024, tn=512):
    t, ka = o_a.shape
    kb = o_b.shape[1]
    n = w_up_a.shape[1]
    tm, tn = min(tm, t), min(tn, n)
    ga_blk, gb_blk = ga_col // tn, gb_col // tn
    return pl.pallas_call(
        _merge_kernel,
        grid=(t // tm, n // tn),
        in_specs=[
            pl.BlockSpec((tm, ka), lambda i, j: (i, 0)),
            pl.BlockSpec((tm, kb), lambda i, j: (i, 0)),
            pl.BlockSpec((ka, tn), lambda i, j: (0, j)),
            pl.BlockSpec((kb, tn), lambda i, j: (0, j)),
            pl.BlockSpec((tm, tn), lambda i, j: (i, ga_blk + j)),
            pl.BlockSpec((tm, tn), lambda i, j: (i, gb_blk + j)),
        ],
        out_specs=pl.BlockSpec((tm, tn), lambda i, j: (i, j)),
        out_shape=jax.ShapeDtypeStruct((t, n), jnp.bfloat16),
        compiler_params=_cparams(("parallel", "parallel")),
        name="merge",
    )(o_a, o_b, w_up_a, w_up_b, proj, proj)


def _ple_kernel(xg_ref, ssq_ref, wg_ref, p_ref, wp_ref, r_ref, o_ref):
    a = xg_ref[...]
    g = _scale_rows(_wdot(a, wg_ref), _row_rstd(ssq_ref, a.shape[1]))
    e = _wdot(p_ref[...], wp_ref)
    o_ref[...] = r_ref[...] + _sigmoid(g) * e


def _ple(xg, ssq, w_gate, p, w_ple, r, tm=1024, tn=512):
    t, k = xg.shape
    kp = p.shape[1]
    n = w_gate.shape[1]
    tm, tn = min(tm, t), min(tn, n)
    return pl.pallas_call(
        _ple_kernel,
        grid=(t // tm, n // tn),
        in_specs=[
            pl.BlockSpec((tm, k), lambda i, j: (i, 0)),
            pl.BlockSpec((tm, LANES), lambda i, j: (i, 0)),
            pl.BlockSpec((k, tn), lambda i, j: (0, j)),
            pl.BlockSpec((tm, kp), lambda i, j: (i, 0)),
            pl.BlockSpec((kp, tn), lambda i, j: (0, j)),
            pl.BlockSpec((tm, tn), lambda i, j: (i, j)),
        ],
        out_specs=pl.BlockSpec((tm, tn), lambda i, j: (i, j)),
        out_shape=jax.ShapeDtypeStruct((t, n), jnp.float32),
        compiler_params=_cparams(("parallel", "parallel")),
        name="ple",
    )(xg, ssq, w_gate, p, w_ple, r)


def _rope_consts():
    half = HEAD_DIM // 2
    inv_full = jnp.power(ROPE_THETA, -jnp.arange(half, dtype=jnp.float32) * (2.0 / HEAD_DIM))
    half_i = IDX_ROPE_DIM // 2
    inv_idx = jnp.power(ROPE_THETA, -jnp.arange(half_i, dtype=jnp.float32) * (2.0 / IDX_ROPE_DIM))
    zeros = jnp.zeros((IDX_DIM - IDX_ROPE_DIM,), jnp.float32)
    sign = np.concatenate([-np.ones(half, np.float32), np.ones(half, np.float32)])
    mask_a = np.zeros(IDX_DIM, np.float32)
    mask_a[:half_i] = -1.0
    mask_b = np.zeros(IDX_DIM, np.float32)
    mask_b[half_i:IDX_ROPE_DIM] = 1.0
    reps = LANES // IDX_DIM
    rows = [
        jnp.concatenate([inv_full, inv_full]),
        jnp.asarray(sign),
        jnp.tile(jnp.concatenate([inv_idx, inv_idx, zeros]), reps),
        jnp.asarray(np.tile(mask_a, reps)),
        jnp.asarray(np.tile(mask_b, reps)),
    ]
    rows += [jnp.zeros((LANES,), jnp.float32)] * 3
    return jnp.stack(rows)


def _rope_tables_kernel(pos_ref, c_ref, cf_ref, sf_ref, ci_ref, sa_ref, sb_ref):
    pos = pos_ref[...].astype(jnp.float32)
    ang = pos * c_ref[0:1, :]
    cf_ref[...] = jnp.cos(ang)
    sf_ref[...] = jnp.sin(ang) * c_ref[1:2, :]
    ang_i = pos * c_ref[2:3, :]
    ci_ref[...] = jnp.cos(ang_i)
    s_i = jnp.sin(ang_i)
    sa_ref[...] = s_i * c_ref[3:4, :]
    sb_ref[...] = s_i * c_ref[4:5, :]


def _rope_tables(pos_col, tm=512):
    t = pos_col.shape[0]
    tm = min(tm, t)
    tab = jax.ShapeDtypeStruct((t, LANES), jnp.float32)
    spec = pl.BlockSpec((tm, LANES), lambda i: (i, 0))
    return pl.pallas_call(
        _rope_tables_kernel,
        grid=(t // tm,),
        in_specs=[pl.BlockSpec((tm, 1), lambda i: (i, 0)), pl.BlockSpec((8, LANES), lambda i: (0, 0))],
        out_specs=[spec] * 5,
        out_shape=[tab] * 5,
        compiler_params=_cparams(("parallel",)),
        name="rope_tables",
    )(pos_col, _rope_consts())


def _rope_full(x, cf, sf):
    return x * cf + pltpu.roll(x, HEAD_DIM // 2, axis=1) * sf


def _rope_idx(x, ci, sa, sb):
    half = IDX_ROPE_DIM // 2
    return x * ci + pltpu.roll(x, LANES - half, axis=1) * sa + pltpu.roll(x, half, axis=1) * sb


def _head_rms(x, g):
    ms = jnp.mean(x * x, axis=-1, keepdims=True)
    return x * lax.rsqrt(ms + EPS) * g


def _q_proj_kernel(cq_ref, gcq_ref, wa_ref, wi_ref, gqa_ref, cf_ref, sf_ref, ci_ref, sa_ref, sb_ref, qa_ref, qi_ref):
    cq = cq_ref[...].astype(jnp.float32)
    ms = jnp.mean(cq * cq, axis=-1, keepdims=True)
    cq = (cq * lax.rsqrt(ms + EPS) * gcq_ref[...]).astype(jnp.bfloat16)
    cf, sf = cf_ref[...], sf_ref[...]
    gqa = gqa_ref[...]
    for j in range(N_HEADS_A // 2):
        acc = jnp.dot(cq, wa_ref[:, j * 2 * HEAD_DIM:(j + 1) * 2 * HEAD_DIM], preferred_element_type=jnp.float32)
        for e in range(2):
            hq = _head_rms(acc[:, e * HEAD_DIM:(e + 1) * HEAD_DIM], gqa)
            qa_ref[2 * j + e] = (_rope_full(hq, cf, sf) * Q_SCALE).astype(qa_ref.dtype)
    ci, sa, sb = ci_ref[...], sa_ref[...], sb_ref[...]
    low_half = lax.broadcasted_iota(jnp.int32, (cq.shape[0], LANES), 1) < IDX_DIM
    per_dot = 2 * LANES // IDX_DIM
    for j in range(N_IDX_HEADS // per_dot):
        acc = jnp.dot(cq, wi_ref[:, j * 2 * LANES:(j + 1) * 2 * LANES], preferred_element_type=jnp.float32)
        for c in range(2):
            pair = _rope_idx(acc[:, c * LANES:(c + 1) * LANES], ci, sa, sb)
            for e, head in enumerate((pair, pltpu.roll(pair, IDX_DIM, axis=1))):
                qi_ref[per_dot * j + 2 * c + e] = jnp.where(low_half, head, 0.0).astype(qi_ref.dtype)


def _q_proj(proj, g_cq, w_uq, w_uq_idx, g_q_a, tabs, tm=512):
    t = proj.shape[0]
    tm = min(tm, t)
    cf, sf, ci, sa, sb = tabs
    tab_spec = pl.BlockSpec((tm, LANES), lambda i: (i, 0))
    return pl.pallas_call(
        _q_proj_kernel,
        grid=(t // tm,),
        in_specs=[
            pl.BlockSpec((tm, Q_RANK), lambda i: (i, 0)),
            pl.BlockSpec((1, Q_RANK), lambda i: (0, 0)),
            pl.BlockSpec((Q_RANK, W_A), lambda i: (0, 0)),
            pl.BlockSpec((Q_RANK, N_IDX_HEADS * IDX_DIM), lambda i: (0, 0)),
            pl.BlockSpec((1, HEAD_DIM), lambda i: (0, 0)),
            tab_spec, tab_spec, tab_spec, tab_spec, tab_spec,
        ],
        out_specs=[
            pl.BlockSpec((N_HEADS_A, tm, HEAD_DIM), lambda i: (0, i, 0)),
            pl.BlockSpec((N_IDX_HEADS, tm, LANES), lambda i: (0, i, 0)),
        ],
        out_shape=[
            jax.ShapeDtypeStruct((N_HEADS_A, t, HEAD_DIM), jnp.bfloat16),
            jax.ShapeDtypeStruct((N_IDX_HEADS, t, LANES), jnp.bfloat16),
        ],
        compiler_params=_cparams(("parallel",)),
        name="q_proj",
    )(proj, g_cq.reshape(1, Q_RANK), w_uq, w_uq_idx, g_q_a.reshape(1, HEAD_DIM), cf, sf, ci, sa, sb)


def _ka_kernel(x_ref, g_ref, cf_ref, sf_ref, o_ref):
    x = x_ref[...].astype(jnp.float32)
    g, cf, sf = g_ref[...], cf_ref[...], sf_ref[...]
    for h in range(N_KV_A):
        hk = _head_rms(x[:, h * HEAD_DIM:(h + 1) * HEAD_DIM], g)
        o_ref[:, h * HEAD_DIM:(h + 1) * HEAD_DIM] = _rope_full(hk, cf, sf).astype(o_ref.dtype)


def _ka_norm_rope(proj, col, g_k_a, cf, sf, tm=512):
    t = proj.shape[0]
    tm = min(tm, t)
    tab_spec = pl.BlockSpec((tm, LANES), lambda i: (i, 0))
    return pl.pallas_call(
        _ka_kernel,
        grid=(t // tm,),
        in_specs=[
            pl.BlockSpec((tm, KV_A_W), lambda i: (i, col // KV_A_W)),
            pl.BlockSpec((1, HEAD_DIM), lambda i: (0, 0)),
            tab_spec, tab_spec,
        ],
        out_specs=pl.BlockSpec((tm, KV_A_W), lambda i: (i, 0)),
        out_shape=jax.ShapeDtypeStruct((t, KV_A_W), jnp.bfloat16),
        compiler_params=_cparams(("parallel",)),
        name="ka_norm_rope",
    )(proj, g_k_a.reshape(1, HEAD_DIM), cf, sf)


def _split3_bf16(x):
    hi = x.astype(jnp.bfloat16)
    r1 = x - hi.astype(jnp.float32)
    mid = r1.astype(jnp.bfloat16)
    lo = (r1 - mid.astype(jnp.float32)).astype(jnp.bfloat16)
    return hi, mid, lo


def _misc_kernel(m_ref, c_ref, ci_ref, sa_ref, sb_ref, tri_ref, kidx_ref, wf_ref, wt_ref, frow_ref, carry_ref, *,
                 tiles_per_seq):
    i = pl.program_id(0)

    @pl.when(i % tiles_per_seq == 0)
    def _():
        carry_ref[...] = jnp.zeros_like(carry_ref)

    x = m_ref[...]
    lane = lax.broadcasted_iota(jnp.int32, x.shape, 1)
    is_k = lane < IDX_DIM
    xk = jnp.where(is_k, x, 0.0)
    mu = jnp.sum(xk, axis=-1, keepdims=True) * (1.0 / IDX_DIM)
    dk = jnp.where(is_k, x - mu, 0.0)
    var = jnp.sum(dk * dk, axis=-1, keepdims=True) * (1.0 / IDX_DIM)
    y = dk * lax.rsqrt(var + EPS) * c_ref[0:1, :] + c_ref[1:2, :]
    y = _rope_idx(y, ci_ref[...], sa_ref[...], sb_ref[...])
    kidx_ref[...] = jnp.where(is_k, y, 0.0).astype(kidx_ref.dtype)
    f = x + c_ref[2:3, :]
    log_f = jnp.minimum(f, 0.0) - jnp.log1p(jnp.exp(-jnp.abs(f)))
    hi, mid, lo = _split3_bf16(log_f)
    tri = tri_ref[...]
    csum = (jnp.dot(tri, hi, preferred_element_type=jnp.float32)
            + jnp.dot(tri, mid, preferred_element_type=jnp.float32)
            + jnp.dot(tri, lo, preferred_element_type=jnp.float32))
    csum = csum + carry_ref[0:1, :]
    carry_ref[...] = jnp.broadcast_to(csum[-1:, :], carry_ref.shape)
    is_w = (lane >= MISC_W_LANE) & (lane < MISC_F_LANE)
    wf = jnp.where(is_w, x * c_ref[3:4, :], csum * LOG2E)
    wf_ref[...] = wf
    wf_t = wf.T
    wt_ref[...] = wf_t[MISC_W_LANE:MISC_F_LANE, :]
    frow_ref[0] = wf_t[MISC_F_LANE:MISC_F_LANE + N_HEADS_B, :]


def _misc_post(misc, g_kidx, b_kidx, b_forget, tabs, seq):
    t = misc.shape[0]
    tm = min(KV_TILE, seq)
    idx_w_scale = (N_IDX_HEADS ** -0.5) * (IDX_DIM ** -0.5)
    pad = lambda v, off: jnp.zeros((LANES,), jnp.float32).at[off:off + v.shape[0]].set(v)
    consts = jnp.stack([
        pad(g_kidx, 0), pad(b_kidx, 0), pad(b_forget, MISC_F_LANE),
        pad(jnp.full((N_IDX_HEADS,), idx_w_scale, jnp.float32), MISC_W_LANE),
    ] + [jnp.zeros((LANES,), jnp.float32)] * 4)
    tri = jnp.asarray(np.tril(np.ones((tm, tm), np.float32)), jnp.bfloat16)
    _, _, ci, sa, sb = tabs
    tab_spec = pl.BlockSpec((tm, LANES), lambda i: (i, 0))
    return pl.pallas_call(
        functools.partial(_misc_kernel, tiles_per_seq=seq // tm),
        grid=(t // tm,),
        in_specs=[
            tab_spec,
            pl.BlockSpec((8, LANES), lambda i: (0, 0)),
            tab_spec, tab_spec, tab_spec,
            pl.BlockSpec((tm, tm), lambda i: (0, 0)),
        ],
        out_specs=[
            tab_spec, tab_spec,
            pl.BlockSpec((N_IDX_HEADS, tm), lambda i: (0, i)),
            pl.BlockSpec((1, N_HEADS_B, tm), lambda i: (i, 0, 0)),
        ],
        out_shape=[
            jax.ShapeDtypeStruct((t, LANES), jnp.bfloat16),
            jax.ShapeDtypeStruct((t, LANES), jnp.float32),
            jax.ShapeDtypeStruct((N_IDX_HEADS, t), jnp.float32),
            jax.ShapeDtypeStruct((t // tm, N_HEADS_B, tm), jnp.float32),
        ],
        scratch_shapes=[pltpu.VMEM((8, LANES), jnp.float32)],
        compiler_params=_cparams(("arbitrary",)),
        name="misc_post",
    )(misc, consts, ci, sa, sb, tri)


IDX_HEAD_GROUP = 8
IDX_SLAB = 128
COUNT_ROWS = 16


def _indexer_kernel(q_ref, k_ref, w_ref, o_ref, sc_ref, *, tq, tk, n_kt, top_k):
    qi = pl.program_id(1)
    t0 = qi * tq
    n_vis = (t0 + tq + tk - 1) // tk
    t_row = lax.broadcasted_iota(jnp.int32, (1, tq), 1) + t0
    vis_end = (jnp.right_shift(t_row, CHUNK_SHIFT) + 1) * CHUNK

    def score_tile(c, carry):
        mx, mn = carry
        kt = k_ref[pl.ds(pl.multiple_of(c * tk, tk), tk), :]
        for g in range(N_IDX_HEADS // IDX_HEAD_GROUP):
            qg = q_ref[g * IDX_HEAD_GROUP:(g + 1) * IDX_HEAD_GROUP].reshape(IDX_HEAD_GROUP * tq, LANES)
            st = _nt_dot(kt, qg)
            for r in range(tk // IDX_SLAB):
                rows = slice(r * IDX_SLAB, (r + 1) * IDX_SLAB)
                part = jnp.zeros((IDX_SLAB, tq), jnp.float32)
                for j in range(IDX_HEAD_GROUP):
                    h = g * IDX_HEAD_GROUP + j
                    part = part + jnp.maximum(st[rows, j * tq:(j + 1) * tq], 0.0) * w_ref[h:h + 1, :]
                if g == 0:
                    sc_ref[c, rows, :] = part
                else:
                    sc_ref[c, rows, :] += part
        s_col = lax.broadcasted_iota(jnp.int32, (tk, 1), 0) + c * tk
        adm = s_col < vis_end
        sc = sc_ref[c]
        sc_ref[c] = jnp.where(adm, sc, SCORE_NEG)
        mx = jnp.maximum(mx, jnp.max(jnp.where(adm, sc, SCORE_NEG), axis=0, keepdims=True))
        mn = jnp.minimum(mn, jnp.min(jnp.where(adm, sc, -SCORE_NEG), axis=0, keepdims=True))
        return mx, mn

    hi0, lo0 = lax.fori_loop(
        0, n_vis, score_tile,
        (jnp.full((1, tq), SCORE_NEG, jnp.float32), jnp.full((1, tq), -SCORE_NEG, jnp.float32)))

    def unsettled(state):
        it, _, _, active = state
        return jnp.logical_and(it < SELECT_ITERS, jnp.max(active) > 0.0)

    def bisect(state):
        it, lo, hi, active = state
        mid = 0.5 * (lo + hi)

        def count_tile(c, cnt):
            ge = jnp.where(sc_ref[c] >= mid, 1.0, 0.0)
            return cnt + jnp.sum(ge.reshape(tk // COUNT_ROWS, COUNT_ROWS, tq), axis=0)

        cnt = lax.fori_loop(0, n_vis, count_tile, jnp.zeros((COUNT_ROWS, tq), jnp.float32))
        cnt = jnp.sum(cnt, axis=0, keepdims=True)
        enough = cnt >= float(top_k)
        active = jnp.where(cnt == float(top_k), 0.0, active)
        return it + 1, jnp.where(enough, mid, lo), jnp.where(enough, hi, mid), active

    active0 = jnp.where(vis_end > top_k, 1.0, 0.0)
    _, thr, _, _ = lax.while_loop(unsettled, bisect, (jnp.int32(0), lo0, hi0, active0))

    def write_tile(c, carry):
        sel = jnp.where(sc_ref[c] >= thr, 0.0, MASK_NEG)
        o_ref[0, c] = sel.T.astype(o_ref.dtype)
        return carry

    lax.fori_loop(0, n_vis, write_tile, 0)

    def fill_tile(c, carry):
        o_ref[0, c] = jnp.full((tq, tk), MASK_NEG, o_ref.dtype)
        return carry

    lax.fori_loop(n_vis, n_kt, fill_tile, 0)


def _indexer_mask(q_idx, k_idx, w_t, batch, seq, top_k, tq=256):
    tq, tk = min(tq, seq), min(KV_TILE, seq)
    n_qt, n_kt = seq // tq, seq // tk
    return pl.pallas_call(
        functools.partial(_indexer_kernel, tq=tq, tk=tk, n_kt=n_kt, top_k=top_k),
        grid=(batch, n_qt),
        in_specs=[
            pl.BlockSpec((N_IDX_HEADS, tq, LANES), lambda b, i: (0, b * n_qt + i, 0)),
            pl.BlockSpec((seq, LANES), lambda b, i: (b, 0)),
            pl.BlockSpec((N_IDX_HEADS, tq), lambda b, i: (0, b * n_qt + i)),
        ],
        out_specs=pl.BlockSpec((1, n_kt, tq, tk), lambda b, i: (b, 0, i, 0)),
        out_shape=jax.ShapeDtypeStruct((batch, n_kt, seq, tk), jnp.bfloat16),
        scratch_shapes=[pltpu.VMEM((n_kt, tk, tq), jnp.float32)],
        compiler_params=_cparams(("parallel", "parallel")),
        name="indexer_mask",
    )(q_idx, k_idx, w_t)


def _softmax_update(s, row_shift, v, m_ref, l_ref, acc_ref, e):
    rows, tk = s.shape
    chunks = [s[:, c * LANES:(c + 1) * LANES] for c in range(tk // LANES)]
    mx = functools.reduce(jnp.maximum, chunks)
    mx = jnp.broadcast_to(jnp.max(mx, axis=-1, keepdims=True), (rows, LANES))
    m_prev = m_ref[e]
    m_new = jnp.maximum(m_prev, mx + row_shift)
    alpha = jnp.exp2(m_prev - m_new)
    r = m_new - row_shift
    p = jnp.concatenate([jnp.exp2(c - r) for c in chunks], axis=1).astype(v.dtype)
    v_ones = jnp.concatenate([v, jnp.ones((tk, LANES), v.dtype)], axis=1)
    pv = jnp.dot(p, v_ones, preferred_element_type=jnp.float32)
    acc_ref[e] = alpha * acc_ref[e] + pv[:, :HEAD_DIM]
    l_ref[e] = alpha * l_ref[e] + pv[:, HEAD_DIM:]
    m_ref[e] = m_new


def _softmax_init(m_ref, l_ref, acc_ref):
    m_ref[...] = jnp.full(m_ref.shape, MASK_NEG, jnp.float32)
    l_ref[...] = jnp.zeros(l_ref.shape, jnp.float32)
    acc_ref[...] = jnp.zeros(acc_ref.shape, jnp.float32)


def _dsa_attn_kernel(q_ref, k_ref, v_ref, b_ref, o_ref, m_ref, l_ref, acc_ref, *, tq, tk):
    qi = pl.program_id(1)
    n_need = ((qi + 1) * tq + tk - 1) // tk
    rows = GROUP_A * tq
    _softmax_init(m_ref, l_ref, acc_ref)

    def kv_step(j, carry):
        off = pl.multiple_of(j * tk, tk)
        bias = b_ref[0, j].astype(jnp.float32)[None]
        for g in range(N_KV_A):
            cols = slice(g * HEAD_DIM, (g + 1) * HEAD_DIM)
            q = q_ref[g * GROUP_A:(g + 1) * GROUP_A].reshape(rows, HEAD_DIM)
            s = _nt_dot(q, k_ref[pl.ds(off, tk), cols])
            s = (s.reshape(GROUP_A, tq, tk) + bias).reshape(rows, tk)
            _softmax_update(s, 0.0, v_ref[pl.ds(off, tk), cols], m_ref, l_ref, acc_ref, g)
        return carry

    lax.fori_loop(0, n_need, kv_step, 0)

    for g in range(N_KV_A):
        out = acc_ref[g] / l_ref[g]
        for h in range(GROUP_A):
            c0 = (g * GROUP_A + h) * HEAD_DIM
            o_ref[:, c0:c0 + HEAD_DIM] = out[h * tq:(h + 1) * tq].astype(o_ref.dtype)


def _dsa_attention(q_a, k_a, proj, v_col, bias, batch, seq, tq=512):
    tk = bias.shape[-1]
    tq = min(tq, seq)
    n_qt, n_kt = seq // tq, seq // tk
    rows = GROUP_A * tq
    return pl.pallas_call(
        functools.partial(_dsa_attn_kernel, tq=tq, tk=tk),
        grid=(batch, n_qt),
        in_specs=[
            pl.BlockSpec((N_HEADS_A, tq, HEAD_DIM), lambda b, i: (0, b * n_qt + i, 0)),
            pl.BlockSpec((seq, KV_A_W), lambda b, i: (b, 0)),
            pl.BlockSpec((seq, KV_A_W), lambda b, i: (b, v_col // KV_A_W)),
            pl.BlockSpec((1, n_kt, tq, tk), lambda b, i: (b, 0, i, 0)),
        ],
        out_specs=pl.BlockSpec((tq, W_A), lambda b, i: (b * n_qt + i, 0)),
        out_shape=jax.ShapeDtypeStruct((batch * seq, W_A), jnp.bfloat16),
        scratch_shapes=[
            pltpu.VMEM((N_KV_A, rows, LANES), jnp.float32),
            pltpu.VMEM((N_KV_A, rows, LANES), jnp.float32),
            pltpu.VMEM((N_KV_A, rows, HEAD_DIM), jnp.float32),
        ],
        compiler_params=_cparams(("parallel", "parallel")),
        name="dsa_attention",
    )(q_a, k_a, proj, bias)


FOX_HEADS_PER_STEP = 4


def _fox_kernel(q_ref, k_ref, v_ref, gq_ref, gk_ref, fq_ref, fk_ref, o_ref, kn_ref, s0_ref, s1_ref, m_ref, l_ref,
                acc_ref, *, tile):
    hp, qi = pl.program_id(1), pl.program_id(2)
    n_t = kn_ref.shape[0] // tile

    @pl.when(qi == 0)
    def _():
        def norm_keys(j, carry):
            off = pl.multiple_of(j * tile, tile)
            for e in range(FOX_HEADS_PER_STEP):
                cols = slice(e * HEAD_DIM, (e + 1) * HEAD_DIM)
                k = k_ref[pl.ds(off, tile), cols].astype(jnp.float32)
                kn_ref[pl.ds(off, tile), cols] = _head_rms(k, gk_ref[...]).astype(kn_ref.dtype)
            return carry

        lax.fori_loop(0, n_t, norm_keys, 0)

    lane = lax.broadcasted_iota(jnp.int32, (tile, LANES), 1)
    heads = [hp * FOX_HEADS_PER_STEP + e for e in range(FOX_HEADS_PER_STEP)]
    fq = [jnp.broadcast_to(jnp.sum(jnp.where(lane == MISC_F_LANE + h, fq_ref[...], 0.0), axis=1, keepdims=True),
                           (tile, LANES)) for h in heads]
    qn = [_head_rms(q_ref[:, e * HEAD_DIM:(e + 1) * HEAD_DIM].astype(jnp.float32), gq_ref[...]).astype(kn_ref.dtype)
          for e in range(FOX_HEADS_PER_STEP)]
    _softmax_init(m_ref, l_ref, acc_ref)

    def logits(j, s_ref):
        off = pl.multiple_of(j * tile, tile)
        for e, h in enumerate(heads):
            cols = slice(e * HEAD_DIM, (e + 1) * HEAD_DIM)
            s_ref[e] = _nt_dot(qn[e], kn_ref[pl.ds(off, tile), cols]) - fk_ref[j, pl.ds(h, 1), :]

    def attend(j, s_ref, diagonal):
        off = pl.multiple_of(j * tile, tile)
        for e in range(FOX_HEADS_PER_STEP):
            cols = slice(e * HEAD_DIM, (e + 1) * HEAD_DIM)
            s = s_ref[e]
            if diagonal:
                t_pos = lax.broadcasted_iota(jnp.int32, (tile, tile), 0)
                s_pos = lax.broadcasted_iota(jnp.int32, (tile, tile), 1)
                s = jnp.where(s_pos <= t_pos, s, MASK_NEG)
            _softmax_update(s, fq[e], v_ref[pl.ds(off, tile), cols], m_ref, l_ref, acc_ref, e)

    def tile_pair(i, carry):
        logits(2 * i + 1, s1_ref)
        attend(2 * i, s0_ref, False)
        logits(2 * i + 2, s0_ref)
        attend(2 * i + 1, s1_ref, False)
        return carry

    logits(0, s0_ref)
    lax.fori_loop(0, qi // 2, tile_pair, 0)

    @pl.when(qi % 2 == 0)
    def _():
        attend(qi, s0_ref, True)

    @pl.when(qi % 2 == 1)
    def _():
        logits(qi, s1_ref)
        attend(qi - 1, s0_ref, False)
        attend(qi, s1_ref, True)

    for e in range(FOX_HEADS_PER_STEP):
        o_ref[:, e * HEAD_DIM:(e + 1) * HEAD_DIM] = (acc_ref[e] / l_ref[e]).astype(o_ref.dtype)


def _fox_attention(proj, q_col, g_q, g_k, wf, f_row, batch, seq):
    tile = min(KV_TILE, seq)
    n_t = seq // tile
    wb = FOX_HEADS_PER_STEP * HEAD_DIM
    q_blk, k_blk, v_blk = q_col // wb, (q_col + W_B) // wb, (q_col + 2 * W_B) // wb
    gain_spec = pl.BlockSpec((1, HEAD_DIM), lambda b, h, i: (0, 0))
    return pl.pallas_call(
        functools.partial(_fox_kernel, tile=tile),
        grid=(batch, N_HEADS_B // FOX_HEADS_PER_STEP, n_t),
        in_specs=[
            pl.BlockSpec((tile, wb), lambda b, h, i: (b * n_t + i, q_blk + h)),
            pl.BlockSpec((seq, wb), lambda b, h, i: (b, k_blk + h)),
            pl.BlockSpec((seq, wb), lambda b, h, i: (b, v_blk + h)),
            gain_spec, gain_spec,
            pl.BlockSpec((tile, LANES), lambda b, h, i: (b * n_t + i, 0)),
            pl.BlockSpec((n_t, N_HEADS_B, tile), lambda b, h, i: (b, 0, 0)),
        ],
        out_specs=pl.BlockSpec((tile, wb), lambda b, h, i: (b * n_t + i, h)),
        out_shape=jax.ShapeDtypeStruct((batch * seq, W_B), jnp.bfloat16),
        scratch_shapes=[
            pltpu.VMEM((seq, wb), jnp.bfloat16),
            pltpu.VMEM((FOX_HEADS_PER_STEP, tile, tile), jnp.float32),
            pltpu.VMEM((FOX_HEADS_PER_STEP, tile, tile), jnp.float32),
            pltpu.VMEM((FOX_HEADS_PER_STEP, tile, LANES), jnp.float32),
            pltpu.VMEM((FOX_HEADS_PER_STEP, tile, LANES), jnp.float32),
            pltpu.VMEM((FOX_HEADS_PER_STEP, tile, HEAD_DIM), jnp.float32),
        ],
        compiler_params=_cparams(("parallel", "parallel", "arbitrary")),
        name="fox_attention",
    )(proj, proj, proj, g_q.reshape(1, HEAD_DIM), g_k.reshape(1, HEAD_DIM), wf, f_row)


def _layer(x, p, pos_col, batch, seq, g_attn, w_in, g_cq, w_uq, w_uq_idx, g_kidx, b_kidx, g_q_a, g_k_a, b_forget,
           g_q_b, g_k_b, w_up_a, w_up_b, w_o, g_ffn, w_ffn_gate, w_ffn_up, w_ffn_down, g_ple, w_ple, w_ple_gate):
    d = x.shape[1]
    bf = jnp.bfloat16
    top_k = min(INDEX_TOPK, seq // 4)

    o_kidx = Q_RANK + 2 * KV_A_W
    o_qb = o_kidx + IDX_DIM + N_IDX_HEADS
    o_fb = o_qb + 3 * W_B
    o_ga = o_fb + N_HEADS_B
    w_in_t = jnp.swapaxes(w_in, 0, 1)
    w_main = _stage_regroup(w_in_t, [(0, o_kidx), (o_qb, o_fb - o_qb), (o_ga, 2 * d)])
    w_misc = _stage_misc(w_in_t, o_kidx, o_qb - o_kidx, o_fb, o_ga - o_fb)
    col_ka, col_va = Q_RANK, Q_RANK + KV_A_W
    col_qb = Q_RANK + 2 * KV_A_W
    col_vb = col_qb + 2 * W_B
    col_ga = col_vb + W_B
    col_gb = col_ga + d

    tabs = _rope_tables(pos_col)
    cf, sf = tabs[0], tabs[1]

    h = _rmsnorm(x, g_attn)
    proj = _matmul(h, w_main, bf, name="proj_main")
    misc = _matmul(h, w_misc, jnp.float32, tn=LANES, name="proj_misc")

    q_a, q_idx = _q_proj(proj, g_cq, w_uq.astype(bf), w_uq_idx.astype(bf), g_q_a, tabs)
    k_a = _ka_norm_rope(proj, col_ka, g_k_a, cf, sf)
    k_idx, wf, w_t, f_row = _misc_post(misc, g_kidx, b_kidx, b_forget, tabs, seq)
    bias = _indexer_mask(q_idx, k_idx, w_t, batch, seq, top_k)
    o_a = _dsa_attention(q_a, k_a, proj, col_va, bias, batch, seq)

    o_b = _fox_attention(proj, col_qb, g_q_b * Q_SCALE, g_k_b, wf, f_row, batch, seq)

    merged = _merge(o_a, o_b, w_up_a, w_up_b, proj, col_ga, col_gb)
    x, xg, ssq = _matmul_residual_norm(merged, w_o, x, g_ffn, name="out_proj")

    u = _swiglu(xg, ssq, w_ffn_gate, w_ffn_up, tn=FFN_TILE)
    x, xg, ssq = _matmul_residual_norm(u, w_ffn_down.astype(bf), x, g_ple, tm=512, name="ffn_down")

    return _ple(xg, ssq, w_ple_gate, p.astype(bf), w_ple, x)


def kernel(x, p, positions, g_attn, w_in, g_cq, w_uq, w_uq_idx, g_kidx, b_kidx, g_q_a, g_k_a, b_forget, g_q_b, g_k_b,
           w_up_a, w_up_b, w_o, g_ffn, w_ffn_gate, w_ffn_up, w_ffn_down, g_ple, w_ple, w_ple_gate):
    batch, seq, d = x.shape
    depth = w_in.shape[0]
    xf = x.reshape(batch * seq, d)
    pos_col = positions.reshape(batch * seq, 1)
    for i in range(depth):
        xf = _layer(xf, p[i].reshape(batch * seq, -1), pos_col, batch, seq, g_attn[i], w_in[i], g_cq[i], w_uq[i],
                    w_uq_idx[i], g_kidx[i], b_kidx[i], g_q_a[i], g_k_a[i], b_forget[i], g_q_b[i], g_k_b[i],
                    w_up_a[i], w_up_b[i], w_o[i], g_ffn[i], w_ffn_gate[i], w_ffn_up[i], w_ffn_down[i], g_ple[i],
                    w_ple[i], w_ple_gate[i])
    return xf.reshape(batch, seq, d)
```

```python
import functools

import numpy as np
import jax
import jax.numpy as jnp
from jax import lax
from jax.experimental import pallas as pl
from jax.experimental.pallas import tpu as pltpu

CHUNK = 64
CHUNK_SHIFT = CHUNK.bit_length() - 1
HEAD_DIM = 128
ROPE_THETA = 10000.0
EPS = 1e-6
N_HEADS_A = 16
N_KV_A = 2
Q_RANK = 1024
N_IDX_HEADS = 32
IDX_DIM = 64
IDX_ROPE_DIM = 32
INDEX_TOPK = 256
N_HEADS_B = 16

LANES = 128
V7X_VMEM_BYTES = 64 * 1024 * 1024
KV_A_W = N_KV_A * HEAD_DIM
W_A = N_HEADS_A * HEAD_DIM
W_B = N_HEADS_B * HEAD_DIM
GROUP_A = N_HEADS_A // N_KV_A

MISC_W_LANE = IDX_DIM
MISC_F_LANE = IDX_DIM + N_IDX_HEADS

MASK_NEG = -1e30
SCORE_NEG = -3e38
SELECT_ITERS = 32
LOG2E = 1.4426950408889634
Q_SCALE = HEAD_DIM ** -0.5 * LOG2E
KV_TILE = 512
FFN_TILE = 256
VMEM_LIMIT = V7X_VMEM_BYTES - V7X_VMEM_BYTES // 8


def _cparams(sem):
    return pltpu.CompilerParams(dimension_semantics=sem, vmem_limit_bytes=VMEM_LIMIT)


def _sigmoid(x):
    return 1.0 / (1.0 + jnp.exp(-x))


def _nt_dot(a, b):
    return lax.dot_general(a, b, (((1,), (1,)), ((), ())), preferred_element_type=jnp.float32)


def _wdot(a, w_ref):
    return jnp.dot(a, w_ref[...].astype(a.dtype), preferred_element_type=jnp.float32)


def _proj_regroup_kernel(a_ref, wt_ref, o_ref):
    a = a_ref[...]
    o_ref[...] = _nt_dot(a, wt_ref[...].astype(a.dtype)).astype(o_ref.dtype)


def _project_regrouped(a, w_t, segments, tm=1024, tn=512):
    t, k = a.shape
    lags, dst = [], 0
    for src, width in segments:
        assert width % tn == 0 and src % 16 == 0 and src + width <= w_t.shape[0]
        lags.append((dst // tn, src - dst))
        dst += width
    tm = min(tm, t)

    def src_row(j):
        lag = lags[0][1]
        for first_tile, seg_lag in lags[1:]:
            lag = jnp.where(j >= first_tile, seg_lag, lag)
        return pl.multiple_of(j * tn + lag, 16)

    return pl.pallas_call(
        _proj_regroup_kernel,
        grid=(t // tm, dst // tn),
        in_specs=[
            pl.BlockSpec((tm, k), lambda i, j: (i, 0)),
            pl.BlockSpec((pl.Element(tn), pl.Element(k)), lambda i, j: (src_row(j), 0)),
        ],
        out_specs=pl.BlockSpec((tm, tn), lambda i, j: (i, j)),
        out_shape=jax.ShapeDtypeStruct((t, dst), jnp.bfloat16),
        compiler_params=_cparams(("parallel", "parallel")),
        name="proj_main",
    )(a, w_t)


def _stage_misc_kernel(a_ref, b_ref, o_ref):
    a, b = a_ref[...], b_ref[...]
    pad = jnp.zeros((LANES - a.shape[0] - b.shape[0], a.shape[1]), a.dtype)
    o_ref[...] = jnp.concatenate([a, b, pad], axis=0).T.astype(o_ref.dtype)


def _stage_misc(w_t, start_a, n_a, start_b, n_b, tk=1024):
    k = w_t.shape[1]
    tk = min(tk, k)
    col = lambda kk: pl.multiple_of(kk * tk, tk)
    return pl.pallas_call(
        _stage_misc_kernel,
        grid=(k // tk,),
        in_specs=[
            pl.BlockSpec((pl.Element(n_a), pl.Element(tk)), lambda kk: (start_a, col(kk))),
            pl.BlockSpec((pl.Element(n_b), pl.Element(tk)), lambda kk: (start_b, col(kk))),
        ],
        out_specs=pl.BlockSpec((tk, LANES), lambda kk: (kk, 0)),
        out_shape=jax.ShapeDtypeStruct((k, LANES), jnp.bfloat16),
        compiler_params=_cparams(("parallel",)),
        name="stage_misc",
    )(w_t, w_t)


def _rmsnorm_kernel(x_ref, g_ref, o_ref):
    x = x_ref[...]
    ms = jnp.mean(x * x, axis=-1, keepdims=True)
    o_ref[...] = (x * lax.rsqrt(ms + EPS) * g_ref[...]).astype(o_ref.dtype)


def _rmsnorm(x, g, tm=256):
    t, d = x.shape
    tm = min(tm, t)
    return pl.pallas_call(
        _rmsnorm_kernel,
        grid=(t // tm,),
        in_specs=[pl.BlockSpec((tm, d), lambda i: (i, 0)), pl.BlockSpec((1, d), lambda i: (0, 0))],
        out_specs=pl.BlockSpec((tm, d), lambda i: (i, 0)),
        out_shape=jax.ShapeDtypeStruct((t, d), jnp.bfloat16),
        compiler_params=_cparams(("parallel",)),
        name="rmsnorm",
    )(x, g.reshape(1, d))


def _mm_kernel(a_ref, w_ref, o_ref):
    o_ref[...] = jnp.dot(a_ref[...], w_ref[...], preferred_element_type=jnp.float32).astype(o_ref.dtype)


def _matmul(a, w, out_dtype, tm=1024, tn=512, name="matmul"):
    t, k = a.shape
    n = w.shape[1]
    tm, tn = min(tm, t), min(tn, n)
    return pl.pallas_call(
        _mm_kernel,
        grid=(t // tm, n // tn),
        in_specs=[pl.BlockSpec((tm, k), lambda i, j: (i, 0)), pl.BlockSpec((k, tn), lambda i, j: (0, j))],
        out_specs=pl.BlockSpec((tm, tn), lambda i, j: (i, j)),
        out_shape=jax.ShapeDtypeStruct((t, n), out_dtype),
        compiler_params=_cparams(("parallel", "parallel")),
        name=name,
    )(a, w)


def _mm_res_norm_kernel(a_ref, w_ref, r_ref, g_ref, o_ref, xg_ref, ssq_ref):
    y = r_ref[...] + _wdot(a_ref[...], w_ref)
    o_ref[...] = y
    xg_ref[...] = (y * g_ref[...]).astype(xg_ref.dtype)
    part = jnp.broadcast_to(jnp.sum(y * y, axis=-1, keepdims=True), ssq_ref.shape)

    @pl.when(pl.program_id(1) == 0)
    def _():
        ssq_ref[...] = part

    @pl.when(pl.program_id(1) > 0)
    def _():
        ssq_ref[...] += part


def _matmul_residual_norm(a, w, r, g, tm=1024, tn=512, name="matmul_residual_norm"):
    t, k = a.shape
    n = w.shape[1]
    tm, tn = min(tm, t), min(tn, n)
    tile = pl.BlockSpec((tm, tn), lambda i, j: (i, j))
    return pl.pallas_call(
        _mm_res_norm_kernel,
        grid=(t // tm, n // tn),
        in_specs=[
            pl.BlockSpec((tm, k), lambda i, j: (i, 0)),
            pl.BlockSpec((k, tn), lambda i, j: (0, j)),
            tile,
            pl.BlockSpec((1, tn), lambda i, j: (0, j)),
        ],
        out_specs=[tile, tile, pl.BlockSpec((tm, LANES), lambda i, j: (i, 0))],
        out_shape=[
            jax.ShapeDtypeStruct((t, n), jnp.float32),
            jax.ShapeDtypeStruct((t, n), jnp.bfloat16),
            jax.ShapeDtypeStruct((t, LANES), jnp.float32),
        ],
        compiler_params=_cparams(("parallel", "arbitrary")),
        name=name,
    )(a, w, r, g.reshape(1, n))


def _row_rstd(ssq_ref, width):
    return lax.rsqrt(ssq_ref[...] * (1.0 / width) + EPS)


def _scale_rows(acc, rstd):
    return jnp.concatenate([acc[:, c * LANES:(c + 1) * LANES] * rstd for c in range(acc.shape[1] // LANES)], axis=1)


def _swiglu_kernel(a_ref, ssq_ref, wg_ref, wu_ref, o_ref):
    a = a_ref[...]
    rstd = _row_rstd(ssq_ref, a.shape[1])
    g = _scale_rows(_wdot(a, wg_ref), rstd)
    u = _scale_rows(_wdot(a, wu_ref), rstd)
    o_ref[...] = (g * _sigmoid(g) * u).astype(o_ref.dtype)


def _swiglu(xg, ssq, wg, wu, tm=1024, tn=512):
    t, k = xg.shape
    n = wg.shape[1]
    tm, tn = min(tm, t), min(tn, n)
    return pl.pallas_call(
        _swiglu_kernel,
        grid=(t // tm, n // tn),
        in_specs=[
            pl.BlockSpec((tm, k), lambda i, j: (i, 0)),
            pl.BlockSpec((tm, LANES), lambda i, j: (i, 0)),
            pl.BlockSpec((k, tn), lambda i, j: (0, j)),
            pl.BlockSpec((k, tn), lambda i, j: (0, j)),
        ],
        out_specs=pl.BlockSpec((tm, tn), lambda i, j: (i, j)),
        out_shape=jax.ShapeDtypeStruct((t, n), jnp.bfloat16),
        compiler_params=_cparams(("parallel", "parallel")),
        name="swiglu",
    )(xg, ssq, wg, wu)


def _merge_kernel(oa_ref, ob_ref, wa_ref, wb_ref, ga_ref, gb_ref, o_ref):
    a = _wdot(oa_ref[...], wa_ref)
    b = _wdot(ob_ref[...], wb_ref)
    ga = _sigmoid(ga_ref[...].astype(jnp.float32))
    gb = _sigmoid(gb_ref[...].astype(jnp.float32))
    o_ref[...] = (ga * a + gb * b).astype(o_ref.dtype)


def _merge(o_a, o_b, w_up_a, w_up_b, proj, ga_col, gb_col, tm=1024, tn=512):
    t, ka = o_a.shape
    kb = o_b.shape[1]
    n = w_up_a.shape[1]
    tm, tn = min(tm, t), min(tn, n)
    ga_blk, gb_blk = ga_col // tn, gb_col // tn
    return pl.pallas_call(
        _merge_kernel,
        grid=(t // tm, n // tn),
        in_specs=[
            pl.BlockSpec((tm, ka), lambda i, j: (i, 0)),
            pl.BlockSpec((tm, kb), lambda i, j: (i, 0)),
            pl.BlockSpec((ka, tn), lambda i, j: (0, j)),
            pl.BlockSpec((kb, tn), lambda i, j: (0, j)),
            pl.BlockSpec((tm, tn), lambda i, j: (i, ga_blk + j)),
            pl.BlockSpec((tm, tn), lambda i, j: (i, gb_blk + j)),
        ],
        out_specs=pl.BlockSpec((tm, tn), lambda i, j: (i, j)),
        out_shape=jax.ShapeDtypeStruct((t, n), jnp.bfloat16),
        compiler_params=_cparams(("parallel", "parallel")),
        name="merge",
    )(o_a, o_b, w_up_a, w_up_b, proj, proj)


def _ple_kernel(xg_ref, ssq_ref, wg_ref, p_ref, wp_ref, r_ref, o_ref):
    a = xg_ref[...]
    g = _scale_rows(_wdot(a, wg_ref), _row_rstd(ssq_ref, a.shape[1]))
    e = _wdot(p_ref[...], wp_ref)
    o_ref[...] = r_ref[...] + _sigmoid(g) * e


def _ple(xg, ssq, w_gate, p, w_ple, r, tm=1024, tn=512):
    t, k = xg.shape
    kp = p.shape[1]
    n = w_gate.shape[1]
    tm, tn = min(tm, t), min(tn, n)
    return pl.pallas_call(
        _ple_kernel,
        grid=(t // tm, n // tn),
        in_specs=[
            pl.BlockSpec((tm, k), lambda i, j: (i, 0)),
            pl.BlockSpec((tm, LANES), lambda i, j: (i, 0)),
            pl.BlockSpec((k, tn), lambda i, j: (0, j)),
            pl.BlockSpec((tm, kp), lambda i, j: (i, 0)),
            pl.BlockSpec((kp, tn), lambda i, j: (0, j)),
            pl.BlockSpec((tm, tn), lambda i, j: (i, j)),
        ],
        out_specs=pl.BlockSpec((tm, tn), lambda i, j: (i, j)),
        out_shape=jax.ShapeDtypeStruct((t, n), jnp.float32),
        compiler_params=_cparams(("parallel", "parallel")),
        name="ple",
    )(xg, ssq, w_gate, p, w_ple, r)


def _rope_consts():
    half = HEAD_DIM // 2
    inv_full = jnp.power(ROPE_THETA, -jnp.arange(half, dtype=jnp.float32) * (2.0 / HEAD_DIM))
    half_i = IDX_ROPE_DIM // 2
    inv_idx = jnp.power(ROPE_THETA, -jnp.arange(half_i, dtype=jnp.float32) * (2.0 / IDX_ROPE_DIM))
    zeros = jnp.zeros((IDX_DIM - IDX_ROPE_DIM,), jnp.float32)
    sign = np.concatenate([-np.ones(half, np.float32), np.ones(half, np.float32)])
    mask_a = np.zeros(IDX_DIM, np.float32)
    mask_a[:half_i] = -1.0
    mask_b = np.zeros(IDX_DIM, np.float32)
    mask_b[half_i:IDX_ROPE_DIM] = 1.0
    reps = LANES // IDX_DIM
    rows = [
        jnp.concatenate([inv_full, inv_full]),
        jnp.asarray(sign),
        jnp.tile(jnp.concatenate([inv_idx, inv_idx, zeros]), reps),
        jnp.asarray(np.tile(mask_a, reps)),
        jnp.asarray(np.tile(mask_b, reps)),
    ]
    rows += [jnp.zeros((LANES,), jnp.float32)] * 3
    return jnp.stack(rows)


def _rope_tables_kernel(pos_ref, c_ref, cf_ref, sf_ref, ci_ref, sa_ref, sb_ref):
    pos = pos_ref[...].astype(jnp.float32)
    ang = pos * c_ref[0:1, :]
    cf_ref[...] = jnp.cos(ang)
    sf_ref[...] = jnp.sin(ang) * c_ref[1:2, :]
    ang_i = pos * c_ref[2:3, :]
    ci_ref[...] = jnp.cos(ang_i)
    s_i = jnp.sin(ang_i)
    sa_ref[...] = s_i * c_ref[3:4, :]
    sb_ref[...] = s_i * c_ref[4:5, :]


def _rope_tables(pos_col, tm=512):
    t = pos_col.shape[0]
    tm = min(tm, t)
    tab = jax.ShapeDtypeStruct((t, LANES), jnp.float32)
    spec = pl.BlockSpec((tm, LANES), lambda i: (i, 0))
    return pl.pallas_call(
        _rope_tables_kernel,
        grid=(t // tm,),
        in_specs=[pl.BlockSpec((tm, 1), lambda i: (i, 0)), pl.BlockSpec((8, LANES), lambda i: (0, 0))],
        out_specs=[spec] * 5,
        out_shape=[tab] * 5,
        compiler_params=_cparams(("parallel",)),
        name="rope_tables",
    )(pos_col, _rope_consts())


def _rope_full(x, cf, sf):
    return x * cf + pltpu.roll(x, HEAD_DIM // 2, axis=1) * sf


def _rope_idx(x, ci, sa, sb):
    half = IDX_ROPE_DIM // 2
    return x * ci + pltpu.roll(x, LANES - half, axis=1) * sa + pltpu.roll(x, half, axis=1) * sb


def _head_rms(x, g):
    ms = jnp.mean(x * x, axis=-1, keepdims=True)
    return x * lax.rsqrt(ms + EPS) * g


def _q_proj_kernel(cq_ref, gcq_ref, wa_ref, wi_ref, gqa_ref, cf_ref, sf_ref, ci_ref, sa_ref, sb_ref, qa_ref, qi_ref):
    cq = cq_ref[...].astype(jnp.float32)
    ms = jnp.mean(cq * cq, axis=-1, keepdims=True)
    cq = (cq * lax.rsqrt(ms + EPS) * gcq_ref[...]).astype(jnp.bfloat16)
    cf, sf = cf_ref[...], sf_ref[...]
    gqa = gqa_ref[...]
    for j in range(N_HEADS_A // 2):
        acc = jnp.dot(cq, wa_ref[:, j * 2 * HEAD_DIM:(j + 1) * 2 * HEAD_DIM], preferred_element_type=jnp.float32)
        for e in range(2):
            hq = _head_rms(acc[:, e * HEAD_DIM:(e + 1) * HEAD_DIM], gqa)
            qa_ref[2 * j + e] = (_rope_full(hq, cf, sf) * Q_SCALE).astype(qa_ref.dtype)
    ci, sa, sb = ci_ref[...], sa_ref[...], sb_ref[...]
    low_half = lax.broadcasted_iota(jnp.int32, (cq.shape[0], LANES), 1) < IDX_DIM
    per_dot = 2 * LANES // IDX_DIM
    for j in range(N_IDX_HEADS // per_dot):
        acc = jnp.dot(cq, wi_ref[:, j * 2 * LANES:(j + 1) * 2 * LANES], preferred_element_type=jnp.float32)
        for c in range(2):
            pair = _rope_idx(acc[:, c * LANES:(c + 1) * LANES], ci, sa, sb)
            for e, head in enumerate((pair, pltpu.roll(pair, IDX_DIM, axis=1))):
                qi_ref[per_dot * j + 2 * c + e] = jnp.where(low_half, head, 0.0).astype(qi_ref.dtype)


def _q_proj(proj, g_cq, w_uq, w_uq_idx, g_q_a, tabs, tm=512):
    t = proj.shape[0]
    tm = min(tm, t)
    cf, sf, ci, sa, sb = tabs
    tab_spec = pl.BlockSpec((tm, LANES), lambda i: (i, 0))
    return pl.pallas_call(
        _q_proj_kernel,
        grid=(t // tm,),
        in_specs=[
            pl.BlockSpec((tm, Q_RANK), lambda i: (i, 0)),
            pl.BlockSpec((1, Q_RANK), lambda i: (0, 0)),
            pl.BlockSpec((Q_RANK, W_A), lambda i: (0, 0)),
            pl.BlockSpec((Q_RANK, N_IDX_HEADS * IDX_DIM), lambda i: (0, 0)),
            pl.BlockSpec((1, HEAD_DIM), lambda i: (0, 0)),
            tab_spec, tab_spec, tab_spec, tab_spec, tab_spec,
        ],
        out_specs=[
            pl.BlockSpec((N_HEADS_A, tm, HEAD_DIM), lambda i: (0, i, 0)),
            pl.BlockSpec((N_IDX_HEADS, tm, LANES), lambda i: (0, i, 0)),
        ],
        out_shape=[
            jax.ShapeDtypeStruct((N_HEADS_A, t, HEAD_DIM), jnp.bfloat16),
            jax.ShapeDtypeStruct((N_IDX_HEADS, t, LANES), jnp.bfloat16),
        ],
        compiler_params=_cparams(("parallel",)),
        name="q_proj",
    )(proj, g_cq.reshape(1, Q_RANK), w_uq, w_uq_idx, g_q_a.reshape(1, HEAD_DIM), cf, sf, ci, sa, sb)


def _ka_kernel(x_ref, g_ref, cf_ref, sf_ref, o_ref):
    x = x_ref[...].astype(jnp.float32)
    g, cf, sf = g_ref[...], cf_ref[...], sf_ref[...]
    for h in range(N_KV_A):
        hk = _head_rms(x[:, h * HEAD_DIM:(h + 1) * HEAD_DIM], g)
        o_ref[:, h * HEAD_DIM:(h + 1) * HEAD_DIM] = _rope_full(hk, cf, sf).astype(o_ref.dtype)


def _ka_norm_rope(proj, col, g_k_a, cf, sf, tm=512):
    t = proj.shape[0]
    tm = min(tm, t)
    tab_spec = pl.BlockSpec((tm, LANES), lambda i: (i, 0))
    return pl.pallas_call(
        _ka_kernel,
        grid=(t // tm,),
        in_specs=[
            pl.BlockSpec((tm, KV_A_W), lambda i: (i, col // KV_A_W)),
            pl.BlockSpec((1, HEAD_DIM), lambda i: (0, 0)),
            tab_spec, tab_spec,
        ],
        out_specs=pl.BlockSpec((tm, KV_A_W), lambda i: (i, 0)),
        out_shape=jax.ShapeDtypeStruct((t, KV_A_W), jnp.bfloat16),
        compiler_params=_cparams(("parallel",)),
        name="ka_norm_rope",
    )(proj, g_k_a.reshape(1, HEAD_DIM), cf, sf)


def _split3_bf16(x):
    hi = x.astype(jnp.bfloat16)
    r1 = x - hi.astype(jnp.float32)
    mid = r1.astype(jnp.bfloat16)
    lo = (r1 - mid.astype(jnp.float32)).astype(jnp.bfloat16)
    return hi, mid, lo


def _misc_kernel(m_ref, c_ref, ci_ref, sa_ref, sb_ref, tri_ref, kidx_ref, wf_ref, wt_ref, frow_ref, carry_ref, *,
                 tiles_per_seq):
    i = pl.program_id(0)

    @pl.when(i % tiles_per_seq == 0)
    def _():
        carry_ref[...] = jnp.zeros_like(carry_ref)

    x = m_ref[...]
    lane = lax.broadcasted_iota(jnp.int32, x.shape, 1)
    is_k = lane < IDX_DIM
    xk = jnp.where(is_k, x, 0.0)
    mu = jnp.sum(xk, axis=-1, keepdims=True) * (1.0 / IDX_DIM)
    dk = jnp.where(is_k, x - mu, 0.0)
    var = jnp.sum(dk * dk, axis=-1, keepdims=True) * (1.0 / IDX_DIM)
    y = dk * lax.rsqrt(var + EPS) * c_ref[0:1, :] + c_ref[1:2, :]
    y = _rope_idx(y, ci_ref[...], sa_ref[...], sb_ref[...])
    kidx_ref[...] = jnp.where(is_k, y, 0.0).astype(kidx_ref.dtype)
    f = x + c_ref[2:3, :]
    log_f = jnp.minimum(f, 0.0) - jnp.log1p(jnp.exp(-jnp.abs(f)))
    hi, mid, lo = _split3_bf16(log_f)
    tri = tri_ref[...]
    csum = (jnp.dot(tri, hi, preferred_element_type=jnp.float32)
            + jnp.dot(tri, mid, preferred_element_type=jnp.float32)
            + jnp.dot(tri, lo, preferred_element_type=jnp.float32))
    csum = csum + carry_ref[0:1, :]
    carry_ref[...] = jnp.broadcast_to(csum[-1:, :], carry_ref.shape)
    is_w = (lane >= MISC_W_LANE) & (lane < MISC_F_LANE)
    wf = jnp.where(is_w, x * c_ref[3:4, :], csum * LOG2E)
    wf_ref[...] = wf
    wf_t = wf.T
    wt_ref[...] = wf_t[MISC_W_LANE:MISC_F_LANE, :]
    frow_ref[0] = wf_t[MISC_F_LANE:MISC_F_LANE + N_HEADS_B, :]


def _misc_post(misc, g_kidx, b_kidx, b_forget, tabs, seq):
    t = misc.shape[0]
    tm = min(KV_TILE, seq)
    idx_w_scale = (N_IDX_HEADS ** -0.5) * (IDX_DIM ** -0.5)
    pad = lambda v, off: jnp.zeros((LANES,), jnp.float32).at[off:off + v.shape[0]].set(v)
    consts = jnp.stack([
        pad(g_kidx, 0), pad(b_kidx, 0), pad(b_forget, MISC_F_LANE),
        pad(jnp.full((N_IDX_HEADS,), idx_w_scale, jnp.float32), MISC_W_LANE),
    ] + [jnp.zeros((LANES,), jnp.float32)] * 4)
    tri = jnp.asarray(np.tril(np.ones((tm, tm), np.float32)), jnp.bfloat16)
    _, _, ci, sa, sb = tabs
    tab_spec = pl.BlockSpec((tm, LANES), lambda i: (i, 0))
    return pl.pallas_call(
        functools.partial(_misc_kernel, tiles_per_seq=seq // tm),
        grid=(t // tm,),
        in_specs=[
            tab_spec,
            pl.BlockSpec((8, LANES), lambda i: (0, 0)),
            tab_spec, tab_spec, tab_spec,
            pl.BlockSpec((tm, tm), lambda i: (0, 0)),
        ],
        out_specs=[
            tab_spec, tab_spec,
            pl.BlockSpec((N_IDX_HEADS, tm), lambda i: (0, i)),
            pl.BlockSpec((1, N_HEADS_B, tm), lambda i: (i, 0, 0)),
        ],
        out_shape=[
            jax.ShapeDtypeStruct((t, LANES), jnp.bfloat16),
            jax.ShapeDtypeStruct((t, LANES), jnp.float32),
            jax.ShapeDtypeStruct((N_IDX_HEADS, t), jnp.float32),
            jax.ShapeDtypeStruct((t // tm, N_HEADS_B, tm), jnp.float32),
        ],
        scratch_shapes=[pltpu.VMEM((8, LANES), jnp.float32)],
        compiler_params=_cparams(("arbitrary",)),
        name="misc_post",
    )(misc, consts, ci, sa, sb, tri)


IDX_HEAD_GROUP = 8
IDX_SLAB = 128
COUNT_ROWS = 16


def _indexer_kernel(q_ref, k_ref, w_ref, o_ref, sc_ref, *, tq, tk, n_kt, top_k):
    qi = pl.program_id(1)
    t0 = qi * tq
    n_vis = (t0 + tq + tk - 1) // tk
    t_row = lax.broadcasted_iota(jnp.int32, (1, tq), 1) + t0
    vis_end = (jnp.right_shift(t_row, CHUNK_SHIFT) + 1) * CHUNK

    def score_tile(c, carry):
        mx, mn = carry
        kt = k_ref[pl.ds(pl.multiple_of(c * tk, tk), tk), :]
        for g in range(N_IDX_HEADS // IDX_HEAD_GROUP):
            qg = q_ref[g * IDX_HEAD_GROUP:(g + 1) * IDX_HEAD_GROUP].reshape(IDX_HEAD_GROUP * tq, LANES)
            st = _nt_dot(kt, qg)
            for r in range(tk // IDX_SLAB):
                rows = slice(r * IDX_SLAB, (r + 1) * IDX_SLAB)
                part = jnp.zeros((IDX_SLAB, tq), jnp.float32)
                for j in range(IDX_HEAD_GROUP):
                    h = g * IDX_HEAD_GROUP + j
                    part = part + jnp.maximum(st[rows, j * tq:(j + 1) * tq], 0.0) * w_ref[h:h + 1, :]
                if g == 0:
                    sc_ref[c, rows, :] = part
                else:
                    sc_ref[c, rows, :] += part
        s_col = lax.broadcasted_iota(jnp.int32, (tk, 1), 0) + c * tk
        adm = s_col < vis_end
        sc = sc_ref[c]
        sc_ref[c] = jnp.where(adm, sc, SCORE_NEG)
        mx = jnp.maximum(mx, jnp.max(jnp.where(adm, sc, SCORE_NEG), axis=0, keepdims=True))
        mn = jnp.minimum(mn, jnp.min(jnp.where(adm, sc, -SCORE_NEG), axis=0, keepdims=True))
        return mx, mn

    hi0, lo0 = lax.fori_loop(
        0, n_vis, score_tile,
        (jnp.full((1, tq), SCORE_NEG, jnp.float32), jnp.full((1, tq), -SCORE_NEG, jnp.float32)))

    def unsettled(state):
        it, _, _, active = state
        return jnp.logical_and(it < SELECT_ITERS, jnp.max(active) > 0.0)

    def bisect(state):
        it, lo, hi, active = state
        mid = 0.5 * (lo + hi)

        def count_tile(c, cnt):
            ge = jnp.where(sc_ref[c] >= mid, 1.0, 0.0)
            return cnt + jnp.sum(ge.reshape(tk // COUNT_ROWS, COUNT_ROWS, tq), axis=0)

        cnt = lax.fori_loop(0, n_vis, count_tile, jnp.zeros((COUNT_ROWS, tq), jnp.float32))
        cnt = jnp.sum(cnt, axis=0, keepdims=True)
        enough = cnt >= float(top_k)
        active = jnp.where(cnt == float(top_k), 0.0, active)
        return it + 1, jnp.where(enough, mid, lo), jnp.where(enough, hi, mid), active

    active0 = jnp.where(vis_end > top_k, 1.0, 0.0)
    _, thr, _, _ = lax.while_loop(unsettled, bisect, (jnp.int32(0), lo0, hi0, active0))

    def write_tile(c, carry):
        sel = jnp.where(sc_ref[c] >= thr, 0.0, MASK_NEG)
        o_ref[0, c] = sel.T.astype(o_ref.dtype)
        return carry

    lax.fori_loop(0, n_vis, write_tile, 0)

    def fill_tile(c, carry):
        o_ref[0, c] = jnp.full((tq, tk), MASK_NEG, o_ref.dtype)
        return carry

    lax.fori_loop(n_vis, n_kt, fill_tile, 0)


def _indexer_mask(q_idx, k_idx, w_t, batch, seq, top_k, tq=256):
    tq, tk = min(tq, seq), min(KV_TILE, seq)
    n_qt, n_kt = seq // tq, seq // tk
    return pl.pallas_call(
        functools.partial(_indexer_kernel, tq=tq, tk=tk, n_kt=n_kt, top_k=top_k),
        grid=(batch, n_qt),
        in_specs=[
            pl.BlockSpec((N_IDX_HEADS, tq, LANES), lambda b, i: (0, b * n_qt + i, 0)),
            pl.BlockSpec((seq, LANES), lambda b, i: (b, 0)),
            pl.BlockSpec((N_IDX_HEADS, tq), lambda b, i: (0, b * n_qt + i)),
        ],
        out_specs=pl.BlockSpec((1, n_kt, tq, tk), lambda b, i: (b, 0, i, 0)),
        out_shape=jax.ShapeDtypeStruct((batch, n_kt, seq, tk), jnp.bfloat16),
        scratch_shapes=[pltpu.VMEM((n_kt, tk, tq), jnp.float32)],
        compiler_params=_cparams(("parallel", "parallel")),
        name="indexer_mask",
    )(q_idx, k_idx, w_t)


def _softmax_update(s, row_shift, v, m_ref, l_ref, acc_ref, e):
    rows, tk = s.shape
    chunks = [s[:, c * LANES:(c + 1) * LANES] for c in range(tk // LANES)]
    mx = functools.reduce(jnp.maximum, chunks)
    mx = jnp.broadcast_to(jnp.max(mx, axis=-1, keepdims=True), (rows, LANES))
    m_prev = m_ref[e]
    m_new = jnp.maximum(m_prev, mx + row_shift)
    alpha = jnp.exp2(m_prev - m_new)
    r = m_new - row_shift
    p = jnp.concatenate([jnp.exp2(c - r) for c in chunks], axis=1).astype(v.dtype)
    v_ones = jnp.concatenate([v, jnp.ones((tk, LANES), v.dtype)], axis=1)
    pv = jnp.dot(p, v_ones, preferred_element_type=jnp.float32)
    acc_ref[e] = alpha * acc_ref[e] + pv[:, :HEAD_DIM]
    l_ref[e] = alpha * l_ref[e] + pv[:, HEAD_DIM:]
    m_ref[e] = m_new


def _softmax_init(m_ref, l_ref, acc_ref):
    m_ref[...] = jnp.full(m_ref.shape, MASK_NEG, jnp.float32)
    l_ref[...] = jnp.zeros(l_ref.shape, jnp.float32)
    acc_ref[...] = jnp.zeros(acc_ref.shape, jnp.float32)


def _dsa_attn_kernel(q_ref, k_ref, v_ref, b_ref, o_ref, m_ref, l_ref, acc_ref, *, tq, tk):
    qi = pl.program_id(1)
    n_need = ((qi + 1) * tq + tk - 1) // tk
    rows = GROUP_A * tq
    _softmax_init(m_ref, l_ref, acc_ref)

    def kv_step(j, carry):
        off = pl.multiple_of(j * tk, tk)
        bias = b_ref[0, j].astype(jnp.float32)[None]
        for g in range(N_KV_A):
            cols = slice(g * HEAD_DIM, (g + 1) * HEAD_DIM)
            q = q_ref[g * GROUP_A:(g + 1) * GROUP_A].reshape(rows, HEAD_DIM)
            s = _nt_dot(q, k_ref[pl.ds(off, tk), cols])
            s = (s.reshape(GROUP_A, tq, tk) + bias).reshape(rows, tk)
            _softmax_update(s, 0.0, v_ref[pl.ds(off, tk), cols], m_ref, l_ref, acc_ref, g)
        return carry

    lax.fori_loop(0, n_need, kv_step, 0)

    for g in range(N_KV_A):
        out = acc_ref[g] / l_ref[g]
        for h in range(GROUP_A):
            c0 = (g * GROUP_A + h) * HEAD_DIM
            o_ref[:, c0:c0 + HEAD_DIM] = out[h * tq:(h + 1) * tq].astype(o_ref.dtype)


def _dsa_attention(q_a, k_a, proj, v_col, bias, batch, seq, tq=512):
    tk = bias.shape[-1]
    tq = min(tq, seq)
    n_qt, n_kt = seq // tq, seq // tk
    rows = GROUP_A * tq
    return pl.pallas_call(
        functools.partial(_dsa_attn_kernel, tq=tq, tk=tk),
        grid=(batch, n_qt),
        in_specs=[
            pl.BlockSpec((N_HEADS_A, tq, HEAD_DIM), lambda b, i: (0, b * n_qt + i, 0)),
            pl.BlockSpec((seq, KV_A_W), lambda b, i: (b, 0)),
            pl.BlockSpec((seq, KV_A_W), lambda b, i: (b, v_col // KV_A_W)),
            pl.BlockSpec((1, n_kt, tq, tk), lambda b, i: (b, 0, i, 0)),
        ],
        out_specs=pl.BlockSpec((tq, W_A), lambda b, i: (b * n_qt + i, 0)),
        out_shape=jax.ShapeDtypeStruct((batch * seq, W_A), jnp.bfloat16),
        scratch_shapes=[
            pltpu.VMEM((N_KV_A, rows, LANES), jnp.float32),
            pltpu.VMEM((N_KV_A, rows, LANES), jnp.float32),
            pltpu.VMEM((N_KV_A, rows, HEAD_DIM), jnp.float32),
        ],
        compiler_params=_cparams(("parallel", "parallel")),
        name="dsa_attention",
    )(q_a, k_a, proj, bias)


FOX_HEADS_PER_STEP = 4


def _fox_kernel(q_ref, k_ref, v_ref, gq_ref, gk_ref, fq_ref, fk_ref, o_ref, kn_ref, s0_ref, s1_ref, m_ref, l_ref,
                acc_ref, *, tile):
    hp, qi = pl.program_id(1), pl.program_id(2)
    n_t = kn_ref.shape[0] // tile

    @pl.when(qi == 0)
    def _():
        def norm_keys(j, carry):
            off = pl.multiple_of(j * tile, tile)
            for e in range(FOX_HEADS_PER_STEP):
                cols = slice(e * HEAD_DIM, (e + 1) * HEAD_DIM)
                k = k_ref[pl.ds(off, tile), cols].astype(jnp.float32)
                kn_ref[pl.ds(off, tile), cols] = _head_rms(k, gk_ref[...]).astype(kn_ref.dtype)
            return carry

        lax.fori_loop(0, n_t, norm_keys, 0)

    lane = lax.broadcasted_iota(jnp.int32, (tile, LANES), 1)
    heads = [hp * FOX_HEADS_PER_STEP + e for e in range(FOX_HEADS_PER_STEP)]
    fq = [jnp.broadcast_to(jnp.sum(jnp.where(lane == MISC_F_LANE + h, fq_ref[...], 0.0), axis=1, keepdims=True),
                           (tile, LANES)) for h in heads]
    qn = [_head_rms(q_ref[:, e * HEAD_DIM:(e + 1) * HEAD_DIM].astype(jnp.float32), gq_ref[...]).astype(kn_ref.dtype)
          for e in range(FOX_HEADS_PER_STEP)]
    _softmax_init(m_ref, l_ref, acc_ref)

    def logits(j, s_ref):
        off = pl.multiple_of(j * tile, tile)
        for e, h in enumerate(heads):
            cols = slice(e * HEAD_DIM, (e + 1) * HEAD_DIM)
            s_ref[e] = _nt_dot(qn[e], kn_ref[pl.ds(off, tile), cols]) - fk_ref[j, pl.ds(h, 1), :]

    def attend(j, s_ref, diagonal):
        off = pl.multiple_of(j * tile, tile)
        for e in range(FOX_HEADS_PER_STEP):
            cols = slice(e * HEAD_DIM, (e + 1) * HEAD_DIM)
            s = s_ref[e]
            if diagonal:
                t_pos = lax.broadcasted_iota(jnp.int32, (tile, tile), 0)
                s_pos = lax.broadcasted_iota(jnp.int32, (tile, tile), 1)
                s = jnp.where(s_pos <= t_pos, s, MASK_NEG)
            _softmax_update(s, fq[e], v_ref[pl.ds(off, tile), cols], m_ref, l_ref, acc_ref, e)

    def tile_pair(i, carry):
        logits(2 * i + 1, s1_ref)
        attend(2 * i, s0_ref, False)
        logits(2 * i + 2, s0_ref)
        attend(2 * i + 1, s1_ref, False)
        return carry

    logits(0, s0_ref)
    lax.fori_loop(0, qi // 2, tile_pair, 0)

    @pl.when(qi % 2 == 0)
    def _():
        attend(qi, s0_ref, True)

    @pl.when(qi % 2 == 1)
    def _():
        logits(qi, s1_ref)
        attend(qi - 1, s0_ref, False)
        attend(qi, s1_ref, True)

    for e in range(FOX_HEADS_PER_STEP):
        o_ref[:, e * HEAD_DIM:(e + 1) * HEAD_DIM] = (acc_ref[e] / l_ref[e]).astype(o_ref.dtype)


def _fox_attention(proj, q_col, g_q, g_k, wf, f_row, batch, seq):
    tile = min(KV_TILE, seq)
    n_t = seq // tile
    wb = FOX_HEADS_PER_STEP * HEAD_DIM
    q_blk, k_blk, v_blk = q_col // wb, (q_col + W_B) // wb, (q_col + 2 * W_B) // wb
    gain_spec = pl.BlockSpec((1, HEAD_DIM), lambda b, h, i: (0, 0))
    return pl.pallas_call(
        functools.partial(_fox_kernel, tile=tile),
        grid=(batch, N_HEADS_B // FOX_HEADS_PER_STEP, n_t),
        in_specs=[
            pl.BlockSpec((tile, wb), lambda b, h, i: (b * n_t + i, q_blk + h)),
            pl.BlockSpec((seq, wb), lambda b, h, i: (b, k_blk + h)),
            pl.BlockSpec((seq, wb), lambda b, h, i: (b, v_blk + h)),
            gain_spec, gain_spec,
            pl.BlockSpec((tile, LANES), lambda b, h, i: (b * n_t + i, 0)),
            pl.BlockSpec((n_t, N_HEADS_B, tile), lambda b, h, i: (b, 0, 0)),
        ],
        out_specs=pl.BlockSpec((tile, wb), lambda b, h, i: (b * n_t + i, h)),
        out_shape=jax.ShapeDtypeStruct((batch * seq, W_B), jnp.bfloat16),
        scratch_shapes=[
            pltpu.VMEM((seq, wb), jnp.bfloat16),
            pltpu.VMEM((FOX_HEADS_PER_STEP, tile, tile), jnp.float32),
            pltpu.VMEM((FOX_HEADS_PER_STEP, tile, tile), jnp.float32),
            pltpu.VMEM((FOX_HEADS_PER_STEP, tile, LANES), jnp.float32),
            pltpu.VMEM((FOX_HEADS_PER_STEP, tile, LANES), jnp.float32),
            pltpu.VMEM((FOX_HEADS_PER_STEP, tile, HEAD_DIM), jnp.float32),
        ],
        compiler_params=_cparams(("parallel", "parallel", "arbitrary")),
        name="fox_attention",
    )(proj, proj, proj, g_q.reshape(1, HEAD_DIM), g_k.reshape(1, HEAD_DIM), wf, f_row)


def _layer(x, p, pos_col, batch, seq, g_attn, w_in, g_cq, w_uq, w_uq_idx, g_kidx, b_kidx, g_q_a, g_k_a, b_forget,
           g_q_b, g_k_b, w_up_a, w_up_b, w_o, g_ffn, w_ffn_gate, w_ffn_up, w_ffn_down, g_ple, w_ple, w_ple_gate):
    d = x.shape[1]
    bf = jnp.bfloat16
    top_k = min(INDEX_TOPK, seq // 4)

    o_kidx = Q_RANK + 2 * KV_A_W
    o_qb = o_kidx + IDX_DIM + N_IDX_HEADS
    o_fb = o_qb + 3 * W_B
    o_ga = o_fb + N_HEADS_B
    w_in_t = jnp.swapaxes(w_in, 0, 1)
    w_misc = _stage_misc(w_in_t, o_kidx, o_qb - o_kidx, o_fb, o_ga - o_fb)
    col_ka, col_va = Q_RANK, Q_RANK + KV_A_W
    col_qb = Q_RANK + 2 * KV_A_W
    col_vb = col_qb + 2 * W_B
    col_ga = col_vb + W_B
    col_gb = col_ga + d

    tabs = _rope_tables(pos_col)
    cf, sf = tabs[0], tabs[1]

    h = _rmsnorm(x, g_attn)
    proj = _project_regrouped(h, w_in_t, [(0, o_kidx), (o_qb, o_fb - o_qb), (o_ga, 2 * d)])
    misc = _matmul(h, w_misc, jnp.float32, tn=LANES, name="proj_misc")

    q_a, q_idx = _q_proj(proj, g_cq, w_uq.astype(bf), w_uq_idx.astype(bf), g_q_a, tabs)
    k_a = _ka_norm_rope(proj, col_ka, g_k_a, cf, sf)
    k_idx, wf, w_t, f_row = _misc_post(misc, g_kidx, b_kidx, b_forget, tabs, seq)
    bias = _indexer_mask(q_idx, k_idx, w_t, batch, seq, top_k)
    o_a = _dsa_attention(q_a, k_a, proj, col_va, bias, batch, seq)

    o_b = _fox_attention(proj, col_qb, g_q_b * Q_SCALE, g_k_b, wf, f_row, batch, seq)

    merged = _merge(o_a, o_b, w_up_a, w_up_b, proj, col_ga, col_gb)
    x, xg, ssq = _matmul_residual_norm(merged, w_o, x, g_ffn, name="out_proj")

    u = _swiglu(xg, ssq, w_ffn_gate, w_ffn_up, tn=FFN_TILE)
    x, xg, ssq = _matmul_residual_norm(u, w_ffn_down.astype(bf), x, g_ple, tm=512, name="ffn_down")

    return _ple(xg, ssq, w_ple_gate, p.astype(bf), w_ple, x)


def kernel(x, p, positions, g_attn, w_in, g_cq, w_uq, w_uq_idx, g_kidx, b_kidx, g_q_a, g_k_a, b_forget, g_q_b, g_k_b,
           w_up_a, w_up_b, w_o, g_ffn, w_ffn_gate, w_ffn_up, w_ffn_down, g_ple, w_ple, w_ple_gate):
    batch, seq, d = x.shape
    depth = w_in.shape[0]
    xf = x.reshape(batch * seq, d)
    pos_col = positions.reshape(batch * seq, 1)
    for i in range(depth):
        xf = _layer(xf, p[i].reshape(batch * seq, -1), pos_col, batch, seq, g_attn[i], w_in[i], g_cq[i], w_uq[i],
                    w_uq_idx[i], g_kidx[i], b_kidx[i], g_q_a[i], g_k_a[i], b_forget[i], g_q_b[i], g_k_b[i],
                    w_up_a[i], w_up_b[i], w_o[i], g_ffn[i], w_ffn_gate[i], w_ffn_up[i], w_ffn_down[i], g_ple[i],
                    w_ple[i], w_ple_gate[i])
    return xf.reshape(batch, seq, d)
```

```python
import functools

import numpy as np
import jax
import jax.numpy as jnp
from jax import lax
from jax.experimental import pallas as pl
from jax.experimental.pallas import tpu as pltpu

CHUNK = 64
CHUNK_SHIFT = CHUNK.bit_length() - 1
HEAD_DIM = 128
ROPE_THETA = 10000.0
EPS = 1e-6
N_HEADS_A = 16
N_KV_A = 2
Q_RANK = 1024
N_IDX_HEADS = 32
IDX_DIM = 64
IDX_ROPE_DIM = 32
INDEX_TOPK = 256
N_HEADS_B = 16

LANES = 128
V7X_VMEM_BYTES = 64 * 1024 * 1024
KV_A_W = N_KV_A * HEAD_DIM
W_A = N_HEADS_A * HEAD_DIM
W_B = N_HEADS_B * HEAD_DIM
GROUP_A = N_HEADS_A // N_KV_A

MISC_W_LANE = IDX_DIM
MISC_F_LANE = IDX_DIM + N_IDX_HEADS

MASK_NEG = -1e30
SCORE_NEG = -3e38
SELECT_ITERS = 32
LOG2E = 1.4426950408889634
Q_SCALE = HEAD_DIM ** -0.5 * LOG2E
KV_TILE = 512
FFN_TILE = 256
VMEM_LIMIT = V7X_VMEM_BYTES - V7X_VMEM_BYTES // 8


def _cparams(sem):
    return pltpu.CompilerParams(dimension_semantics=sem, vmem_limit_bytes=VMEM_LIMIT)


def _sigmoid(x):
    return 1.0 / (1.0 + jnp.exp(-x))


def _nt_dot(a, b):
    return lax.dot_general(a, b, (((1,), (1,)), ((), ())), preferred_element_type=jnp.float32)


def _wdot(a, w_ref):
    return jnp.dot(a, w_ref[...].astype(a.dtype), preferred_element_type=jnp.float32)


def _proj_regroup_kernel(a_ref, wt_ref, o_ref):
    a = a_ref[...]
    o_ref[...] = _nt_dot(a, wt_ref[...].astype(a.dtype)).astype(o_ref.dtype)


def _project_regrouped(a, w_t, segments, tm=1024, tn=512):
    t, k = a.shape
    lags, dst = [], 0
    for src, width in segments:
        assert width % tn == 0 and src % 16 == 0 and src + width <= w_t.shape[0]
        lags.append((dst // tn, src - dst))
        dst += width
    tm = min(tm, t)

    def src_row(j):
        lag = lags[0][1]
        for first_tile, seg_lag in lags[1:]:
            lag = jnp.where(j >= first_tile, seg_lag, lag)
        return pl.multiple_of(j * tn + lag, 16)

    return pl.pallas_call(
        _proj_regroup_kernel,
        grid=(t // tm, dst // tn),
        in_specs=[
            pl.BlockSpec((tm, k), lambda i, j: (i, 0)),
            pl.BlockSpec((pl.Element(tn), pl.Element(k)), lambda i, j: (src_row(j), 0)),
        ],
        out_specs=pl.BlockSpec((tm, tn), lambda i, j: (i, j)),
        out_shape=jax.ShapeDtypeStruct((t, dst), jnp.bfloat16),
        compiler_params=_cparams(("parallel", "parallel")),
        name="proj_main",
    )(a, w_t)


def _stage_misc_kernel(a_ref, b_ref, o_ref):
    a, b = a_ref[...], b_ref[...]
    pad = jnp.zeros((LANES - a.shape[0] - b.shape[0], a.shape[1]), a.dtype)
    o_ref[...] = jnp.concatenate([a, b, pad], axis=0).T.astype(o_ref.dtype)


def _stage_misc(w_t, start_a, n_a, start_b, n_b, tk=1024):
    k = w_t.shape[1]
    tk = min(tk, k)
    col = lambda kk: pl.multiple_of(kk * tk, tk)
    return pl.pallas_call(
        _stage_misc_kernel,
        grid=(k // tk,),
        in_specs=[
            pl.BlockSpec((pl.Element(n_a), pl.Element(tk)), lambda kk: (start_a, col(kk))),
            pl.BlockSpec((pl.Element(n_b), pl.Element(tk)), lambda kk: (start_b, col(kk))),
        ],
        out_specs=pl.BlockSpec((tk, LANES), lambda kk: (kk, 0)),
        out_shape=jax.ShapeDtypeStruct((k, LANES), jnp.bfloat16),
        compiler_params=_cparams(("parallel",)),
        name="stage_misc",
    )(w_t, w_t)


def _rmsnorm_kernel(x_ref, g_ref, o_ref):
    x = x_ref[...]
    ms = jnp.mean(x * x, axis=-1, keepdims=True)
    o_ref[...] = (x * lax.rsqrt(ms + EPS) * g_ref[...]).astype(o_ref.dtype)


def _rmsnorm(x, g, tm=256):
    t, d = x.shape
    tm = min(tm, t)
    return pl.pallas_call(
        _rmsnorm_kernel,
        grid=(t // tm,),
        in_specs=[pl.BlockSpec((tm, d), lambda i: (i, 0)), pl.BlockSpec((1, d), lambda i: (0, 0))],
        out_specs=pl.BlockSpec((tm, d), lambda i: (i, 0)),
        out_shape=jax.ShapeDtypeStruct((t, d), jnp.bfloat16),
        compiler_params=_cparams(("parallel",)),
        name="rmsnorm",
    )(x, g.reshape(1, d))


def _mm_kernel(a_ref, w_ref, o_ref):
    o_ref[...] = jnp.dot(a_ref[...], w_ref[...], preferred_element_type=jnp.float32).astype(o_ref.dtype)


def _matmul(a, w, out_dtype, tm=1024, tn=512, name="matmul"):
    t, k = a.shape
    n = w.shape[1]
    tm, tn = min(tm, t), min(tn, n)
    return pl.pallas_call(
        _mm_kernel,
        grid=(t // tm, n // tn),
        in_specs=[pl.BlockSpec((tm, k), lambda i, j: (i, 0)), pl.BlockSpec((k, tn), lambda i, j: (0, j))],
        out_specs=pl.BlockSpec((tm, tn), lambda i, j: (i, j)),
        out_shape=jax.ShapeDtypeStruct((t, n), out_dtype),
        compiler_params=_cparams(("parallel", "parallel")),
        name=name,
    )(a, w)


def _mm_res_norm_kernel(a_ref, w_ref, r_ref, g_ref, o_ref, xg_ref, ssq_ref):
    y = r_ref[...] + _wdot(a_ref[...], w_ref)
    o_ref[...] = y
    xg_ref[...] = (y * g_ref[...]).astype(xg_ref.dtype)
    part = jnp.broadcast_to(jnp.sum(y * y, axis=-1, keepdims=True), ssq_ref.shape)

    @pl.when(pl.program_id(1) == 0)
    def _():
        ssq_ref[...] = part

    @pl.when(pl.program_id(1) > 0)
    def _():
        ssq_ref[...] += part


def _matmul_residual_norm(a, w, r, g, tm=1024, tn=512, name="matmul_residual_norm"):
    t, k = a.shape
    n = w.shape[1]
    tm, tn = min(tm, t), min(tn, n)
    tile = pl.BlockSpec((tm, tn), lambda i, j: (i, j))
    return pl.pallas_call(
        _mm_res_norm_kernel,
        grid=(t // tm, n // tn),
        in_specs=[
            pl.BlockSpec((tm, k), lambda i, j: (i, 0)),
            pl.BlockSpec((k, tn), lambda i, j: (0, j)),
            tile,
            pl.BlockSpec((1, tn), lambda i, j: (0, j)),
        ],
        out_specs=[tile, tile, pl.BlockSpec((tm, LANES), lambda i, j: (i, 0))],
        out_shape=[
            jax.ShapeDtypeStruct((t, n), jnp.float32),
            jax.ShapeDtypeStruct((t, n), jnp.bfloat16),
            jax.ShapeDtypeStruct((t, LANES), jnp.float32),
        ],
        compiler_params=_cparams(("parallel", "arbitrary")),
        name=name,
    )(a, w, r, g.reshape(1, n))


def _row_rstd(ssq_ref, width):
    return lax.rsqrt(ssq_ref[...] * (1.0 / width) + EPS)


def _scale_rows(acc, rstd):
    return jnp.concatenate([acc[:, c * LANES:(c + 1) * LANES] * rstd for c in range(acc.shape[1] // LANES)], axis=1)


def _swiglu_kernel(a_ref, ssq_ref, wg_ref, wu_ref, o_ref):
    a = a_ref[...]
    rstd = _row_rstd(ssq_ref, a.shape[1])
    g = _scale_rows(_wdot(a, wg_ref), rstd)
    u = _scale_rows(_wdot(a, wu_ref), rstd)
    o_ref[...] = (g * _sigmoid(g) * u).astype(o_ref.dtype)


def _swiglu(xg, ssq, wg, wu, tm=1024, tn=512):
    t, k = xg.shape
    n = wg.shape[1]
    tm, tn = min(tm, t), min(tn, n)
    return pl.pallas_call(
        _swiglu_kernel,
        grid=(t // tm, n // tn),
        in_specs=[
            pl.BlockSpec((tm, k), lambda i, j: (i, 0)),
            pl.BlockSpec((tm, LANES), lambda i, j: (i, 0)),
            pl.BlockSpec((k, tn), lambda i, j: (0, j)),
            pl.BlockSpec((k, tn), lambda i, j: (0, j)),
        ],
        out_specs=pl.BlockSpec((tm, tn), lambda i, j: (i, j)),
        out_shape=jax.ShapeDtypeStruct((t, n), jnp.bfloat16),
        compiler_params=_cparams(("parallel", "parallel")),
        name="swiglu",
    )(xg, ssq, wg, wu)


def _merge_kernel(oa_ref, ob_ref, wa_ref, wb_ref, ga_ref, gb_ref, o_ref):
    a = _wdot(oa_ref[...], wa_ref)
    b = _wdot(ob_ref[...], wb_ref)
    ga = _sigmoid(ga_ref[...].astype(jnp.float32))
    gb = _sigmoid(gb_ref[...].astype(jnp.float32))
    o_ref[...] = (ga * a + gb * b).astype(o_ref.dtype)


def _merge(o_a, o_b, w_up_a, w_up_b, proj, ga_col, gb_col, tm=1024, tn=512):
    t, ka = o_a.shape
    kb = o_b.shape[1]
    n = w_up_a.shape[1]
    tm, tn = min(tm, t), min(tn, n)
    ga_blk, gb_blk = ga_col // tn, gb_col // tn
    return pl.pallas_call(
        _merge_kernel,
        grid=(t // tm, n // tn),
        in_specs=[
            pl.BlockSpec((tm, ka), lambda i, j: (i, 0)),
            pl.BlockSpec((tm, kb), lambda i, j: (i, 0)),
            pl.BlockSpec((ka, tn), lambda i, j: (0, j)),
            pl.BlockSpec((kb, tn), lambda i, j: (0, j)),
            pl.BlockSpec((tm, tn), lambda i, j: (i, ga_blk + j)),
            pl.BlockSpec((tm, tn), lambda i, j: (i, gb_blk + j)),
        ],
        out_specs=pl.BlockSpec((tm, tn), lambda i, j: (i, j)),
        out_shape=jax.ShapeDtypeStruct((t, n), jnp.bfloat16),
        compiler_params=_cparams(("parallel", "parallel")),
        name="merge",
    )(o_a, o_b, w_up_a, w_up_b, proj, proj)


def _ple_kernel(xg_ref, ssq_ref, wg_ref, p_ref, wp_ref, r_ref, o_ref):
    a = xg_ref[...]
    g = _scale_rows(_wdot(a, wg_ref), _row_rstd(ssq_ref, a.shape[1]))
    e = _wdot(p_ref[...], wp_ref)
    o_ref[...] = r_ref[...] + _sigmoid(g) * e


def _ple(xg, ssq, w_gate, p, w_ple, r, tm=1024, tn=512):
    t, k = xg.shape
    kp = p.shape[1]
    n = w_gate.shape[1]
    tm, tn = min(tm, t), min(tn, n)
    return pl.pallas_call(
        _ple_kernel,
        grid=(t // tm, n // tn),
        in_specs=[
            pl.BlockSpec((tm, k), lambda i, j: (i, 0)),
            pl.BlockSpec((tm, LANES), lambda i, j: (i, 0)),
            pl.BlockSpec((k, tn), lambda i, j: (0, j)),
            pl.BlockSpec((tm, kp), lambda i, j: (i, 0)),
            pl.BlockSpec((kp, tn), lambda i, j: (0, j)),
            pl.BlockSpec((tm, tn), lambda i, j: (i, j)),
        ],
        out_specs=pl.BlockSpec((tm, tn), lambda i, j: (i, j)),
        out_shape=jax.ShapeDtypeStruct((t, n), jnp.float32),
        compiler_params=_cparams(("parallel", "parallel")),
        name="ple",
    )(xg, ssq, w_gate, p, w_ple, r)


def _rope_consts():
    half = HEAD_DIM // 2
    inv_full = jnp.power(ROPE_THETA, -jnp.arange(half, dtype=jnp.float32) * (2.0 / HEAD_DIM))
    half_i = IDX_ROPE_DIM // 2
    inv_idx = jnp.power(ROPE_THETA, -jnp.arange(half_i, dtype=jnp.float32) * (2.0 / IDX_ROPE_DIM))
    zeros = jnp.zeros((IDX_DIM - IDX_ROPE_DIM,), jnp.float32)
    sign = np.concatenate([-np.ones(half, np.float32), np.ones(half, np.float32)])
    mask_a = np.zeros(IDX_DIM, np.float32)
    mask_a[:half_i] = -1.0
    mask_b = np.zeros(IDX_DIM, np.float32)
    mask_b[half_i:IDX_ROPE_DIM] = 1.0
    reps = LANES // IDX_DIM
    rows = [
        jnp.concatenate([inv_full, inv_full]),
        jnp.asarray(sign),
        jnp.tile(jnp.concatenate([inv_idx, inv_idx, zeros]), reps),
        jnp.asarray(np.tile(mask_a, reps)),
        jnp.asarray(np.tile(mask_b, reps)),
    ]
    rows += [jnp.zeros((LANES,), jnp.float32)] * 3
    return jnp.stack(rows)


def _rope_tables_kernel(pos_ref, c_ref, cf_ref, sf_ref, ci_ref, sa_ref, sb_ref):
    pos = pos_ref[...].astype(jnp.float32)
    ang = pos * c_ref[0:1, :]
    cf_ref[...] = jnp.cos(ang)
    sf_ref[...] = jnp.sin(ang) * c_ref[1:2, :]
    ang_i = pos * c_ref[2:3, :]
    ci_ref[...] = jnp.cos(ang_i)
    s_i = jnp.sin(ang_i)
    sa_ref[...] = s_i * c_ref[3:4, :]
    sb_ref[...] = s_i * c_ref[4:5, :]


def _rope_tables(pos_col, tm=512):
    t = pos_col.shape[0]
    tm = min(tm, t)
    tab = jax.ShapeDtypeStruct((t, LANES), jnp.float32)
    spec = pl.BlockSpec((tm, LANES), lambda i: (i, 0))
    return pl.pallas_call(
        _rope_tables_kernel,
        grid=(t // tm,),
        in_specs=[pl.BlockSpec((tm, 1), lambda i: (i, 0)), pl.BlockSpec((8, LANES), lambda i: (0, 0))],
        out_specs=[spec] * 5,
        out_shape=[tab] * 5,
        compiler_params=_cparams(("parallel",)),
        name="rope_tables",
    )(pos_col, _rope_consts())


def _rope_full(x, cf, sf):
    return x * cf + pltpu.roll(x, HEAD_DIM // 2, axis=1) * sf


def _rope_idx(x, ci, sa, sb):
    half = IDX_ROPE_DIM // 2
    return x * ci + pltpu.roll(x, LANES - half, axis=1) * sa + pltpu.roll(x, half, axis=1) * sb


def _head_rms(x, g):
    ms = jnp.mean(x * x, axis=-1, keepdims=True)
    return x * lax.rsqrt(ms + EPS) * g


def _q_proj_kernel(cq_ref, gcq_ref, wa_ref, wi_ref, gqa_ref, cf_ref, sf_ref, ci_ref, sa_ref, sb_ref, qa_ref, qi_ref):
    cq = cq_ref[...].astype(jnp.float32)
    ms = jnp.mean(cq * cq, axis=-1, keepdims=True)
    cq = (cq * lax.rsqrt(ms + EPS) * gcq_ref[...]).astype(jnp.bfloat16)
    cf, sf = cf_ref[...], sf_ref[...]
    gqa = gqa_ref[...]
    for j in range(N_HEADS_A // 2):
        acc = jnp.dot(cq, wa_ref[:, j * 2 * HEAD_DIM:(j + 1) * 2 * HEAD_DIM], preferred_element_type=jnp.float32)
        for e in range(2):
            hq = _head_rms(acc[:, e * HEAD_DIM:(e + 1) * HEAD_DIM], gqa)
            qa_ref[2 * j + e] = (_rope_full(hq, cf, sf) * Q_SCALE).astype(qa_ref.dtype)
    ci, sa, sb = ci_ref[...], sa_ref[...], sb_ref[...]
    low_half = lax.broadcasted_iota(jnp.int32, (cq.shape[0], LANES), 1) < IDX_DIM
    per_dot = 2 * LANES // IDX_DIM
    for j in range(N_IDX_HEADS // per_dot):
        acc = jnp.dot(cq, wi_ref[:, j * 2 * LANES:(j + 1) * 2 * LANES], preferred_element_type=jnp.float32)
        for c in range(2):
            pair = _rope_idx(acc[:, c * LANES:(c + 1) * LANES], ci, sa, sb)
            for e, head in enumerate((pair, pltpu.roll(pair, IDX_DIM, axis=1))):
                qi_ref[per_dot * j + 2 * c + e] = jnp.where(low_half, head, 0.0).astype(qi_ref.dtype)


def _q_proj(proj, g_cq, w_uq, w_uq_idx, g_q_a, tabs, tm=512):
    t = proj.shape[0]
    tm = min(tm, t)
    cf, sf, ci, sa, sb = tabs
    tab_spec = pl.BlockSpec((tm, LANES), lambda i: (i, 0))
    return pl.pallas_call(
        _q_proj_kernel,
        grid=(t // tm,),
        in_specs=[
            pl.BlockSpec((tm, Q_RANK), lambda i: (i, 0)),
            pl.BlockSpec((1, Q_RANK), lambda i: (0, 0)),
            pl.BlockSpec((Q_RANK, W_A), lambda i: (0, 0)),
            pl.BlockSpec((Q_RANK, N_IDX_HEADS * IDX_DIM), lambda i: (0, 0)),
            pl.BlockSpec((1, HEAD_DIM), lambda i: (0, 0)),
            tab_spec, tab_spec, tab_spec, tab_spec, tab_spec,
        ],
        out_specs=[
            pl.BlockSpec((N_HEADS_A, tm, HEAD_DIM), lambda i: (0, i, 0)),
            pl.BlockSpec((N_IDX_HEADS, tm, LANES), lambda i: (0, i, 0)),
        ],
        out_shape=[
            jax.ShapeDtypeStruct((N_HEADS_A, t, HEAD_DIM), jnp.bfloat16),
            jax.ShapeDtypeStruct((N_IDX_HEADS, t, LANES), jnp.bfloat16),
        ],
        compiler_params=_cparams(("parallel",)),
        name="q_proj",
    )(proj, g_cq.reshape(1, Q_RANK), w_uq, w_uq_idx, g_q_a.reshape(1, HEAD_DIM), cf, sf, ci, sa, sb)


def _ka_kernel(x_ref, g_ref, cf_ref, sf_ref, o_ref):
    x = x_ref[...].astype(jnp.float32)
    g, cf, sf = g_ref[...], cf_ref[...], sf_ref[...]
    for h in range(N_KV_A):
        hk = _head_rms(x[:, h * HEAD_DIM:(h + 1) * HEAD_DIM], g)
        o_ref[:, h * HEAD_DIM:(h + 1) * HEAD_DIM] = _rope_full(hk, cf, sf).astype(o_ref.dtype)


def _ka_norm_rope(proj, col, g_k_a, cf, sf, tm=512):
    t = proj.shape[0]
    tm = min(tm, t)
    tab_spec = pl.BlockSpec((tm, LANES), lambda i: (i, 0))
    return pl.pallas_call(
        _ka_kernel,
        grid=(t // tm,),
        in_specs=[
            pl.BlockSpec((tm, KV_A_W), lambda i: (i, col // KV_A_W)),
            pl.BlockSpec((1, HEAD_DIM), lambda i: (0, 0)),
            tab_spec, tab_spec,
        ],
        out_specs=pl.BlockSpec((tm, KV_A_W), lambda i: (i, 0)),
        out_shape=jax.ShapeDtypeStruct((t, KV_A_W), jnp.bfloat16),
        compiler_params=_cparams(("parallel",)),
        name="ka_norm_rope",
    )(proj, g_k_a.reshape(1, HEAD_DIM), cf, sf)


def _split3_bf16(x):
    hi = x.astype(jnp.bfloat16)
    r1 = x - hi.astype(jnp.float32)
    mid = r1.astype(jnp.bfloat16)
    lo = (r1 - mid.astype(jnp.float32)).astype(jnp.bfloat16)
    return hi, mid, lo


def _misc_kernel(m_ref, c_ref, ci_ref, sa_ref, sb_ref, tri_ref, kidx_ref, wf_ref, wt_ref, frow_ref, carry_ref, *,
                 tiles_per_seq):
    i = pl.program_id(0)

    @pl.when(i % tiles_per_seq == 0)
    def _():
        carry_ref[...] = jnp.zeros_like(carry_ref)

    x = m_ref[...]
    lane = lax.broadcasted_iota(jnp.int32, x.shape, 1)
    is_k = lane < IDX_DIM
    xk = jnp.where(is_k, x, 0.0)
    mu = jnp.sum(xk, axis=-1, keepdims=True) * (1.0 / IDX_DIM)
    dk = jnp.where(is_k, x - mu, 0.0)
    var = jnp.sum(dk * dk, axis=-1, keepdims=True) * (1.0 / IDX_DIM)
    y = dk * lax.rsqrt(var + EPS) * c_ref[0:1, :] + c_ref[1:2, :]
    y = _rope_idx(y, ci_ref[...], sa_ref[...], sb_ref[...])
    kidx_ref[...] = jnp.where(is_k, y, 0.0).astype(kidx_ref.dtype)
    f = x + c_ref[2:3, :]
    log_f = jnp.minimum(f, 0.0) - jnp.log1p(jnp.exp(-jnp.abs(f)))
    hi, mid, lo = _split3_bf16(log_f)
    tri = tri_ref[...]
    csum = (jnp.dot(tri, hi, preferred_element_type=jnp.float32)
            + jnp.dot(tri, mid, preferred_element_type=jnp.float32)
            + jnp.dot(tri, lo, preferred_element_type=jnp.float32))
    csum = csum + carry_ref[0:1, :]
    carry_ref[...] = jnp.broadcast_to(csum[-1:, :], carry_ref.shape)
    is_w = (lane >= MISC_W_LANE) & (lane < MISC_F_LANE)
    wf = jnp.where(is_w, x * c_ref[3:4, :], csum * LOG2E)
    wf_ref[...] = wf
    wf_t = wf.T
    wt_ref[...] = wf_t[MISC_W_LANE:MISC_F_LANE, :]
    frow_ref[0] = wf_t[MISC_F_LANE:MISC_F_LANE + N_HEADS_B, :]


def _misc_post(misc, g_kidx, b_kidx, b_forget, tabs, seq):
    t = misc.shape[0]
    tm = min(KV_TILE, seq)
    idx_w_scale = (N_IDX_HEADS ** -0.5) * (IDX_DIM ** -0.5)
    pad = lambda v, off: jnp.zeros((LANES,), jnp.float32).at[off:off + v.shape[0]].set(v)
    consts = jnp.stack([
        pad(g_kidx, 0), pad(b_kidx, 0), pad(b_forget, MISC_F_LANE),
        pad(jnp.full((N_IDX_HEADS,), idx_w_scale, jnp.float32), MISC_W_LANE),
    ] + [jnp.zeros((LANES,), jnp.float32)] * 4)
    tri = jnp.asarray(np.tril(np.ones((tm, tm), np.float32)), jnp.bfloat16)
    _, _, ci, sa, sb = tabs
    tab_spec = pl.BlockSpec((tm, LANES), lambda i: (i, 0))
    return pl.pallas_call(
        functools.partial(_misc_kernel, tiles_per_seq=seq // tm),
        grid=(t // tm,),
        in_specs=[
            tab_spec,
            pl.BlockSpec((8, LANES), lambda i: (0, 0)),
            tab_spec, tab_spec, tab_spec,
            pl.BlockSpec((tm, tm), lambda i: (0, 0)),
        ],
        out_specs=[
            tab_spec, tab_spec,
            pl.BlockSpec((N_IDX_HEADS, tm), lambda i: (0, i)),
            pl.BlockSpec((1, N_HEADS_B, tm), lambda i: (i, 0, 0)),
        ],
        out_shape=[
            jax.ShapeDtypeStruct((t, LANES), jnp.bfloat16),
            jax.ShapeDtypeStruct((t, LANES), jnp.float32),
            jax.ShapeDtypeStruct((N_IDX_HEADS, t), jnp.float32),
            jax.ShapeDtypeStruct((t // tm, N_HEADS_B, tm), jnp.float32),
        ],
        scratch_shapes=[pltpu.VMEM((8, LANES), jnp.float32)],
        compiler_params=_cparams(("arbitrary",)),
        name="misc_post",
    )(misc, consts, ci, sa, sb, tri)


IDX_HEAD_GROUP = 8
IDX_SLAB = 128
COUNT_ROWS = 16
HALVINGS_PER_CHECK = 2


def _indexer_kernel(q_ref, k_ref, w_ref, o_ref, sc_ref, *, tq, tk, n_kt, top_k):
    qi = pl.program_id(1)
    t0 = qi * tq
    n_vis = (t0 + tq + tk - 1) // tk
    t_row = lax.broadcasted_iota(jnp.int32, (1, tq), 1) + t0
    vis_end = (jnp.right_shift(t_row, CHUNK_SHIFT) + 1) * CHUNK

    def score_tile(c, carry):
        mx, mn = carry
        kt = k_ref[pl.ds(pl.multiple_of(c * tk, tk), tk), :]
        for g in range(N_IDX_HEADS // IDX_HEAD_GROUP):
            qg = q_ref[g * IDX_HEAD_GROUP:(g + 1) * IDX_HEAD_GROUP].reshape(IDX_HEAD_GROUP * tq, LANES)
            st = _nt_dot(kt, qg)
            for r in range(tk // IDX_SLAB):
                rows = slice(r * IDX_SLAB, (r + 1) * IDX_SLAB)
                part = jnp.zeros((IDX_SLAB, tq), jnp.float32)
                for j in range(IDX_HEAD_GROUP):
                    h = g * IDX_HEAD_GROUP + j
                    part = part + jnp.maximum(st[rows, j * tq:(j + 1) * tq], 0.0) * w_ref[h:h + 1, :]
                if g == 0:
                    sc_ref[c, rows, :] = part
                else:
                    sc_ref[c, rows, :] += part
        s_col = lax.broadcasted_iota(jnp.int32, (tk, 1), 0) + c * tk
        adm = s_col < vis_end
        sc = sc_ref[c]
        sc_ref[c] = jnp.where(adm, sc, SCORE_NEG)
        mx = jnp.maximum(mx, jnp.max(jnp.where(adm, sc, SCORE_NEG), axis=0, keepdims=True))
        mn = jnp.minimum(mn, jnp.min(jnp.where(adm, sc, -SCORE_NEG), axis=0, keepdims=True))
        return mx, mn

    hi0, lo0 = lax.fori_loop(
        0, n_vis, score_tile,
        (jnp.full((1, tq), SCORE_NEG, jnp.float32), jnp.full((1, tq), -SCORE_NEG, jnp.float32)))

    def unsettled(state):
        it, _, _, active = state
        return jnp.logical_and(it < SELECT_ITERS, jnp.max(active) > 0.0)

    def halve(lo, hi, active):
        mid = 0.5 * (lo + hi)

        def count_tile(c, cnt):
            ge = jnp.where(sc_ref[c] >= mid, 1.0, 0.0)
            return cnt + jnp.sum(ge.reshape(tk // COUNT_ROWS, COUNT_ROWS, tq), axis=0)

        cnt = lax.fori_loop(0, n_vis, count_tile, jnp.zeros((COUNT_ROWS, tq), jnp.float32))
        cnt = jnp.sum(cnt, axis=0, keepdims=True)
        enough = cnt >= float(top_k)
        active = jnp.where(cnt == float(top_k), 0.0, active)
        return jnp.where(enough, mid, lo), jnp.where(enough, hi, mid), active

    def bisect(state):
        it, lo, hi, active = state
        for _ in range(HALVINGS_PER_CHECK):
            lo, hi, active = halve(lo, hi, active)
        return it + HALVINGS_PER_CHECK, lo, hi, active

    active0 = jnp.where(vis_end > top_k, 1.0, 0.0)
    _, thr, _, _ = lax.while_loop(unsettled, bisect, (jnp.int32(0), lo0, hi0, active0))

    def write_tile(c, carry):
        sel = jnp.where(sc_ref[c] >= thr, 0.0, MASK_NEG)
        o_ref[0, c] = sel.T.astype(o_ref.dtype)
        return carry

    lax.fori_loop(0, n_vis, write_tile, 0)

    def fill_tile(c, carry):
        o_ref[0, c] = jnp.full((tq, tk), MASK_NEG, o_ref.dtype)
        return carry

    lax.fori_loop(n_vis, n_kt, fill_tile, 0)


def _indexer_mask(q_idx, k_idx, w_t, batch, seq, top_k, tq=256):
    tq, tk = min(tq, seq), min(KV_TILE, seq)
    n_qt, n_kt = seq // tq, seq // tk
    return pl.pallas_call(
        functools.partial(_indexer_kernel, tq=tq, tk=tk, n_kt=n_kt, top_k=top_k),
        grid=(batch, n_qt),
        in_specs=[
            pl.BlockSpec((N_IDX_HEADS, tq, LANES), lambda b, i: (0, b * n_qt + i, 0)),
            pl.BlockSpec((seq, LANES), lambda b, i: (b, 0)),
            pl.BlockSpec((N_IDX_HEADS, tq), lambda b, i: (0, b * n_qt + i)),
        ],
        out_specs=pl.BlockSpec((1, n_kt, tq, tk), lambda b, i: (b, 0, i, 0)),
        out_shape=jax.ShapeDtypeStruct((batch, n_kt, seq, tk), jnp.bfloat16),
        scratch_shapes=[pltpu.VMEM((n_kt, tk, tq), jnp.float32)],
        compiler_params=_cparams(("parallel", "parallel")),
        name="indexer_mask",
    )(q_idx, k_idx, w_t)


def _softmax_update(s, row_shift, v, m_ref, l_ref, acc_ref, e):
    rows, tk = s.shape
    chunks = [s[:, c * LANES:(c + 1) * LANES] for c in range(tk // LANES)]
    mx = functools.reduce(jnp.maximum, chunks)
    mx = jnp.broadcast_to(jnp.max(mx, axis=-1, keepdims=True), (rows, LANES))
    m_prev = m_ref[e]
    m_new = jnp.maximum(m_prev, mx + row_shift)
    alpha = jnp.exp2(m_prev - m_new)
    r = m_new - row_shift
    p = jnp.concatenate([jnp.exp2(c - r) for c in chunks], axis=1).astype(v.dtype)
    v_ones = jnp.concatenate([v, jnp.ones((tk, LANES), v.dtype)], axis=1)
    pv = jnp.dot(p, v_ones, preferred_element_type=jnp.float32)
    acc_ref[e] = alpha * acc_ref[e] + pv[:, :HEAD_DIM]
    l_ref[e] = alpha * l_ref[e] + pv[:, HEAD_DIM:]
    m_ref[e] = m_new


def _softmax_init(m_ref, l_ref, acc_ref):
    m_ref[...] = jnp.full(m_ref.shape, MASK_NEG, jnp.float32)
    l_ref[...] = jnp.zeros(l_ref.shape, jnp.float32)
    acc_ref[...] = jnp.zeros(acc_ref.shape, jnp.float32)


def _dsa_attn_kernel(q_ref, k_ref, v_ref, b_ref, o_ref, m_ref, l_ref, acc_ref, *, tq, tk):
    qi = pl.program_id(1)
    n_need = ((qi + 1) * tq + tk - 1) // tk
    rows = GROUP_A * tq
    _softmax_init(m_ref, l_ref, acc_ref)

    def kv_step(j, carry):
        off = pl.multiple_of(j * tk, tk)
        bias = b_ref[0, j].astype(jnp.float32)[None]
        for g in range(N_KV_A):
            cols = slice(g * HEAD_DIM, (g + 1) * HEAD_DIM)
            q = q_ref[g * GROUP_A:(g + 1) * GROUP_A].reshape(rows, HEAD_DIM)
            s = _nt_dot(q, k_ref[pl.ds(off, tk), cols])
            s = (s.reshape(GROUP_A, tq, tk) + bias).reshape(rows, tk)
            _softmax_update(s, 0.0, v_ref[pl.ds(off, tk), cols], m_ref, l_ref, acc_ref, g)
        return carry

    lax.fori_loop(0, n_need, kv_step, 0)

    for g in range(N_KV_A):
        out = acc_ref[g] / l_ref[g]
        for h in range(GROUP_A):
            c0 = (g * GROUP_A + h) * HEAD_DIM
            o_ref[:, c0:c0 + HEAD_DIM] = out[h * tq:(h + 1) * tq].astype(o_ref.dtype)


def _dsa_attention(q_a, k_a, proj, v_col, bias, batch, seq, tq=512):
    tk = bias.shape[-1]
    tq = min(tq, seq)
    n_qt, n_kt = seq // tq, seq // tk
    rows = GROUP_A * tq
    return pl.pallas_call(
        functools.partial(_dsa_attn_kernel, tq=tq, tk=tk),
        grid=(batch, n_qt),
        in_specs=[
            pl.BlockSpec((N_HEADS_A, tq, HEAD_DIM), lambda b, i: (0, b * n_qt + i, 0)),
            pl.BlockSpec((seq, KV_A_W), lambda b, i: (b, 0)),
            pl.BlockSpec((seq, KV_A_W), lambda b, i: (b, v_col // KV_A_W)),
            pl.BlockSpec((1, n_kt, tq, tk), lambda b, i: (b, 0, i, 0)),
        ],
        out_specs=pl.BlockSpec((tq, W_A), lambda b, i: (b * n_qt + i, 0)),
        out_shape=jax.ShapeDtypeStruct((batch * seq, W_A), jnp.bfloat16),
        scratch_shapes=[
            pltpu.VMEM((N_KV_A, rows, LANES), jnp.float32),
            pltpu.VMEM((N_KV_A, rows, LANES), jnp.float32),
            pltpu.VMEM((N_KV_A, rows, HEAD_DIM), jnp.float32),
        ],
        compiler_params=_cparams(("parallel", "parallel")),
        name="dsa_attention",
    )(q_a, k_a, proj, bias)


FOX_HEADS_PER_STEP = 4


def _fox_kernel(q_ref, k_ref, v_ref, gq_ref, gk_ref, fq_ref, fk_ref, o_ref, kn_ref, s0_ref, s1_ref, m_ref, l_ref,
                acc_ref, *, tile):
    hp, qi = pl.program_id(1), pl.program_id(2)
    n_t = kn_ref.shape[0] // tile

    @pl.when(qi == 0)
    def _():
        def norm_keys(j, carry):
            off = pl.multiple_of(j * tile, tile)
            for e in range(FOX_HEADS_PER_STEP):
                cols = slice(e * HEAD_DIM, (e + 1) * HEAD_DIM)
                k = k_ref[pl.ds(off, tile), cols].astype(jnp.float32)
                kn_ref[pl.ds(off, tile), cols] = _head_rms(k, gk_ref[...]).astype(kn_ref.dtype)
            return carry

        lax.fori_loop(0, n_t, norm_keys, 0)

    lane = lax.broadcasted_iota(jnp.int32, (tile, LANES), 1)
    heads = [hp * FOX_HEADS_PER_STEP + e for e in range(FOX_HEADS_PER_STEP)]
    fq = [jnp.broadcast_to(jnp.sum(jnp.where(lane == MISC_F_LANE + h, fq_ref[...], 0.0), axis=1, keepdims=True),
                           (tile, LANES)) for h in heads]
    qn = [_head_rms(q_ref[:, e * HEAD_DIM:(e + 1) * HEAD_DIM].astype(jnp.float32), gq_ref[...]).astype(kn_ref.dtype)
          for e in range(FOX_HEADS_PER_STEP)]
    _softmax_init(m_ref, l_ref, acc_ref)

    def logits(j, s_ref):
        off = pl.multiple_of(j * tile, tile)
        for e, h in enumerate(heads):
            cols = slice(e * HEAD_DIM, (e + 1) * HEAD_DIM)
            s_ref[e] = _nt_dot(qn[e], kn_ref[pl.ds(off, tile), cols]) - fk_ref[j, pl.ds(h, 1), :]

    def attend(j, s_ref, diagonal):
        off = pl.multiple_of(j * tile, tile)
        for e in range(FOX_HEADS_PER_STEP):
            cols = slice(e * HEAD_DIM, (e + 1) * HEAD_DIM)
            s = s_ref[e]
            if diagonal:
                t_pos = lax.broadcasted_iota(jnp.int32, (tile, tile), 0)
                s_pos = lax.broadcasted_iota(jnp.int32, (tile, tile), 1)
                s = jnp.where(s_pos <= t_pos, s, MASK_NEG)
            _softmax_update(s, fq[e], v_ref[pl.ds(off, tile), cols], m_ref, l_ref, acc_ref, e)

    def tile_pair(i, carry):
        logits(2 * i + 1, s1_ref)
        attend(2 * i, s0_ref, False)
        logits(2 * i + 2, s0_ref)
        attend(2 * i + 1, s1_ref, False)
        return carry

    logits(0, s0_ref)
    lax.fori_loop(0, qi // 2, tile_pair, 0)

    @pl.when(qi % 2 == 0)
    def _():
        attend(qi, s0_ref, True)

    @pl.when(qi % 2 == 1)
    def _():
        logits(qi, s1_ref)
        attend(qi - 1, s0_ref, False)
        attend(qi, s1_ref, True)

    for e in range(FOX_HEADS_PER_STEP):
        o_ref[:, e * HEAD_DIM:(e + 1) * HEAD_DIM] = (acc_ref[e] / l_ref[e]).astype(o_ref.dtype)


def _fox_attention(proj, q_col, g_q, g_k, wf, f_row, batch, seq):
    tile = min(KV_TILE, seq)
    n_t = seq // tile
    wb = FOX_HEADS_PER_STEP * HEAD_DIM
    q_blk, k_blk, v_blk = q_col // wb, (q_col + W_B) // wb, (q_col + 2 * W_B) // wb
    gain_spec = pl.BlockSpec((1, HEAD_DIM), lambda b, h, i: (0, 0))
    return pl.pallas_call(
        functools.partial(_fox_kernel, tile=tile),
        grid=(batch, N_HEADS_B // FOX_HEADS_PER_STEP, n_t),
        in_specs=[
            pl.BlockSpec((tile, wb), lambda b, h, i: (b * n_t + i, q_blk + h)),
            pl.BlockSpec((seq, wb), lambda b, h, i: (b, k_blk + h)),
            pl.BlockSpec((seq, wb), lambda b, h, i: (b, v_blk + h)),
            gain_spec, gain_spec,
            pl.BlockSpec((tile, LANES), lambda b, h, i: (b * n_t + i, 0)),
            pl.BlockSpec((n_t, N_HEADS_B, tile), lambda b, h, i: (b, 0, 0)),
        ],
        out_specs=pl.BlockSpec((tile, wb), lambda b, h, i: (b * n_t + i, h)),
        out_shape=jax.ShapeDtypeStruct((batch * seq, W_B), jnp.bfloat16),
        scratch_shapes=[
            pltpu.VMEM((seq, wb), jnp.bfloat16),
            pltpu.VMEM((FOX_HEADS_PER_STEP, tile, tile), jnp.float32),
            pltpu.VMEM((FOX_HEADS_PER_STEP, tile, tile), jnp.float32),
            pltpu.VMEM((FOX_HEADS_PER_STEP, tile, LANES), jnp.float32),
            pltpu.VMEM((FOX_HEADS_PER_STEP, tile, LANES), jnp.float32),
            pltpu.VMEM((FOX_HEADS_PER_STEP, tile, HEAD_DIM), jnp.float32),
        ],
        compiler_params=_cparams(("parallel", "parallel", "arbitrary")),
        name="fox_attention",
    )(proj, proj, proj, g_q.reshape(1, HEAD_DIM), g_k.reshape(1, HEAD_DIM), wf, f_row)


def _layer(x, p, pos_col, batch, seq, g_attn, w_in, g_cq, w_uq, w_uq_idx, g_kidx, b_kidx, g_q_a, g_k_a, b_forget,
           g_q_b, g_k_b, w_up_a, w_up_b, w_o, g_ffn, w_ffn_gate, w_ffn_up, w_ffn_down, g_ple, w_ple, w_ple_gate):
    d = x.shape[1]
    bf = jnp.bfloat16
    top_k = min(INDEX_TOPK, seq // 4)

    o_kidx = Q_RANK + 2 * KV_A_W
    o_qb = o_kidx + IDX_DIM + N_IDX_HEADS
    o_fb = o_qb + 3 * W_B
    o_ga = o_fb + N_HEADS_B
    w_in_t = jnp.swapaxes(w_in, 0, 1)
    w_misc = _stage_misc(w_in_t, o_kidx, o_qb - o_kidx, o_fb, o_ga - o_fb)
    col_ka, col_va = Q_RANK, Q_RANK + KV_A_W
    col_qb = Q_RANK + 2 * KV_A_W
    col_vb = col_qb + 2 * W_B
    col_ga = col_vb + W_B
    col_gb = col_ga + d

    tabs = _rope_tables(pos_col)
    cf, sf = tabs[0], tabs[1]

    h = _rmsnorm(x, g_attn)
    proj = _project_regrouped(h, w_in_t, [(0, o_kidx), (o_qb, o_fb - o_qb), (o_ga, 2 * d)])
    misc = _matmul(h, w_misc, jnp.float32, tn=LANES, name="proj_misc")

    q_a, q_idx = _q_proj(proj, g_cq, w_uq.astype(bf), w_uq_idx.astype(bf), g_q_a, tabs)
    k_a = _ka_norm_rope(proj, col_ka, g_k_a, cf, sf)
    k_idx, wf, w_t, f_row = _misc_post(misc, g_kidx, b_kidx, b_forget, tabs, seq)
    bias = _indexer_mask(q_idx, k_idx, w_t, batch, seq, top_k)
    o_a = _dsa_attention(q_a, k_a, proj, col_va, bias, batch, seq)

    o_b = _fox_attention(proj, col_qb, g_q_b * Q_SCALE, g_k_b, wf, f_row, batch, seq)

    merged = _merge(o_a, o_b, w_up_a, w_up_b, proj, col_ga, col_gb)
    x, xg, ssq = _matmul_residual_norm(merged, w_o, x, g_ffn, name="out_proj")

    u = _swiglu(xg, ssq, w_ffn_gate, w_ffn_up, tn=FFN_TILE)
    x, xg, ssq = _matmul_residual_norm(u, w_ffn_down.astype(bf), x, g_ple, tm=512, name="ffn_down")

    return _ple(xg, ssq, w_ple_gate, p.astype(bf), w_ple, x)


def kernel(x, p, positions, g_attn, w_in, g_cq, w_uq, w_uq_idx, g_kidx, b_kidx, g_q_a, g_k_a, b_forget, g_q_b, g_k_b,
           w_up_a, w_up_b, w_o, g_ffn, w_ffn_gate, w_ffn_up, w_ffn_down, g_ple, w_ple, w_ple_gate):
    batch, seq, d = x.shape
    depth = w_in.shape[0]
    xf = x.reshape(batch * seq, d)
    pos_col = positions.reshape(batch * seq, 1)
    for i in range(depth):
        xf = _layer(xf, p[i].reshape(batch * seq, -1), pos_col, batch, seq, g_attn[i], w_in[i], g_cq[i], w_uq[i],
                    w_uq_idx[i], g_kidx[i], b_kidx[i], g_q_a[i], g_k_a[i], b_forget[i], g_q_b[i], g_k_b[i],
                    w_up_a[i], w_up_b[i], w_o[i], g_ffn[i], w_ffn_gate[i], w_ffn_up[i], w_ffn_down[i], g_ple[i],
                    w_ple[i], w_ple_gate[i])
    return xf.reshape(batch, seq, d)
```

```python
import functools

import numpy as np
import jax
import jax.numpy as jnp
from jax import lax
from jax.experimental import pallas as pl
from jax.experimental.pallas import tpu as pltpu

CHUNK = 64
CHUNK_SHIFT = CHUNK.bit_length() - 1
HEAD_DIM = 128
ROPE_THETA = 10000.0
EPS = 1e-6
N_HEADS_A = 16
N_KV_A = 2
Q_RANK = 1024
N_IDX_HEADS = 32
IDX_DIM = 64
IDX_ROPE_DIM = 32
INDEX_TOPK = 256
N_HEADS_B = 16

LANES = 128
V7X_VMEM_BYTES = 64 * 1024 * 1024
KV_A_W = N_KV_A * HEAD_DIM
W_A = N_HEADS_A * HEAD_DIM
W_B = N_HEADS_B * HEAD_DIM
GROUP_A = N_HEADS_A // N_KV_A
IDX_PAIRS = N_IDX_HEADS * IDX_DIM // LANES

MISC_W_LANE = IDX_DIM
MISC_F_LANE = IDX_DIM + N_IDX_HEADS

MASK_NEG = -1e30
SCORE_NEG = -3e38
SELECT_ITERS = 32
LOG2E = 1.4426950408889634
Q_SCALE = HEAD_DIM ** -0.5 * LOG2E
KV_TILE = 512
FFN_TILE = 256
VMEM_LIMIT = V7X_VMEM_BYTES - V7X_VMEM_BYTES // 8


def _cparams(sem):
    return pltpu.CompilerParams(dimension_semantics=sem, vmem_limit_bytes=VMEM_LIMIT)


def _sigmoid(x):
    return 1.0 / (1.0 + jnp.exp(-x))


def _nt_dot(a, b):
    return lax.dot_general(a, b, (((1,), (1,)), ((), ())), preferred_element_type=jnp.float32)


def _wdot(a, w_ref):
    return jnp.dot(a, w_ref[...].astype(a.dtype), preferred_element_type=jnp.float32)


def _proj_regroup_kernel(a_ref, wt_ref, o_ref):
    a = a_ref[...]
    o_ref[...] = _nt_dot(a, wt_ref[...].astype(a.dtype)).astype(o_ref.dtype)


def _project_regrouped(a, w_t, segments, tm=1024, tn=512):
    t, k = a.shape
    lags, dst = [], 0
    for src, width in segments:
        assert width % tn == 0 and src % 16 == 0 and src + width <= w_t.shape[0]
        lags.append((dst // tn, src - dst))
        dst += width
    tm = min(tm, t)

    def src_row(j):
        lag = lags[0][1]
        for first_tile, seg_lag in lags[1:]:
            lag = jnp.where(j >= first_tile, seg_lag, lag)
        return pl.multiple_of(j * tn + lag, 16)

    return pl.pallas_call(
        _proj_regroup_kernel,
        grid=(t // tm, dst // tn),
        in_specs=[
            pl.BlockSpec((tm, k), lambda i, j: (i, 0)),
            pl.BlockSpec((pl.Element(tn), pl.Element(k)), lambda i, j: (src_row(j), 0)),
        ],
        out_specs=pl.BlockSpec((tm, tn), lambda i, j: (i, j)),
        out_shape=jax.ShapeDtypeStruct((t, dst), jnp.bfloat16),
        compiler_params=_cparams(("parallel", "parallel")),
        name="proj_main",
    )(a, w_t)


def _stage_misc_kernel(a_ref, b_ref, o_ref):
    a, b = a_ref[...], b_ref[...]
    pad = jnp.zeros((LANES - a.shape[0] - b.shape[0], a.shape[1]), a.dtype)
    o_ref[...] = jnp.concatenate([a, b, pad], axis=0).T.astype(o_ref.dtype)


def _stage_misc(w_t, start_a, n_a, start_b, n_b, tk=1024):
    k = w_t.shape[1]
    tk = min(tk, k)
    col = lambda kk: pl.multiple_of(kk * tk, tk)
    return pl.pallas_call(
        _stage_misc_kernel,
        grid=(k // tk,),
        in_specs=[
            pl.BlockSpec((pl.Element(n_a), pl.Element(tk)), lambda kk: (start_a, col(kk))),
            pl.BlockSpec((pl.Element(n_b), pl.Element(tk)), lambda kk: (start_b, col(kk))),
        ],
        out_specs=pl.BlockSpec((tk, LANES), lambda kk: (kk, 0)),
        out_shape=jax.ShapeDtypeStruct((k, LANES), jnp.bfloat16),
        compiler_params=_cparams(("parallel",)),
        name="stage_misc",
    )(w_t, w_t)


def _rmsnorm_kernel(x_ref, g_ref, o_ref):
    x = x_ref[...]
    ms = jnp.mean(x * x, axis=-1, keepdims=True)
    o_ref[...] = (x * lax.rsqrt(ms + EPS) * g_ref[...]).astype(o_ref.dtype)


def _rmsnorm(x, g, tm=256):
    t, d = x.shape
    tm = min(tm, t)
    return pl.pallas_call(
        _rmsnorm_kernel,
        grid=(t // tm,),
        in_specs=[pl.BlockSpec((tm, d), lambda i: (i, 0)), pl.BlockSpec((1, d), lambda i: (0, 0))],
        out_specs=pl.BlockSpec((tm, d), lambda i: (i, 0)),
        out_shape=jax.ShapeDtypeStruct((t, d), jnp.bfloat16),
        compiler_params=_cparams(("parallel",)),
        name="rmsnorm",
    )(x, g.reshape(1, d))


def _mm_kernel(a_ref, w_ref, o_ref):
    o_ref[...] = jnp.dot(a_ref[...], w_ref[...], preferred_element_type=jnp.float32).astype(o_ref.dtype)


def _matmul(a, w, out_dtype, tm=1024, tn=512, name="matmul"):
    t, k = a.shape
    n = w.shape[1]
    tm, tn = min(tm, t), min(tn, n)
    return pl.pallas_call(
        _mm_kernel,
        grid=(t // tm, n // tn),
        in_specs=[pl.BlockSpec((tm, k), lambda i, j: (i, 0)), pl.BlockSpec((k, tn), lambda i, j: (0, j))],
        out_specs=pl.BlockSpec((tm, tn), lambda i, j: (i, j)),
        out_shape=jax.ShapeDtypeStruct((t, n), out_dtype),
        compiler_params=_cparams(("parallel", "parallel")),
        name=name,
    )(a, w)


def _mm_res_norm_kernel(a_ref, w_ref, r_ref, g_ref, o_ref, xg_ref, ssq_ref):
    y = r_ref[...] + _wdot(a_ref[...], w_ref)
    o_ref[...] = y
    xg_ref[...] = (y * g_ref[...]).astype(xg_ref.dtype)
    part = jnp.broadcast_to(jnp.sum(y * y, axis=-1, keepdims=True), ssq_ref.shape)

    @pl.when(pl.program_id(1) == 0)
    def _():
        ssq_ref[...] = part

    @pl.when(pl.program_id(1) > 0)
    def _():
        ssq_ref[...] += part


def _matmul_residual_norm(a, w, r, g, tm=1024, tn=512, name="matmul_residual_norm"):
    t, k = a.shape
    n = w.shape[1]
    tm, tn = min(tm, t), min(tn, n)
    tile = pl.BlockSpec((tm, tn), lambda i, j: (i, j))
    return pl.pallas_call(
        _mm_res_norm_kernel,
        grid=(t // tm, n // tn),
        in_specs=[
            pl.BlockSpec((tm, k), lambda i, j: (i, 0)),
            pl.BlockSpec((k, tn), lambda i, j: (0, j)),
            tile,
            pl.BlockSpec((1, tn), lambda i, j: (0, j)),
        ],
        out_specs=[tile, tile, pl.BlockSpec((tm, LANES), lambda i, j: (i, 0))],
        out_shape=[
            jax.ShapeDtypeStruct((t, n), jnp.float32),
            jax.ShapeDtypeStruct((t, n), jnp.bfloat16),
            jax.ShapeDtypeStruct((t, LANES), jnp.float32),
        ],
        compiler_params=_cparams(("parallel", "arbitrary")),
        name=name,
    )(a, w, r, g.reshape(1, n))


def _row_rstd(ssq_ref, width):
    return lax.rsqrt(ssq_ref[...] * (1.0 / width) + EPS)


def _scale_rows(acc, rstd):
    return jnp.concatenate([acc[:, c * LANES:(c + 1) * LANES] * rstd for c in range(acc.shape[1] // LANES)], axis=1)


def _swiglu_kernel(a_ref, ssq_ref, wg_ref, wu_ref, o_ref):
    a = a_ref[...]
    rstd = _row_rstd(ssq_ref, a.shape[1])
    g = _scale_rows(_wdot(a, wg_ref), rstd)
    u = _scale_rows(_wdot(a, wu_ref), rstd)
    o_ref[...] = (g * _sigmoid(g) * u).astype(o_ref.dtype)


def _swiglu(xg, ssq, wg, wu, tm=1024, tn=512):
    t, k = xg.shape
    n = wg.shape[1]
    tm, tn = min(tm, t), min(tn, n)
    return pl.pallas_call(
        _swiglu_kernel,
        grid=(t // tm, n // tn),
        in_specs=[
            pl.BlockSpec((tm, k), lambda i, j: (i, 0)),
            pl.BlockSpec((tm, LANES), lambda i, j: (i, 0)),
            pl.BlockSpec((k, tn), lambda i, j: (0, j)),
            pl.BlockSpec((k, tn), lambda i, j: (0, j)),
        ],
        out_specs=pl.BlockSpec((tm, tn), lambda i, j: (i, j)),
        out_shape=jax.ShapeDtypeStruct((t, n), jnp.bfloat16),
        compiler_params=_cparams(("parallel", "parallel")),
        name="swiglu",
    )(xg, ssq, wg, wu)


def _merge_kernel(oa_ref, ob_ref, wa_ref, wb_ref, ga_ref, gb_ref, o_ref):
    a = _wdot(oa_ref[...], wa_ref)
    b = _wdot(ob_ref[...], wb_ref)
    ga = _sigmoid(ga_ref[...].astype(jnp.float32))
    gb = _sigmoid(gb_ref[...].astype(jnp.float32))
    o_ref[...] = (ga * a + gb * b).astype(o_ref.dtype)


def _merge(o_a, o_b, w_up_a, w_up_b, proj, ga_col, gb_col, tm=1024, tn=512):
    t, ka = o_a.shape
    kb = o_b.shape[1]
    n = w_up_a.shape[1]
    tm, tn = min(tm, t), min(tn, n)
    ga_blk, gb_blk = ga_col // tn, gb_col // tn
    return pl.pallas_call(
        _merge_kernel,
        grid=(t // tm, n // tn),
        in_specs=[
            pl.BlockSpec((tm, ka), lambda i, j: (i, 0)),
            pl.BlockSpec((tm, kb), lambda i, j: (i, 0)),
            pl.BlockSpec((ka, tn), lambda i, j: (0, j)),
            pl.BlockSpec((kb, tn), lambda i, j: (0, j)),
            pl.BlockSpec((tm, tn), lambda i, j: (i, ga_blk + j)),
            pl.BlockSpec((tm, tn), lambda i, j: (i, gb_blk + j)),
        ],
        out_specs=pl.BlockSpec((tm, tn), lambda i, j: (i, j)),
        out_shape=jax.ShapeDtypeStruct((t, n), jnp.bfloat16),
        compiler_params=_cparams(("parallel", "parallel")),
        name="merge",
    )(o_a, o_b, w_up_a, w_up_b, proj, proj)


def _ple_kernel(xg_ref, ssq_ref, wg_ref, p_ref, wp_ref, r_ref, o_ref):
    a = xg_ref[...]
    g = _scale_rows(_wdot(a, wg_ref), _row_rstd(ssq_ref, a.shape[1]))
    e = _wdot(p_ref[...], wp_ref)
    o_ref[...] = r_ref[...] + _sigmoid(g) * e


def _ple(xg, ssq, w_gate, p, w_ple, r, tm=1024, tn=512):
    t, k = xg.shape
    kp = p.shape[1]
    n = w_gate.shape[1]
    tm, tn = min(tm, t), min(tn, n)
    return pl.pallas_call(
        _ple_kernel,
        grid=(t // tm, n // tn),
        in_specs=[
            pl.BlockSpec((tm, k), lambda i, j: (i, 0)),
            pl.BlockSpec((tm, LANES), lambda i, j: (i, 0)),
            pl.BlockSpec((k, tn), lambda i, j: (0, j)),
            pl.BlockSpec((tm, kp), lambda i, j: (i, 0)),
            pl.BlockSpec((kp, tn), lambda i, j: (0, j)),
            pl.BlockSpec((tm, tn), lambda i, j: (i, j)),
        ],
        out_specs=pl.BlockSpec((tm, tn), lambda i, j: (i, j)),
        out_shape=jax.ShapeDtypeStruct((t, n), jnp.float32),
        compiler_params=_cparams(("parallel", "parallel")),
        name="ple",
    )(xg, ssq, w_gate, p, w_ple, r)


def _rope_consts():
    half = HEAD_DIM // 2
    inv_full = jnp.power(ROPE_THETA, -jnp.arange(half, dtype=jnp.float32) * (2.0 / HEAD_DIM))
    half_i = IDX_ROPE_DIM // 2
    inv_idx = jnp.power(ROPE_THETA, -jnp.arange(half_i, dtype=jnp.float32) * (2.0 / IDX_ROPE_DIM))
    zeros = jnp.zeros((IDX_DIM - IDX_ROPE_DIM,), jnp.float32)
    sign = np.concatenate([-np.ones(half, np.float32), np.ones(half, np.float32)])
    mask_a = np.zeros(IDX_DIM, np.float32)
    mask_a[:half_i] = -1.0
    mask_b = np.zeros(IDX_DIM, np.float32)
    mask_b[half_i:IDX_ROPE_DIM] = 1.0
    reps = LANES // IDX_DIM
    rows = [
        jnp.concatenate([inv_full, inv_full]),
        jnp.asarray(sign),
        jnp.tile(jnp.concatenate([inv_idx, inv_idx, zeros]), reps),
        jnp.asarray(np.tile(mask_a, reps)),
        jnp.asarray(np.tile(mask_b, reps)),
    ]
    rows += [jnp.zeros((LANES,), jnp.float32)] * 3
    return jnp.stack(rows)


def _rope_tables_kernel(pos_ref, c_ref, cf_ref, sf_ref, ci_ref, sa_ref, sb_ref):
    pos = pos_ref[...].astype(jnp.float32)
    ang = pos * c_ref[0:1, :]
    cf_ref[...] = jnp.cos(ang)
    sf_ref[...] = jnp.sin(ang) * c_ref[1:2, :]
    ang_i = pos * c_ref[2:3, :]
    ci_ref[...] = jnp.cos(ang_i)
    s_i = jnp.sin(ang_i)
    sa_ref[...] = s_i * c_ref[3:4, :]
    sb_ref[...] = s_i * c_ref[4:5, :]


def _rope_tables(pos_col, tm=512):
    t = pos_col.shape[0]
    tm = min(tm, t)
    tab = jax.ShapeDtypeStruct((t, LANES), jnp.float32)
    spec = pl.BlockSpec((tm, LANES), lambda i: (i, 0))
    return pl.pallas_call(
        _rope_tables_kernel,
        grid=(t // tm,),
        in_specs=[pl.BlockSpec((tm, 1), lambda i: (i, 0)), pl.BlockSpec((8, LANES), lambda i: (0, 0))],
        out_specs=[spec] * 5,
        out_shape=[tab] * 5,
        compiler_params=_cparams(("parallel",)),
        name="rope_tables",
    )(pos_col, _rope_consts())


def _rope_full(x, cf, sf):
    return x * cf + pltpu.roll(x, HEAD_DIM // 2, axis=1) * sf


def _rope_idx(x, ci, sa, sb):
    half = IDX_ROPE_DIM // 2
    return x * ci + pltpu.roll(x, LANES - half, axis=1) * sa + pltpu.roll(x, half, axis=1) * sb


def _head_rms(x, g):
    ms = jnp.mean(x * x, axis=-1, keepdims=True)
    return x * lax.rsqrt(ms + EPS) * g


def _q_proj_kernel(cq_ref, gcq_ref, wa_ref, wi_ref, gqa_ref, cf_ref, sf_ref, ci_ref, sa_ref, sb_ref, qa_ref, qi_ref):
    cq = cq_ref[...].astype(jnp.float32)
    ms = jnp.mean(cq * cq, axis=-1, keepdims=True)
    cq = (cq * lax.rsqrt(ms + EPS) * gcq_ref[...]).astype(jnp.bfloat16)
    cf, sf = cf_ref[...], sf_ref[...]
    gqa = gqa_ref[...]
    for j in range(N_HEADS_A // 2):
        acc = jnp.dot(cq, wa_ref[:, j * 2 * HEAD_DIM:(j + 1) * 2 * HEAD_DIM], preferred_element_type=jnp.float32)
        for e in range(2):
            hq = _head_rms(acc[:, e * HEAD_DIM:(e + 1) * HEAD_DIM], gqa)
            qa_ref[2 * j + e] = (_rope_full(hq, cf, sf) * Q_SCALE).astype(qa_ref.dtype)
    ci, sa, sb = ci_ref[...], sa_ref[...], sb_ref[...]
    for j in range(wi_ref.shape[1] // (2 * LANES)):
        acc = jnp.dot(cq, wi_ref[:, j * 2 * LANES:(j + 1) * 2 * LANES], preferred_element_type=jnp.float32)
        for c in range(2):
            qi_ref[2 * j + c] = _rope_idx(acc[:, c * LANES:(c + 1) * LANES], ci, sa, sb).astype(qi_ref.dtype)


def _q_proj(proj, g_cq, w_uq, w_uq_idx, g_q_a, tabs, tm=512):
    t = proj.shape[0]
    tm = min(tm, t)
    cf, sf, ci, sa, sb = tabs
    tab_spec = pl.BlockSpec((tm, LANES), lambda i: (i, 0))
    return pl.pallas_call(
        _q_proj_kernel,
        grid=(t // tm,),
        in_specs=[
            pl.BlockSpec((tm, Q_RANK), lambda i: (i, 0)),
            pl.BlockSpec((1, Q_RANK), lambda i: (0, 0)),
            pl.BlockSpec((Q_RANK, W_A), lambda i: (0, 0)),
            pl.BlockSpec((Q_RANK, N_IDX_HEADS * IDX_DIM), lambda i: (0, 0)),
            pl.BlockSpec((1, HEAD_DIM), lambda i: (0, 0)),
            tab_spec, tab_spec, tab_spec, tab_spec, tab_spec,
        ],
        out_specs=[
            pl.BlockSpec((N_HEADS_A, tm, HEAD_DIM), lambda i: (0, i, 0)),
            pl.BlockSpec((IDX_PAIRS, tm, LANES), lambda i: (0, i, 0)),
        ],
        out_shape=[
            jax.ShapeDtypeStruct((N_HEADS_A, t, HEAD_DIM), jnp.bfloat16),
            jax.ShapeDtypeStruct((IDX_PAIRS, t, LANES), jnp.bfloat16),
        ],
        compiler_params=_cparams(("parallel",)),
        name="q_proj",
    )(proj, g_cq.reshape(1, Q_RANK), w_uq, w_uq_idx, g_q_a.reshape(1, HEAD_DIM), cf, sf, ci, sa, sb)


def _ka_kernel(x_ref, g_ref, cf_ref, sf_ref, o_ref):
    x = x_ref[...].astype(jnp.float32)
    g, cf, sf = g_ref[...], cf_ref[...], sf_ref[...]
    for h in range(N_KV_A):
        hk = _head_rms(x[:, h * HEAD_DIM:(h + 1) * HEAD_DIM], g)
        o_ref[:, h * HEAD_DIM:(h + 1) * HEAD_DIM] = _rope_full(hk, cf, sf).astype(o_ref.dtype)


def _ka_norm_rope(proj, col, g_k_a, cf, sf, tm=512):
    t = proj.shape[0]
    tm = min(tm, t)
    tab_spec = pl.BlockSpec((tm, LANES), lambda i: (i, 0))
    return pl.pallas_call(
        _ka_kernel,
        grid=(t // tm,),
        in_specs=[
            pl.BlockSpec((tm, KV_A_W), lambda i: (i, col // KV_A_W)),
            pl.BlockSpec((1, HEAD_DIM), lambda i: (0, 0)),
            tab_spec, tab_spec,
        ],
        out_specs=pl.BlockSpec((tm, KV_A_W), lambda i: (i, 0)),
        out_shape=jax.ShapeDtypeStruct((t, KV_A_W), jnp.bfloat16),
        compiler_params=_cparams(("parallel",)),
        name="ka_norm_rope",
    )(proj, g_k_a.reshape(1, HEAD_DIM), cf, sf)


def _split3_bf16(x):
    hi = x.astype(jnp.bfloat16)
    r1 = x - hi.astype(jnp.float32)
    mid = r1.astype(jnp.bfloat16)
    lo = (r1 - mid.astype(jnp.float32)).astype(jnp.bfloat16)
    return hi, mid, lo


def _misc_kernel(m_ref, c_ref, ci_ref, sa_ref, sb_ref, tri_ref, kidx_ref, wf_ref, wt_ref, frow_ref, carry_ref, *,
                 tiles_per_seq):
    i = pl.program_id(0)

    @pl.when(i % tiles_per_seq == 0)
    def _():
        carry_ref[...] = jnp.zeros_like(carry_ref)

    x = m_ref[...]
    lane = lax.broadcasted_iota(jnp.int32, x.shape, 1)
    is_k = lane < IDX_DIM
    xk = jnp.where(is_k, x, 0.0)
    mu = jnp.sum(xk, axis=-1, keepdims=True) * (1.0 / IDX_DIM)
    dk = jnp.where(is_k, x - mu, 0.0)
    var = jnp.sum(dk * dk, axis=-1, keepdims=True) * (1.0 / IDX_DIM)
    y = dk * lax.rsqrt(var + EPS) * c_ref[0:1, :] + c_ref[1:2, :]
    y = _rope_idx(y, ci_ref[...], sa_ref[...], sb_ref[...])
    y = jnp.where(is_k, y, 0.0)
    kidx_ref[...] = (y + pltpu.roll(y, IDX_DIM, axis=1)).astype(kidx_ref.dtype)
    f = x + c_ref[2:3, :]
    log_f = jnp.minimum(f, 0.0) - jnp.log1p(jnp.exp(-jnp.abs(f)))
    hi, mid, lo = _split3_bf16(log_f)
    tri = tri_ref[...]
    csum = (jnp.dot(tri, hi, preferred_element_type=jnp.float32)
            + jnp.dot(tri, mid, preferred_element_type=jnp.float32)
            + jnp.dot(tri, lo, preferred_element_type=jnp.float32))
    csum = csum + carry_ref[0:1, :]
    carry_ref[...] = jnp.broadcast_to(csum[-1:, :], carry_ref.shape)
    is_w = (lane >= MISC_W_LANE) & (lane < MISC_F_LANE)
    wf = jnp.where(is_w, x * c_ref[3:4, :], csum * LOG2E)
    wf_ref[...] = wf
    wf_t = wf.T
    wt_ref[...] = wf_t[MISC_W_LANE:MISC_F_LANE, :]
    frow_ref[0] = wf_t[MISC_F_LANE:MISC_F_LANE + N_HEADS_B, :]


def _misc_post(misc, g_kidx, b_kidx, b_forget, tabs, seq):
    t = misc.shape[0]
    tm = min(KV_TILE, seq)
    idx_w_scale = (N_IDX_HEADS ** -0.5) * (IDX_DIM ** -0.5)
    pad = lambda v, off: jnp.zeros((LANES,), jnp.float32).at[off:off + v.shape[0]].set(v)
    consts = jnp.stack([
        pad(g_kidx, 0), pad(b_kidx, 0), pad(b_forget, MISC_F_LANE),
        pad(jnp.full((N_IDX_HEADS,), idx_w_scale, jnp.float32), MISC_W_LANE),
    ] + [jnp.zeros((LANES,), jnp.float32)] * 4)
    tri = jnp.asarray(np.tril(np.ones((tm, tm), np.float32)), jnp.bfloat16)
    _, _, ci, sa, sb = tabs
    tab_spec = pl.BlockSpec((tm, LANES), lambda i: (i, 0))
    return pl.pallas_call(
        functools.partial(_misc_kernel, tiles_per_seq=seq // tm),
        grid=(t // tm,),
        in_specs=[
            tab_spec,
            pl.BlockSpec((8, LANES), lambda i: (0, 0)),
            tab_spec, tab_spec, tab_spec,
            pl.BlockSpec((tm, tm), lambda i: (0, 0)),
        ],
        out_specs=[
            tab_spec, tab_spec,
            pl.BlockSpec((N_IDX_HEADS, tm), lambda i: (0, i)),
            pl.BlockSpec((1, N_HEADS_B, tm), lambda i: (i, 0, 0)),
        ],
        out_shape=[
            jax.ShapeDtypeStruct((t, LANES), jnp.bfloat16),
            jax.ShapeDtypeStruct((t, LANES), jnp.float32),
            jax.ShapeDtypeStruct((N_IDX_HEADS, t), jnp.float32),
            jax.ShapeDtypeStruct((t // tm, N_HEADS_B, tm), jnp.float32),
        ],
        scratch_shapes=[pltpu.VMEM((8, LANES), jnp.float32)],
        compiler_params=_cparams(("arbitrary",)),
        name="misc_post",
    )(misc, consts, ci, sa, sb, tri)


IDX_HEAD_GROUP = 8
IDX_SLAB = 128
COUNT_ROWS = 16
HALVINGS_PER_CHECK = 2


def _indexer_kernel(q_ref, k_ref, w_ref, o_ref, sc_ref, qh_ref, *, tq, tk, n_kt, top_k):
    qi = pl.program_id(1)
    t0 = qi * tq
    n_vis = (t0 + tq + tk - 1) // tk
    t_row = lax.broadcasted_iota(jnp.int32, (1, tq), 1) + t0
    vis_end = (jnp.right_shift(t_row, CHUNK_SHIFT) + 1) * CHUNK

    low_half = lax.broadcasted_iota(jnp.int32, (tq, LANES), 1) < IDX_DIM
    for p in range(IDX_PAIRS):
        pair = q_ref[p]
        qh_ref[2 * p] = jnp.where(low_half, pair, jnp.zeros_like(pair))
        qh_ref[2 * p + 1] = jnp.where(low_half, jnp.zeros_like(pair), pair)

    def score_tile(c, carry):
        mx, mn = carry
        kt = k_ref[pl.ds(pl.multiple_of(c * tk, tk), tk), :]
        for g in range(N_IDX_HEADS // IDX_HEAD_GROUP):
            qg = qh_ref[g * IDX_HEAD_GROUP:(g + 1) * IDX_HEAD_GROUP].reshape(IDX_HEAD_GROUP * tq, LANES)
            st = _nt_dot(kt, qg)
            for r in range(tk // IDX_SLAB):
                rows = slice(r * IDX_SLAB, (r + 1) * IDX_SLAB)
                part = jnp.zeros((IDX_SLAB, tq), jnp.float32)
                for j in range(IDX_HEAD_GROUP):
                    h = g * IDX_HEAD_GROUP + j
                    part = part + jnp.maximum(st[rows, j * tq:(j + 1) * tq], 0.0) * w_ref[h:h + 1, :]
                if g == 0:
                    sc_ref[c, rows, :] = part
                else:
                    sc_ref[c, rows, :] += part
        s_col = lax.broadcasted_iota(jnp.int32, (tk, 1), 0) + c * tk
        adm = s_col < vis_end
        sc = sc_ref[c]
        sc_ref[c] = jnp.where(adm, sc, SCORE_NEG)
        mx = jnp.maximum(mx, jnp.max(jnp.where(adm, sc, SCORE_NEG), axis=0, keepdims=True))
        mn = jnp.minimum(mn, jnp.min(jnp.where(adm, sc, -SCORE_NEG), axis=0, keepdims=True))
        return mx, mn

    hi0, lo0 = lax.fori_loop(
        0, n_vis, score_tile,
        (jnp.full((1, tq), SCORE_NEG, jnp.float32), jnp.full((1, tq), -SCORE_NEG, jnp.float32)))

    def unsettled(state):
        it, _, _, active = state
        return jnp.logical_and(it < SELECT_ITERS, jnp.max(active) > 0.0)

    def halve(lo, hi, active):
        mid = 0.5 * (lo + hi)

        def count_tile(c, cnt):
            ge = jnp.where(sc_ref[c] >= mid, 1.0, 0.0)
            return cnt + jnp.sum(ge.reshape(tk // COUNT_ROWS, COUNT_ROWS, tq), axis=0)

        cnt = lax.fori_loop(0, n_vis, count_tile, jnp.zeros((COUNT_ROWS, tq), jnp.float32))
        cnt = jnp.sum(cnt, axis=0, keepdims=True)
        enough = cnt >= float(top_k)
        active = jnp.where(cnt == float(top_k), 0.0, active)
        return jnp.where(enough, mid, lo), jnp.where(enough, hi, mid), active

    def bisect(state):
        it, lo, hi, active = state
        for _ in range(HALVINGS_PER_CHECK):
            lo, hi, active = halve(lo, hi, active)
        return it + HALVINGS_PER_CHECK, lo, hi, active

    active0 = jnp.where(vis_end > top_k, 1.0, 0.0)
    _, thr, _, _ = lax.while_loop(unsettled, bisect, (jnp.int32(0), lo0, hi0, active0))

    def write_tile(c, carry):
        sel = jnp.where(sc_ref[c] >= thr, 0.0, MASK_NEG)
        o_ref[0, c] = sel.T.astype(o_ref.dtype)
        return carry

    lax.fori_loop(0, n_vis, write_tile, 0)

    def fill_tile(c, carry):
        o_ref[0, c] = jnp.full((tq, tk), MASK_NEG, o_ref.dtype)
        return carry

    lax.fori_loop(n_vis, n_kt, fill_tile, 0)


def _indexer_mask(q_idx, k_idx, w_t, batch, seq, top_k, tq=256):
    tq, tk = min(tq, seq), min(KV_TILE, seq)
    n_qt, n_kt = seq // tq, seq // tk
    return pl.pallas_call(
        functools.partial(_indexer_kernel, tq=tq, tk=tk, n_kt=n_kt, top_k=top_k),
        grid=(batch, n_qt),
        in_specs=[
            pl.BlockSpec((IDX_PAIRS, tq, LANES), lambda b, i: (0, b * n_qt + i, 0)),
            pl.BlockSpec((seq, LANES), lambda b, i: (b, 0)),
            pl.BlockSpec((N_IDX_HEADS, tq), lambda b, i: (0, b * n_qt + i)),
        ],
        out_specs=pl.BlockSpec((1, n_kt, tq, tk), lambda b, i: (b, 0, i, 0)),
        out_shape=jax.ShapeDtypeStruct((batch, n_kt, seq, tk), jnp.bfloat16),
        scratch_shapes=[
            pltpu.VMEM((n_kt, tk, tq), jnp.float32),
            pltpu.VMEM((N_IDX_HEADS, tq, LANES), jnp.bfloat16),
        ],
        compiler_params=_cparams(("parallel", "parallel")),
        name="indexer_mask",
    )(q_idx, k_idx, w_t)


def _softmax_update(s, row_shift, v, m_ref, l_ref, acc_ref, e):
    rows, tk = s.shape
    chunks = [s[:, c * LANES:(c + 1) * LANES] for c in range(tk // LANES)]
    mx = functools.reduce(jnp.maximum, chunks)
    mx = jnp.broadcast_to(jnp.max(mx, axis=-1, keepdims=True), (rows, LANES))
    m_prev = m_ref[e]
    m_new = jnp.maximum(m_prev, mx + row_shift)
    alpha = jnp.exp2(m_prev - m_new)
    r = m_new - row_shift
    p = jnp.concatenate([jnp.exp2(c - r) for c in chunks], axis=1).astype(v.dtype)
    v_ones = jnp.concatenate([v, jnp.ones((tk, LANES), v.dtype)], axis=1)
    pv = jnp.dot(p, v_ones, preferred_element_type=jnp.float32)
    acc_ref[e] = alpha * acc_ref[e] + pv[:, :HEAD_DIM]
    l_ref[e] = alpha * l_ref[e] + pv[:, HEAD_DIM:]
    m_ref[e] = m_new


def _softmax_init(m_ref, l_ref, acc_ref):
    m_ref[...] = jnp.full(m_ref.shape, MASK_NEG, jnp.float32)
    l_ref[...] = jnp.zeros(l_ref.shape, jnp.float32)
    acc_ref[...] = jnp.zeros(acc_ref.shape, jnp.float32)


def _dsa_attn_kernel(q_ref, k_ref, v_ref, b_ref, o_ref, m_ref, l_ref, acc_ref, *, tq, tk):
    qi = pl.program_id(1)
    n_need = ((qi + 1) * tq + tk - 1) // tk
    rows = GROUP_A * tq
    _softmax_init(m_ref, l_ref, acc_ref)

    def kv_step(j, carry):
        off = pl.multiple_of(j * tk, tk)
        bias = b_ref[0, j].astype(jnp.float32)[None]
        for g in range(N_KV_A):
            cols = slice(g * HEAD_DIM, (g + 1) * HEAD_DIM)
            q = q_ref[g * GROUP_A:(g + 1) * GROUP_A].reshape(rows, HEAD_DIM)
            s = _nt_dot(q, k_ref[pl.ds(off, tk), cols])
            s = (s.reshape(GROUP_A, tq, tk) + bias).reshape(rows, tk)
            _softmax_update(s, 0.0, v_ref[pl.ds(off, tk), cols], m_ref, l_ref, acc_ref, g)
        return carry

    lax.fori_loop(0, n_need, kv_step, 0)

    for g in range(N_KV_A):
        out = acc_ref[g] / l_ref[g]
        for h in range(GROUP_A):
            c0 = (g * GROUP_A + h) * HEAD_DIM
            o_ref[:, c0:c0 + HEAD_DIM] = out[h * tq:(h + 1) * tq].astype(o_ref.dtype)


def _dsa_attention(q_a, k_a, proj, v_col, bias, batch, seq, tq=512):
    tk = bias.shape[-1]
    tq = min(tq, seq)
    n_qt, n_kt = seq // tq, seq // tk
    rows = GROUP_A * tq
    return pl.pallas_call(
        functools.partial(_dsa_attn_kernel, tq=tq, tk=tk),
        grid=(batch, n_qt),
        in_specs=[
            pl.BlockSpec((N_HEADS_A, tq, HEAD_DIM), lambda b, i: (0, b * n_qt + i, 0)),
            pl.BlockSpec((seq, KV_A_W), lambda b, i: (b, 0)),
            pl.BlockSpec((seq, KV_A_W), lambda b, i: (b, v_col // KV_A_W)),
            pl.BlockSpec((1, n_kt, tq, tk), lambda b, i: (b, 0, i, 0)),
        ],
        out_specs=pl.BlockSpec((tq, W_A), lambda b, i: (b * n_qt + i, 0)),
        out_shape=jax.ShapeDtypeStruct((batch * seq, W_A), jnp.bfloat16),
        scratch_shapes=[
            pltpu.VMEM((N_KV_A, rows, LANES), jnp.float32),
            pltpu.VMEM((N_KV_A, rows, LANES), jnp.float32),
            pltpu.VMEM((N_KV_A, rows, HEAD_DIM), jnp.float32),
        ],
        compiler_params=_cparams(("parallel", "parallel")),
        name="dsa_attention",
    )(q_a, k_a, proj, bias)


FOX_HEADS_PER_STEP = 4


def _fox_kernel(q_ref, k_ref, v_ref, gq_ref, gk_ref, fq_ref, fk_ref, o_ref, kn_ref, s0_ref, s1_ref, m_ref, l_ref,
                acc_ref, *, tile):
    hp, qi = pl.program_id(1), pl.program_id(2)
    n_t = kn_ref.shape[0] // tile

    @pl.when(qi == 0)
    def _():
        def norm_keys(j, carry):
            off = pl.multiple_of(j * tile, tile)
            for e in range(FOX_HEADS_PER_STEP):
                cols = slice(e * HEAD_DIM, (e + 1) * HEAD_DIM)
                k = k_ref[pl.ds(off, tile), cols].astype(jnp.float32)
                kn_ref[pl.ds(off, tile), cols] = _head_rms(k, gk_ref[...]).astype(kn_ref.dtype)
            return carry

        lax.fori_loop(0, n_t, norm_keys, 0)

    lane = lax.broadcasted_iota(jnp.int32, (tile, LANES), 1)
    heads = [hp * FOX_HEADS_PER_STEP + e for e in range(FOX_HEADS_PER_STEP)]
    fq = [jnp.broadcast_to(jnp.sum(jnp.where(lane == MISC_F_LANE + h, fq_ref[...], 0.0), axis=1, keepdims=True),
                           (tile, LANES)) for h in heads]
    qn = [_head_rms(q_ref[:, e * HEAD_DIM:(e + 1) * HEAD_DIM].astype(jnp.float32), gq_ref[...]).astype(kn_ref.dtype)
          for e in range(FOX_HEADS_PER_STEP)]
    _softmax_init(m_ref, l_ref, acc_ref)

    def logits(j, s_ref):
        off = pl.multiple_of(j * tile, tile)
        for e, h in enumerate(heads):
            cols = slice(e * HEAD_DIM, (e + 1) * HEAD_DIM)
            s_ref[e] = _nt_dot(qn[e], kn_ref[pl.ds(off, tile), cols]) - fk_ref[j, pl.ds(h, 1), :]

    def attend(j, s_ref, diagonal):
        off = pl.multiple_of(j * tile, tile)
        for e in range(FOX_HEADS_PER_STEP):
            cols = slice(e * HEAD_DIM, (e + 1) * HEAD_DIM)
            s = s_ref[e]
            if diagonal:
                t_pos = lax.broadcasted_iota(jnp.int32, (tile, tile), 0)
                s_pos = lax.broadcasted_iota(jnp.int32, (tile, tile), 1)
                s = jnp.where(s_pos <= t_pos, s, MASK_NEG)
            _softmax_update(s, fq[e], v_ref[pl.ds(off, tile), cols], m_ref, l_ref, acc_ref, e)

    def tile_pair(i, carry):
        logits(2 * i + 1, s1_ref)
        attend(2 * i, s0_ref, False)
        logits(2 * i + 2, s0_ref)
        attend(2 * i + 1, s1_ref, False)
        return carry

    logits(0, s0_ref)
    lax.fori_loop(0, qi // 2, tile_pair, 0)

    @pl.when(qi % 2 == 0)
    def _():
        attend(qi, s0_ref, True)

    @pl.when(qi % 2 == 1)
    def _():
        logits(qi, s1_ref)
        attend(qi - 1, s0_ref, False)
        attend(qi, s1_ref, True)

    for e in range(FOX_HEADS_PER_STEP):
        o_ref[:, e * HEAD_DIM:(e + 1) * HEAD_DIM] = (acc_ref[e] / l_ref[e]).astype(o_ref.dtype)


def _fox_attention(proj, q_col, g_q, g_k, wf, f_row, batch, seq):
    tile = min(KV_TILE, seq)
    n_t = seq // tile
    wb = FOX_HEADS_PER_STEP * HEAD_DIM
    q_blk, k_blk, v_blk = q_col // wb, (q_col + W_B) // wb, (q_col + 2 * W_B) // wb
    gain_spec = pl.BlockSpec((1, HEAD_DIM), lambda b, h, i: (0, 0))
    return pl.pallas_call(
        functools.partial(_fox_kernel, tile=tile),
        grid=(batch, N_HEADS_B // FOX_HEADS_PER_STEP, n_t),
        in_specs=[
            pl.BlockSpec((tile, wb), lambda b, h, i: (b * n_t + i, q_blk + h)),
            pl.BlockSpec((seq, wb), lambda b, h, i: (b, k_blk + h)),
            pl.BlockSpec((seq, wb), lambda b, h, i: (b, v_blk + h)),
            gain_spec, gain_spec,
            pl.BlockSpec((tile, LANES), lambda b, h, i: (b * n_t + i, 0)),
            pl.BlockSpec((n_t, N_HEADS_B, tile), lambda b, h, i: (b, 0, 0)),
        ],
        out_specs=pl.BlockSpec((tile, wb), lambda b, h, i: (b * n_t + i, h)),
        out_shape=jax.ShapeDtypeStruct((batch * seq, W_B), jnp.bfloat16),
        scratch_shapes=[
            pltpu.VMEM((seq, wb), jnp.bfloat16),
            pltpu.VMEM((FOX_HEADS_PER_STEP, tile, tile), jnp.float32),
            pltpu.VMEM((FOX_HEADS_PER_STEP, tile, tile), jnp.float32),
            pltpu.VMEM((FOX_HEADS_PER_STEP, tile, LANES), jnp.float32),
            pltpu.VMEM((FOX_HEADS_PER_STEP, tile, LANES), jnp.float32),
            pltpu.VMEM((FOX_HEADS_PER_STEP, tile, HEAD_DIM), jnp.float32),
        ],
        compiler_params=_cparams(("parallel", "parallel", "arbitrary")),
        name="fox_attention",
    )(proj, proj, proj, g_q.reshape(1, HEAD_DIM), g_k.reshape(1, HEAD_DIM), wf, f_row)


def _layer(x, p, pos_col, batch, seq, g_attn, w_in, g_cq, w_uq, w_uq_idx, g_kidx, b_kidx, g_q_a, g_k_a, b_forget,
           g_q_b, g_k_b, w_up_a, w_up_b, w_o, g_ffn, w_ffn_gate, w_ffn_up, w_ffn_down, g_ple, w_ple, w_ple_gate):
    d = x.shape[1]
    bf = jnp.bfloat16
    top_k = min(INDEX_TOPK, seq // 4)

    o_kidx = Q_RANK + 2 * KV_A_W
    o_qb = o_kidx + IDX_DIM + N_IDX_HEADS
    o_fb = o_qb + 3 * W_B
    o_ga = o_fb + N_HEADS_B
    w_in_t = jnp.swapaxes(w_in, 0, 1)
    w_misc = _stage_misc(w_in_t, o_kidx, o_qb - o_kidx, o_fb, o_ga - o_fb)
    col_ka, col_va = Q_RANK, Q_RANK + KV_A_W
    col_qb = Q_RANK + 2 * KV_A_W
    col_vb = col_qb + 2 * W_B
    col_ga = col_vb + W_B
    col_gb = col_ga + d

    tabs = _rope_tables(pos_col)
    cf, sf = tabs[0], tabs[1]

    h = _rmsnorm(x, g_attn)
    proj = _project_regrouped(h, w_in_t, [(0, o_kidx), (o_qb, o_fb - o_qb), (o_ga, 2 * d)])
    misc = _matmul(h, w_misc, jnp.float32, tn=LANES, name="proj_misc")

    q_a, q_idx = _q_proj(proj, g_cq, w_uq.astype(bf), w_uq_idx.astype(bf), g_q_a, tabs)
    k_a = _ka_norm_rope(proj, col_ka, g_k_a, cf, sf)
    k_idx, wf, w_t, f_row = _misc_post(misc, g_kidx, b_kidx, b_forget, tabs, seq)
    bias = _indexer_mask(q_idx, k_idx, w_t, batch, seq, top_k)
    o_a = _dsa_attention(q_a, k_a, proj, col_va, bias, batch, seq)

    o_b = _fox_attention(proj, col_qb, g_q_b * Q_SCALE, g_k_b, wf, f_row, batch, seq)

    merged = _merge(o_a, o_b, w_up_a, w_up_b, proj, col_ga, col_gb)
    x, xg, ssq = _matmul_residual_norm(merged, w_o, x, g_ffn, name="out_proj")

    u = _swiglu(xg, ssq, w_ffn_gate, w_ffn_up, tn=FFN_TILE)
    x, xg, ssq = _matmul_residual_norm(u, w_ffn_down.astype(bf), x, g_ple, tm=512, name="ffn_down")

    return _ple(xg, ssq, w_ple_gate, p.astype(bf), w_ple, x)


def kernel(x, p, positions, g_attn, w_in, g_cq, w_uq, w_uq_idx, g_kidx, b_kidx, g_q_a, g_k_a, b_forget, g_q_b, g_k_b,
           w_up_a, w_up_b, w_o, g_ffn, w_ffn_gate, w_ffn_up, w_ffn_down, g_ple, w_ple, w_ple_gate):
    batch, seq, d = x.shape
    depth = w_in.shape[0]
    xf = x.reshape(batch * seq, d)
    pos_col = positions.reshape(batch * seq, 1)
    for i in range(depth):
        xf = _layer(xf, p[i].reshape(batch * seq, -1), pos_col, batch, seq, g_attn[i], w_in[i], g_cq[i], w_uq[i],
                    w_uq_idx[i], g_kidx[i], b_kidx[i], g_q_a[i], g_k_a[i], b_forget[i], g_q_b[i], g_k_b[i],
                    w_up_a[i], w_up_b[i], w_o[i], g_ffn[i], w_ffn_gate[i], w_ffn_up[i], w_ffn_down[i], g_ple[i],
                    w_ple[i], w_ple_gate[i])
    return xf.reshape(batch, seq, d)
```

```python
import functools

import numpy as np
import jax
import jax.numpy as jnp
from jax import lax
from jax.experimental import pallas as pl
from jax.experimental.pallas import tpu as pltpu

CHUNK = 64
CHUNK_SHIFT = CHUNK.bit_length() - 1
HEAD_DIM = 128
ROPE_THETA = 10000.0
EPS = 1e-6
N_HEADS_A = 16
N_KV_A = 2
Q_RANK = 1024
N_IDX_HEADS = 32
IDX_DIM = 64
IDX_ROPE_DIM = 32
INDEX_TOPK = 256
N_HEADS_B = 16

LANES = 128
V7X_VMEM_BYTES = 64 * 1024 * 1024
KV_A_W = N_KV_A * HEAD_DIM
W_A = N_HEADS_A * HEAD_DIM
W_B = N_HEADS_B * HEAD_DIM
GROUP_A = N_HEADS_A // N_KV_A

MISC_W_LANE = IDX_DIM
MISC_F_LANE = IDX_DIM + N_IDX_HEADS

MASK_NEG = -1e30
SCORE_NEG = -3e38
SELECT_ITERS = 32
LOG2E = 1.4426950408889634
Q_SCALE = HEAD_DIM ** -0.5 * LOG2E
KV_TILE = 512
FFN_TILE = 256
VMEM_LIMIT = V7X_VMEM_BYTES - V7X_VMEM_BYTES // 8


def _cparams(sem):
    return pltpu.CompilerParams(dimension_semantics=sem, vmem_limit_bytes=VMEM_LIMIT)


def _sigmoid(x):
    return 1.0 / (1.0 + jnp.exp(-x))


def _nt_dot(a, b):
    return lax.dot_general(a, b, (((1,), (1,)), ((), ())), preferred_element_type=jnp.float32)


def _wdot(a, w_ref):
    return jnp.dot(a, w_ref[...].astype(a.dtype), preferred_element_type=jnp.float32)


def _proj_regroup_kernel(a_ref, wt_ref, o_ref):
    a = a_ref[...]
    o_ref[...] = _nt_dot(a, wt_ref[...].astype(a.dtype)).astype(o_ref.dtype)


def _project_regrouped(a, w_t, segments, tm=1024, tn=512):
    t, k = a.shape
    lags, dst = [], 0
    for src, width in segments:
        assert width % tn == 0 and src % 16 == 0 and src + width <= w_t.shape[0]
        lags.append((dst // tn, src - dst))
        dst += width
    tm = min(tm, t)

    def src_row(j):
        lag = lags[0][1]
        for first_tile, seg_lag in lags[1:]:
            lag = jnp.where(j >= first_tile, seg_lag, lag)
        return pl.multiple_of(j * tn + lag, 16)

    return pl.pallas_call(
        _proj_regroup_kernel,
        grid=(t // tm, dst // tn),
        in_specs=[
            pl.BlockSpec((tm, k), lambda i, j: (i, 0)),
            pl.BlockSpec((pl.Element(tn), pl.Element(k)), lambda i, j: (src_row(j), 0)),
        ],
        out_specs=pl.BlockSpec((tm, tn), lambda i, j: (i, j)),
        out_shape=jax.ShapeDtypeStruct((t, dst), jnp.bfloat16),
        compiler_params=_cparams(("parallel", "parallel")),
        name="proj_main",
    )(a, w_t)


def _stage_misc_kernel(a_ref, b_ref, o_ref):
    a, b = a_ref[...], b_ref[...]
    pad = jnp.zeros((LANES - a.shape[0] - b.shape[0], a.shape[1]), a.dtype)
    o_ref[...] = jnp.concatenate([a, b, pad], axis=0).T.astype(o_ref.dtype)


def _stage_misc(w_t, start_a, n_a, start_b, n_b, tk=1024):
    k = w_t.shape[1]
    tk = min(tk, k)
    col = lambda kk: pl.multiple_of(kk * tk, tk)
    return pl.pallas_call(
        _stage_misc_kernel,
        grid=(k // tk,),
        in_specs=[
            pl.BlockSpec((pl.Element(n_a), pl.Element(tk)), lambda kk: (start_a, col(kk))),
            pl.BlockSpec((pl.Element(n_b), pl.Element(tk)), lambda kk: (start_b, col(kk))),
        ],
        out_specs=pl.BlockSpec((tk, LANES), lambda kk: (kk, 0)),
        out_shape=jax.ShapeDtypeStruct((k, LANES), jnp.bfloat16),
        compiler_params=_cparams(("parallel",)),
        name="stage_misc",
    )(w_t, w_t)


def _rmsnorm_kernel(x_ref, g_ref, o_ref):
    x = x_ref[...]
    ms = jnp.mean(x * x, axis=-1, keepdims=True)
    o_ref[...] = (x * lax.rsqrt(ms + EPS) * g_ref[...]).astype(o_ref.dtype)


def _rmsnorm(x, g, tm=512):
    t, d = x.shape
    tm = min(tm, t)
    return pl.pallas_call(
        _rmsnorm_kernel,
        grid=(t // tm,),
        in_specs=[pl.BlockSpec((tm, d), lambda i: (i, 0)), pl.BlockSpec((1, d), lambda i: (0, 0))],
        out_specs=pl.BlockSpec((tm, d), lambda i: (i, 0)),
        out_shape=jax.ShapeDtypeStruct((t, d), jnp.bfloat16),
        compiler_params=_cparams(("parallel",)),
        name="rmsnorm",
    )(x, g.reshape(1, d))


def _mm_kernel(a_ref, w_ref, o_ref):
    o_ref[...] = jnp.dot(a_ref[...], w_ref[...], preferred_element_type=jnp.float32).astype(o_ref.dtype)


def _matmul(a, w, out_dtype, tm=1024, tn=512, name="matmul"):
    t, k = a.shape
    n = w.shape[1]
    tm, tn = min(tm, t), min(tn, n)
    return pl.pallas_call(
        _mm_kernel,
        grid=(t // tm, n // tn),
        in_specs=[pl.BlockSpec((tm, k), lambda i, j: (i, 0)), pl.BlockSpec((k, tn), lambda i, j: (0, j))],
        out_specs=pl.BlockSpec((tm, tn), lambda i, j: (i, j)),
        out_shape=jax.ShapeDtypeStruct((t, n), out_dtype),
        compiler_params=_cparams(("parallel", "parallel")),
        name=name,
    )(a, w)


def _mm_res_norm_kernel(a_ref, w_ref, r_ref, g_ref, o_ref, xg_ref, ssq_ref):
    y = r_ref[...] + _wdot(a_ref[...], w_ref)
    o_ref[...] = y
    xg_ref[...] = (y * g_ref[...]).astype(xg_ref.dtype)
    part = jnp.broadcast_to(jnp.sum(y * y, axis=-1, keepdims=True), ssq_ref.shape)

    @pl.when(pl.program_id(1) == 0)
    def _():
        ssq_ref[...] = part

    @pl.when(pl.program_id(1) > 0)
    def _():
        ssq_ref[...] += part


def _matmul_residual_norm(a, w, r, g, tm=1024, tn=512, name="matmul_residual_norm"):
    t, k = a.shape
    n = w.shape[1]
    tm, tn = min(tm, t), min(tn, n)
    tile = pl.BlockSpec((tm, tn), lambda i, j: (i, j))
    return pl.pallas_call(
        _mm_res_norm_kernel,
        grid=(t // tm, n // tn),
        in_specs=[
            pl.BlockSpec((tm, k), lambda i, j: (i, 0)),
            pl.BlockSpec((k, tn), lambda i, j: (0, j)),
            tile,
            pl.BlockSpec((1, tn), lambda i, j: (0, j)),
        ],
        out_specs=[tile, tile, pl.BlockSpec((tm, LANES), lambda i, j: (i, 0))],
        out_shape=[
            jax.ShapeDtypeStruct((t, n), jnp.float32),
            jax.ShapeDtypeStruct((t, n), jnp.bfloat16),
            jax.ShapeDtypeStruct((t, LANES), jnp.float32),
        ],
        compiler_params=_cparams(("parallel", "arbitrary")),
        name=name,
    )(a, w, r, g.reshape(1, n))


def _row_rstd(ssq_ref, width):
    return lax.rsqrt(ssq_ref[...] * (1.0 / width) + EPS)


def _scale_rows(acc, rstd):
    return jnp.concatenate([acc[:, c * LANES:(c + 1) * LANES] * rstd for c in range(acc.shape[1] // LANES)], axis=1)


def _swiglu_kernel(a_ref, ssq_ref, wg_ref, wu_ref, o_ref):
    a = a_ref[...]
    rstd = _row_rstd(ssq_ref, a.shape[1])
    g = _scale_rows(_wdot(a, wg_ref), rstd)
    u = _scale_rows(_wdot(a, wu_ref), rstd)
    o_ref[...] = (g * _sigmoid(g) * u).astype(o_ref.dtype)


def _swiglu(xg, ssq, wg, wu, tm=1024, tn=512):
    t, k = xg.shape
    n = wg.shape[1]
    tm, tn = min(tm, t), min(tn, n)
    return pl.pallas_call(
        _swiglu_kernel,
        grid=(t // tm, n // tn),
        in_specs=[
            pl.BlockSpec((tm, k), lambda i, j: (i, 0)),
            pl.BlockSpec((tm, LANES), lambda i, j: (i, 0)),
            pl.BlockSpec((k, tn), lambda i, j: (0, j)),
            pl.BlockSpec((k, tn), lambda i, j: (0, j)),
        ],
        out_specs=pl.BlockSpec((tm, tn), lambda i, j: (i, j)),
        out_shape=jax.ShapeDtypeStruct((t, n), jnp.bfloat16),
        compiler_params=_cparams(("parallel", "parallel")),
        name="swiglu",
    )(xg, ssq, wg, wu)


def _merge_kernel(oa_ref, ob_ref, wa_ref, wb_ref, ga_ref, gb_ref, o_ref):
    a = _wdot(oa_ref[...], wa_ref)
    b = _wdot(ob_ref[...], wb_ref)
    ga = _sigmoid(ga_ref[...].astype(jnp.float32))
    gb = _sigmoid(gb_ref[...].astype(jnp.float32))
    o_ref[...] = (ga * a + gb * b).astype(o_ref.dtype)


def _merge(o_a, o_b, w_up_a, w_up_b, proj, ga_col, gb_col, tm=1024, tn=512):
    t, ka = o_a.shape
    kb = o_b.shape[1]
    n = w_up_a.shape[1]
    tm, tn = min(tm, t), min(tn, n)
    ga_blk, gb_blk = ga_col // tn, gb_col // tn
    return pl.pallas_call(
        _merge_kernel,
        grid=(t // tm, n // tn),
        in_specs=[
            pl.BlockSpec((tm, ka), lambda i, j: (i, 0)),
            pl.BlockSpec((tm, kb), lambda i, j: (i, 0)),
            pl.BlockSpec((ka, tn), lambda i, j: (0, j)),
            pl.BlockSpec((kb, tn), lambda i, j: (0, j)),
            pl.BlockSpec((tm, tn), lambda i, j: (i, ga_blk + j)),
            pl.BlockSpec((tm, tn), lambda i, j: (i, gb_blk + j)),
        ],
        out_specs=pl.BlockSpec((tm, tn), lambda i, j: (i, j)),
        out_shape=jax.ShapeDtypeStruct((t, n), jnp.bfloat16),
        compiler_params=_cparams(("parallel", "parallel")),
        name="merge",
    )(o_a, o_b, w_up_a, w_up_b, proj, proj)


def _ple_kernel(xg_ref, ssq_ref, wg_ref, p_ref, wp_ref, r_ref, o_ref):
    a = xg_ref[...]
    g = _scale_rows(_wdot(a, wg_ref), _row_rstd(ssq_ref, a.shape[1]))
    e = _wdot(p_ref[...], wp_ref)
    o_ref[...] = r_ref[...] + _sigmoid(g) * e


def _ple(xg, ssq, w_gate, p, w_ple, r, tm=1024, tn=512):
    t, k = xg.shape
    kp = p.shape[1]
    n = w_gate.shape[1]
    tm, tn = min(tm, t), min(tn, n)
    return pl.pallas_call(
        _ple_kernel,
        grid=(t // tm, n // tn),
        in_specs=[
            pl.BlockSpec((tm, k), lambda i, j: (i, 0)),
            pl.BlockSpec((tm, LANES), lambda i, j: (i, 0)),
            pl.BlockSpec((k, tn), lambda i, j: (0, j)),
            pl.BlockSpec((tm, kp), lambda i, j: (i, 0)),
            pl.BlockSpec((kp, tn), lambda i, j: (0, j)),
            pl.BlockSpec((tm, tn), lambda i, j: (i, j)),
        ],
        out_specs=pl.BlockSpec((tm, tn), lambda i, j: (i, j)),
        out_shape=jax.ShapeDtypeStruct((t, n), jnp.float32),
        compiler_params=_cparams(("parallel", "parallel")),
        name="ple",
    )(xg, ssq, w_gate, p, w_ple, r)


def _rope_consts():
    half = HEAD_DIM // 2
    inv_full = jnp.power(ROPE_THETA, -jnp.arange(half, dtype=jnp.float32) * (2.0 / HEAD_DIM))
    half_i = IDX_ROPE_DIM // 2
    inv_idx = jnp.power(ROPE_THETA, -jnp.arange(half_i, dtype=jnp.float32) * (2.0 / IDX_ROPE_DIM))
    zeros = jnp.zeros((IDX_DIM - IDX_ROPE_DIM,), jnp.float32)
    sign = np.concatenate([-np.ones(half, np.float32), np.ones(half, np.float32)])
    mask_a = np.zeros(IDX_DIM, np.float32)
    mask_a[:half_i] = -1.0
    mask_b = np.zeros(IDX_DIM, np.float32)
    mask_b[half_i:IDX_ROPE_DIM] = 1.0
    reps = LANES // IDX_DIM
    rows = [
        jnp.concatenate([inv_full, inv_full]),
        jnp.asarray(sign),
        jnp.tile(jnp.concatenate([inv_idx, inv_idx, zeros]), reps),
        jnp.asarray(np.tile(mask_a, reps)),
        jnp.asarray(np.tile(mask_b, reps)),
    ]
    rows += [jnp.zeros((LANES,), jnp.float32)] * 3
    return jnp.stack(rows)


def _rope_tables_kernel(pos_ref, c_ref, cf_ref, sf_ref, ci_ref, sa_ref, sb_ref):
    pos = pos_ref[...].astype(jnp.float32)
    ang = pos * c_ref[0:1, :]
    cf_ref[...] = jnp.cos(ang)
    sf_ref[...] = jnp.sin(ang) * c_ref[1:2, :]
    ang_i = pos * c_ref[2:3, :]
    ci_ref[...] = jnp.cos(ang_i)
    s_i = jnp.sin(ang_i)
    sa_ref[...] = s_i * c_ref[3:4, :]
    sb_ref[...] = s_i * c_ref[4:5, :]


def _rope_tables(pos_col, tm=1024):
    t = pos_col.shape[0]
    tm = min(tm, t)
    tab = jax.ShapeDtypeStruct((t, LANES), jnp.float32)
    spec = pl.BlockSpec((tm, LANES), lambda i: (i, 0))
    return pl.pallas_call(
        _rope_tables_kernel,
        grid=(t // tm,),
        in_specs=[pl.BlockSpec((tm, 1), lambda i: (i, 0)), pl.BlockSpec((8, LANES), lambda i: (0, 0))],
        out_specs=[spec] * 5,
        out_shape=[tab] * 5,
        compiler_params=_cparams(("parallel",)),
        name="rope_tables",
    )(pos_col, _rope_consts())


def _rope_full(x, cf, sf):
    return x * cf + pltpu.roll(x, HEAD_DIM // 2, axis=1) * sf


def _rope_idx(x, ci, sa, sb):
    half = IDX_ROPE_DIM // 2
    return x * ci + pltpu.roll(x, LANES - half, axis=1) * sa + pltpu.roll(x, half, axis=1) * sb


def _head_rms(x, g):
    ms = jnp.mean(x * x, axis=-1, keepdims=True)
    return x * lax.rsqrt(ms + EPS) * g


def _q_proj_kernel(cq_ref, gcq_ref, wa_ref, wi_ref, gqa_ref, cf_ref, sf_ref, ci_ref, sa_ref, sb_ref, qa_ref, qi_ref):
    cq = cq_ref[...].astype(jnp.float32)
    ms = jnp.mean(cq * cq, axis=-1, keepdims=True)
    cq = (cq * lax.rsqrt(ms + EPS) * gcq_ref[...]).astype(jnp.bfloat16)
    cf, sf = cf_ref[...], sf_ref[...]
    gqa = gqa_ref[...]
    for j in range(N_HEADS_A // 2):
        acc = jnp.dot(cq, wa_ref[:, j * 2 * HEAD_DIM:(j + 1) * 2 * HEAD_DIM], preferred_element_type=jnp.float32)
        for e in range(2):
            hq = _head_rms(acc[:, e * HEAD_DIM:(e + 1) * HEAD_DIM], gqa)
            qa_ref[2 * j + e] = (_rope_full(hq, cf, sf) * Q_SCALE).astype(qa_ref.dtype)
    ci, sa, sb = ci_ref[...], sa_ref[...], sb_ref[...]
    low_half = lax.broadcasted_iota(jnp.int32, (cq.shape[0], LANES), 1) < IDX_DIM
    per_dot = 2 * LANES // IDX_DIM
    for j in range(N_IDX_HEADS // per_dot):
        acc = jnp.dot(cq, wi_ref[:, j * 2 * LANES:(j + 1) * 2 * LANES], preferred_element_type=jnp.float32)
        for c in range(2):
            pair = _rope_idx(acc[:, c * LANES:(c + 1) * LANES], ci, sa, sb)
            for e, head in enumerate((pair, pltpu.roll(pair, IDX_DIM, axis=1))):
                qi_ref[per_dot * j + 2 * c + e] = jnp.where(low_half, head, 0.0).astype(qi_ref.dtype)


def _q_proj(proj, g_cq, w_uq, w_uq_idx, g_q_a, tabs, tm=512):
    t = proj.shape[0]
    tm = min(tm, t)
    cf, sf, ci, sa, sb = tabs
    tab_spec = pl.BlockSpec((tm, LANES), lambda i: (i, 0))
    return pl.pallas_call(
        _q_proj_kernel,
        grid=(t // tm,),
        in_specs=[
            pl.BlockSpec((tm, Q_RANK), lambda i: (i, 0)),
            pl.BlockSpec((1, Q_RANK), lambda i: (0, 0)),
            pl.BlockSpec((Q_RANK, W_A), lambda i: (0, 0)),
            pl.BlockSpec((Q_RANK, N_IDX_HEADS * IDX_DIM), lambda i: (0, 0)),
            pl.BlockSpec((1, HEAD_DIM), lambda i: (0, 0)),
            tab_spec, tab_spec, tab_spec, tab_spec, tab_spec,
        ],
        out_specs=[
            pl.BlockSpec((N_HEADS_A, tm, HEAD_DIM), lambda i: (0, i, 0)),
            pl.BlockSpec((N_IDX_HEADS, tm, LANES), lambda i: (0, i, 0)),
        ],
        out_shape=[
            jax.ShapeDtypeStruct((N_HEADS_A, t, HEAD_DIM), jnp.bfloat16),
            jax.ShapeDtypeStruct((N_IDX_HEADS, t, LANES), jnp.bfloat16),
        ],
        compiler_params=_cparams(("parallel",)),
        name="q_proj",
    )(proj, g_cq.reshape(1, Q_RANK), w_uq, w_uq_idx, g_q_a.reshape(1, HEAD_DIM), cf, sf, ci, sa, sb)


def _ka_kernel(x_ref, g_ref, cf_ref, sf_ref, o_ref):
    x = x_ref[...].astype(jnp.float32)
    g, cf, sf = g_ref[...], cf_ref[...], sf_ref[...]
    for h in range(N_KV_A):
        hk = _head_rms(x[:, h * HEAD_DIM:(h + 1) * HEAD_DIM], g)
        o_ref[:, h * HEAD_DIM:(h + 1) * HEAD_DIM] = _rope_full(hk, cf, sf).astype(o_ref.dtype)


def _ka_norm_rope(proj, col, g_k_a, cf, sf, tm=512):
    t = proj.shape[0]
    tm = min(tm, t)
    tab_spec = pl.BlockSpec((tm, LANES), lambda i: (i, 0))
    return pl.pallas_call(
        _ka_kernel,
        grid=(t // tm,),
        in_specs=[
            pl.BlockSpec((tm, KV_A_W), lambda i: (i, col // KV_A_W)),
            pl.BlockSpec((1, HEAD_DIM), lambda i: (0, 0)),
            tab_spec, tab_spec,
        ],
        out_specs=pl.BlockSpec((tm, KV_A_W), lambda i: (i, 0)),
        out_shape=jax.ShapeDtypeStruct((t, KV_A_W), jnp.bfloat16),
        compiler_params=_cparams(("parallel",)),
        name="ka_norm_rope",
    )(proj, g_k_a.reshape(1, HEAD_DIM), cf, sf)


def _split3_bf16(x):
    hi = x.astype(jnp.bfloat16)
    r1 = x - hi.astype(jnp.float32)
    mid = r1.astype(jnp.bfloat16)
    lo = (r1 - mid.astype(jnp.float32)).astype(jnp.bfloat16)
    return hi, mid, lo


def _misc_kernel(m_ref, c_ref, ci_ref, sa_ref, sb_ref, tri_ref, kidx_ref, wf_ref, wt_ref, frow_ref, carry_ref, *,
                 tiles_per_seq):
    i = pl.program_id(0)

    @pl.when(i % tiles_per_seq == 0)
    def _():
        carry_ref[...] = jnp.zeros_like(carry_ref)

    x = m_ref[...]
    lane = lax.broadcasted_iota(jnp.int32, x.shape, 1)
    is_k = lane < IDX_DIM
    xk = jnp.where(is_k, x, 0.0)
    mu = jnp.sum(xk, axis=-1, keepdims=True) * (1.0 / IDX_DIM)
    dk = jnp.where(is_k, x - mu, 0.0)
    var = jnp.sum(dk * dk, axis=-1, keepdims=True) * (1.0 / IDX_DIM)
    y = dk * lax.rsqrt(var + EPS) * c_ref[0:1, :] + c_ref[1:2, :]
    y = _rope_idx(y, ci_ref[...], sa_ref[...], sb_ref[...])
    kidx_ref[...] = jnp.where(is_k, y, 0.0).astype(kidx_ref.dtype)
    f = x + c_ref[2:3, :]
    log_f = jnp.minimum(f, 0.0) - jnp.log1p(jnp.exp(-jnp.abs(f)))
    hi, mid, lo = _split3_bf16(log_f)
    tri = tri_ref[...]
    csum = (jnp.dot(tri, hi, preferred_element_type=jnp.float32)
            + jnp.dot(tri, mid, preferred_element_type=jnp.float32)
            + jnp.dot(tri, lo, preferred_element_type=jnp.float32))
    csum = csum + carry_ref[0:1, :]
    carry_ref[...] = jnp.broadcast_to(csum[-1:, :], carry_ref.shape)
    is_w = (lane >= MISC_W_LANE) & (lane < MISC_F_LANE)
    wf = jnp.where(is_w, x * c_ref[3:4, :], csum * LOG2E)
    wf_ref[...] = wf
    wf_t = wf.T
    wt_ref[...] = wf_t[MISC_W_LANE:MISC_F_LANE, :]
    frow_ref[0] = wf_t[MISC_F_LANE:MISC_F_LANE + N_HEADS_B, :]


def _misc_post(misc, g_kidx, b_kidx, b_forget, tabs, seq):
    t = misc.shape[0]
    tm = min(KV_TILE, seq)
    idx_w_scale = (N_IDX_HEADS ** -0.5) * (IDX_DIM ** -0.5)
    pad = lambda v, off: jnp.zeros((LANES,), jnp.float32).at[off:off + v.shape[0]].set(v)
    consts = jnp.stack([
        pad(g_kidx, 0), pad(b_kidx, 0), pad(b_forget, MISC_F_LANE),
        pad(jnp.full((N_IDX_HEADS,), idx_w_scale, jnp.float32), MISC_W_LANE),
    ] + [jnp.zeros((LANES,), jnp.float32)] * 4)
    tri = jnp.asarray(np.tril(np.ones((tm, tm), np.float32)), jnp.bfloat16)
    _, _, ci, sa, sb = tabs
    tab_spec = pl.BlockSpec((tm, LANES), lambda i: (i, 0))
    return pl.pallas_call(
        functools.partial(_misc_kernel, tiles_per_seq=seq // tm),
        grid=(t // tm,),
        in_specs=[
            tab_spec,
            pl.BlockSpec((8, LANES), lambda i: (0, 0)),
            tab_spec, tab_spec, tab_spec,
            pl.BlockSpec((tm, tm), lambda i: (0, 0)),
        ],
        out_specs=[
            tab_spec, tab_spec,
            pl.BlockSpec((N_IDX_HEADS, tm), lambda i: (0, i)),
            pl.BlockSpec((1, N_HEADS_B, tm), lambda i: (i, 0, 0)),
        ],
        out_shape=[
            jax.ShapeDtypeStruct((t, LANES), jnp.bfloat16),
            jax.ShapeDtypeStruct((t, LANES), jnp.float32),
            jax.ShapeDtypeStruct((N_IDX_HEADS, t), jnp.float32),
            jax.ShapeDtypeStruct((t // tm, N_HEADS_B, tm), jnp.float32),
        ],
        scratch_shapes=[pltpu.VMEM((8, LANES), jnp.float32)],
        compiler_params=_cparams(("arbitrary",)),
        name="misc_post",
    )(misc, consts, ci, sa, sb, tri)


IDX_HEAD_GROUP = 8
IDX_SLAB = 128
COUNT_ROWS = 16
HALVINGS_PER_CHECK = 4


def _indexer_kernel(q_ref, k_ref, w_ref, o_ref, sc_ref, *, tq, tk, n_kt, top_k):
    qi = pl.program_id(1)
    t0 = qi * tq
    n_vis = (t0 + tq + tk - 1) // tk
    t_row = lax.broadcasted_iota(jnp.int32, (1, tq), 1) + t0
    vis_end = (jnp.right_shift(t_row, CHUNK_SHIFT) + 1) * CHUNK

    def score_tile(c, carry):
        mx, mn = carry
        kt = k_ref[pl.ds(pl.multiple_of(c * tk, tk), tk), :]
        for g in range(N_IDX_HEADS // IDX_HEAD_GROUP):
            qg = q_ref[g * IDX_HEAD_GROUP:(g + 1) * IDX_HEAD_GROUP].reshape(IDX_HEAD_GROUP * tq, LANES)
            st = _nt_dot(kt, qg)
            for r in range(tk // IDX_SLAB):
                rows = slice(r * IDX_SLAB, (r + 1) * IDX_SLAB)
                part = jnp.zeros((IDX_SLAB, tq), jnp.float32)
                for j in range(IDX_HEAD_GROUP):
                    h = g * IDX_HEAD_GROUP + j
                    part = part + jnp.maximum(st[rows, j * tq:(j + 1) * tq], 0.0) * w_ref[h:h + 1, :]
                if g == 0:
                    sc_ref[c, rows, :] = part
                else:
                    sc_ref[c, rows, :] += part
        s_col = lax.broadcasted_iota(jnp.int32, (tk, 1), 0) + c * tk
        adm = s_col < vis_end
        sc = sc_ref[c]
        sc_ref[c] = jnp.where(adm, sc, SCORE_NEG)
        mx = jnp.maximum(mx, jnp.max(jnp.where(adm, sc, SCORE_NEG), axis=0, keepdims=True))
        mn = jnp.minimum(mn, jnp.min(jnp.where(adm, sc, -SCORE_NEG), axis=0, keepdims=True))
        return mx, mn

    hi0, lo0 = lax.fori_loop(
        0, n_vis, score_tile,
        (jnp.full((1, tq), SCORE_NEG, jnp.float32), jnp.full((1, tq), -SCORE_NEG, jnp.float32)))

    def unsettled(state):
        it, _, _, active = state
        return jnp.logical_and(it < SELECT_ITERS, jnp.max(active) > 0.0)

    def halve(lo, hi, active):
        mid = 0.5 * (lo + hi)

        def count_tile(c, cnt):
            ge = jnp.where(sc_ref[c] >= mid, 1.0, 0.0)
            return cnt + jnp.sum(ge.reshape(tk // COUNT_ROWS, COUNT_ROWS, tq), axis=0)

        cnt = lax.fori_loop(0, n_vis, count_tile, jnp.zeros((COUNT_ROWS, tq), jnp.float32))
        cnt = jnp.sum(cnt, axis=0, keepdims=True)
        enough = cnt >= float(top_k)
        active = jnp.where(cnt == float(top_k), 0.0, active)
        return jnp.where(enough, mid, lo), jnp.where(enough, hi, mid), active

    def bisect(state):
        it, lo, hi, active = state
        for _ in range(HALVINGS_PER_CHECK):
            lo, hi, active = halve(lo, hi, active)
        return it + HALVINGS_PER_CHECK, lo, hi, active

    active0 = jnp.where(vis_end > top_k, 1.0, 0.0)
    _, thr, _, _ = lax.while_loop(unsettled, bisect, (jnp.int32(0), lo0, hi0, active0))

    def write_tile(c, carry):
        sel = jnp.where(sc_ref[c] >= thr, 0.0, MASK_NEG)
        o_ref[0, c] = sel.T.astype(o_ref.dtype)
        return carry

    lax.fori_loop(0, n_vis, write_tile, 0)

    def fill_tile(c, carry):
        o_ref[0, c] = jnp.full((tq, tk), MASK_NEG, o_ref.dtype)
        return carry

    lax.fori_loop(n_vis, n_kt, fill_tile, 0)


def _indexer_mask(q_idx, k_idx, w_t, batch, seq, top_k, tq=256):
    tq, tk = min(tq, seq), min(KV_TILE, seq)
    n_qt, n_kt = seq // tq, seq // tk
    return pl.pallas_call(
        functools.partial(_indexer_kernel, tq=tq, tk=tk, n_kt=n_kt, top_k=top_k),
        grid=(batch, n_qt),
        in_specs=[
            pl.BlockSpec((N_IDX_HEADS, tq, LANES), lambda b, i: (0, b * n_qt + i, 0)),
            pl.BlockSpec((seq, LANES), lambda b, i: (b, 0)),
            pl.BlockSpec((N_IDX_HEADS, tq), lambda b, i: (0, b * n_qt + i)),
        ],
        out_specs=pl.BlockSpec((1, n_kt, tq, tk), lambda b, i: (b, 0, i, 0)),
        out_shape=jax.ShapeDtypeStruct((batch, n_kt, seq, tk), jnp.bfloat16),
        scratch_shapes=[pltpu.VMEM((n_kt, tk, tq), jnp.float32)],
        compiler_params=_cparams(("parallel", "parallel")),
        name="indexer_mask",
    )(q_idx, k_idx, w_t)


def _softmax_update(s, row_shift, v, m_ref, l_ref, acc_ref, e):
    rows, tk = s.shape
    chunks = [s[:, c * LANES:(c + 1) * LANES] for c in range(tk // LANES)]
    mx = functools.reduce(jnp.maximum, chunks)
    mx = jnp.broadcast_to(jnp.max(mx, axis=-1, keepdims=True), (rows, LANES))
    m_prev = m_ref[e]
    m_new = jnp.maximum(m_prev, mx + row_shift)
    alpha = jnp.exp2(m_prev - m_new)
    r = m_new - row_shift
    p = jnp.concatenate([jnp.exp2(c - r) for c in chunks], axis=1).astype(v.dtype)
    v_ones = jnp.concatenate([v, jnp.ones((tk, LANES), v.dtype)], axis=1)
    pv = jnp.dot(p, v_ones, preferred_element_type=jnp.float32)
    acc_ref[e] = alpha * acc_ref[e] + pv[:, :HEAD_DIM]
    l_ref[e] = alpha * l_ref[e] + pv[:, HEAD_DIM:]
    m_ref[e] = m_new


def _softmax_init(m_ref, l_ref, acc_ref):
    m_ref[...] = jnp.full(m_ref.shape, MASK_NEG, jnp.float32)
    l_ref[...] = jnp.zeros(l_ref.shape, jnp.float32)
    acc_ref[...] = jnp.zeros(acc_ref.shape, jnp.float32)


def _dsa_attn_kernel(q_ref, k_ref, v_ref, b_ref, o_ref, m_ref, l_ref, acc_ref, *, tq, tk):
    qi = pl.program_id(1)
    n_need = ((qi + 1) * tq + tk - 1) // tk
    rows = GROUP_A * tq
    _softmax_init(m_ref, l_ref, acc_ref)

    def kv_step(j, carry):
        off = pl.multiple_of(j * tk, tk)
        bias = b_ref[0, j].astype(jnp.float32)[None]
        for g in range(N_KV_A):
            cols = slice(g * HEAD_DIM, (g + 1) * HEAD_DIM)
            q = q_ref[g * GROUP_A:(g + 1) * GROUP_A].reshape(rows, HEAD_DIM)
            s = _nt_dot(q, k_ref[pl.ds(off, tk), cols])
            s = (s.reshape(GROUP_A, tq, tk) + bias).reshape(rows, tk)
            _softmax_update(s, 0.0, v_ref[pl.ds(off, tk), cols], m_ref, l_ref, acc_ref, g)
        return carry

    lax.fori_loop(0, n_need, kv_step, 0)

    for g in range(N_KV_A):
        out = acc_ref[g] / l_ref[g]
        for h in range(GROUP_A):
            c0 = (g * GROUP_A + h) * HEAD_DIM
            o_ref[:, c0:c0 + HEAD_DIM] = out[h * tq:(h + 1) * tq].astype(o_ref.dtype)


def _dsa_attention(q_a, k_a, proj, v_col, bias, batch, seq, tq=512):
    tk = bias.shape[-1]
    tq = min(tq, seq)
    n_qt, n_kt = seq // tq, seq // tk
    rows = GROUP_A * tq
    return pl.pallas_call(
        functools.partial(_dsa_attn_kernel, tq=tq, tk=tk),
        grid=(batch, n_qt),
        in_specs=[
            pl.BlockSpec((N_HEADS_A, tq, HEAD_DIM), lambda b, i: (0, b * n_qt + i, 0)),
            pl.BlockSpec((seq, KV_A_W), lambda b, i: (b, 0)),
            pl.BlockSpec((seq, KV_A_W), lambda b, i: (b, v_col // KV_A_W)),
            pl.BlockSpec((1, n_kt, tq, tk), lambda b, i: (b, 0, i, 0)),
        ],
        out_specs=pl.BlockSpec((tq, W_A), lambda b, i: (b * n_qt + i, 0)),
        out_shape=jax.ShapeDtypeStruct((batch * seq, W_A), jnp.bfloat16),
        scratch_shapes=[
            pltpu.VMEM((N_KV_A, rows, LANES), jnp.float32),
            pltpu.VMEM((N_KV_A, rows, LANES), jnp.float32),
            pltpu.VMEM((N_KV_A, rows, HEAD_DIM), jnp.float32),
        ],
        compiler_params=_cparams(("parallel", "parallel")),
        name="dsa_attention",
    )(q_a, k_a, proj, bias)


FOX_HEADS_PER_STEP = 4


def _fox_kernel(q_ref, k_ref, v_ref, gq_ref, gk_ref, fq_ref, fk_ref, o_ref, kn_ref, s0_ref, s1_ref, m_ref, l_ref,
                acc_ref, *, tile):
    hp, qi = pl.program_id(1), pl.program_id(2)
    n_t = kn_ref.shape[0] // tile

    @pl.when(qi == 0)
    def _():
        def norm_keys(j, carry):
            off = pl.multiple_of(j * tile, tile)
            for e in range(FOX_HEADS_PER_STEP):
                cols = slice(e * HEAD_DIM, (e + 1) * HEAD_DIM)
                k = k_ref[pl.ds(off, tile), cols].astype(jnp.float32)
                kn_ref[pl.ds(off, tile), cols] = _head_rms(k, gk_ref[...]).astype(kn_ref.dtype)
            return carry

        lax.fori_loop(0, n_t, norm_keys, 0)

    lane = lax.broadcasted_iota(jnp.int32, (tile, LANES), 1)
    heads = [hp * FOX_HEADS_PER_STEP + e for e in range(FOX_HEADS_PER_STEP)]
    fq = [jnp.broadcast_to(jnp.sum(jnp.where(lane == MISC_F_LANE + h, fq_ref[...], 0.0), axis=1, keepdims=True),
                           (tile, LANES)) for h in heads]
    qn = [_head_rms(q_ref[:, e * HEAD_DIM:(e + 1) * HEAD_DIM].astype(jnp.float32), gq_ref[...]).astype(kn_ref.dtype)
          for e in range(FOX_HEADS_PER_STEP)]
    _softmax_init(m_ref, l_ref, acc_ref)

    def logits(j, s_ref):
        off = pl.multiple_of(j * tile, tile)
        for e, h in enumerate(heads):
            cols = slice(e * HEAD_DIM, (e + 1) * HEAD_DIM)
            s_ref[e] = _nt_dot(qn[e], kn_ref[pl.ds(off, tile), cols]) - fk_ref[j, pl.ds(h, 1), :]

    def attend(j, s_ref, diagonal):
        off = pl.multiple_of(j * tile, tile)
        for e in range(FOX_HEADS_PER_STEP):
            cols = slice(e * HEAD_DIM, (e + 1) * HEAD_DIM)
            s = s_ref[e]
            if diagonal:
                t_pos = lax.broadcasted_iota(jnp.int32, (tile, tile), 0)
                s_pos = lax.broadcasted_iota(jnp.int32, (tile, tile), 1)
                s = jnp.where(s_pos <= t_pos, s, MASK_NEG)
            _softmax_update(s, fq[e], v_ref[pl.ds(off, tile), cols], m_ref, l_ref, acc_ref, e)

    def tile_pair(i, carry):
        logits(2 * i + 1, s1_ref)
        attend(2 * i, s0_ref, False)
        logits(2 * i + 2, s0_ref)
        attend(2 * i + 1, s1_ref, False)
        return carry

    logits(0, s0_ref)
    lax.fori_loop(0, qi // 2, tile_pair, 0)

    @pl.when(qi % 2 == 0)
    def _():
        attend(qi, s0_ref, True)

    @pl.when(qi % 2 == 1)
    def _():
        logits(qi, s1_ref)
        attend(qi - 1, s0_ref, False)
        attend(qi, s1_ref, True)

    for e in range(FOX_HEADS_PER_STEP):
        o_ref[:, e * HEAD_DIM:(e + 1) * HEAD_DIM] = (acc_ref[e] / l_ref[e]).astype(o_ref.dtype)


def _fox_attention(proj, q_col, g_q, g_k, wf, f_row, batch, seq):
    tile = min(KV_TILE, seq)
    n_t = seq // tile
    wb = FOX_HEADS_PER_STEP * HEAD_DIM
    q_blk, k_blk, v_blk = q_col // wb, (q_col + W_B) // wb, (q_col + 2 * W_B) // wb
    gain_spec = pl.BlockSpec((1, HEAD_DIM), lambda b, h, i: (0, 0))
    return pl.pallas_call(
        functools.partial(_fox_kernel, tile=tile),
        grid=(batch, N_HEADS_B // FOX_HEADS_PER_STEP, n_t),
        in_specs=[
            pl.BlockSpec((tile, wb), lambda b, h, i: (b * n_t + i, q_blk + h)),
            pl.BlockSpec((seq, wb), lambda b, h, i: (b, k_blk + h)),
            pl.BlockSpec((seq, wb), lambda b, h, i: (b, v_blk + h)),
            gain_spec, gain_spec,
            pl.BlockSpec((tile, LANES), lambda b, h, i: (b * n_t + i, 0)),
            pl.BlockSpec((n_t, N_HEADS_B, tile), lambda b, h, i: (b, 0, 0)),
        ],
        out_specs=pl.BlockSpec((tile, wb), lambda b, h, i: (b * n_t + i, h)),
        out_shape=jax.ShapeDtypeStruct((batch * seq, W_B), jnp.bfloat16),
        scratch_shapes=[
            pltpu.VMEM((seq, wb), jnp.bfloat16),
            pltpu.VMEM((FOX_HEADS_PER_STEP, tile, tile), jnp.float32),
            pltpu.VMEM((FOX_HEADS_PER_STEP, tile, tile), jnp.float32),
            pltpu.VMEM((FOX_HEADS_PER_STEP, tile, LANES), jnp.float32),
            pltpu.VMEM((FOX_HEADS_PER_STEP, tile, LANES), jnp.float32),
            pltpu.VMEM((FOX_HEADS_PER_STEP, tile, HEAD_DIM), jnp.float32),
        ],
        compiler_params=_cparams(("parallel", "parallel", "arbitrary")),
        name="fox_attention",
    )(proj, proj, proj, g_q.reshape(1, HEAD_DIM), g_k.reshape(1, HEAD_DIM), wf, f_row)


def _layer(x, p, pos_col, batch, seq, g_attn, w_in, g_cq, w_uq, w_uq_idx, g_kidx, b_kidx, g_q_a, g_k_a, b_forget,
           g_q_b, g_k_b, w_up_a, w_up_b, w_o, g_ffn, w_ffn_gate, w_ffn_up, w_ffn_down, g_ple, w_ple, w_ple_gate):
    d = x.shape[1]
    bf = jnp.bfloat16
    top_k = min(INDEX_TOPK, seq // 4)

    o_kidx = Q_RANK + 2 * KV_A_W
    o_qb = o_kidx + IDX_DIM + N_IDX_HEADS
    o_fb = o_qb + 3 * W_B
    o_ga = o_fb + N_HEADS_B
    w_in_t = jnp.swapaxes(w_in, 0, 1)
    w_misc = _stage_misc(w_in_t, o_kidx, o_qb - o_kidx, o_fb, o_ga - o_fb)
    col_ka, col_va = Q_RANK, Q_RANK + KV_A_W
    col_qb = Q_RANK + 2 * KV_A_W
    col_vb = col_qb + 2 * W_B
    col_ga = col_vb + W_B
    col_gb = col_ga + d

    tabs = _rope_tables(pos_col)
    cf, sf = tabs[0], tabs[1]

    h = _rmsnorm(x, g_attn)
    proj = _project_regrouped(h, w_in_t, [(0, o_kidx), (o_qb, o_fb - o_qb), (o_ga, 2 * d)])
    misc = _matmul(h, w_misc, jnp.float32, tn=LANES, name="proj_misc")

    q_a, q_idx = _q_proj(proj, g_cq, w_uq.astype(bf), w_uq_idx.astype(bf), g_q_a, tabs)
    k_a = _ka_norm_rope(proj, col_ka, g_k_a, cf, sf)
    k_idx, wf, w_t, f_row = _misc_post(misc, g_kidx, b_kidx, b_forget, tabs, seq)
    bias = _indexer_mask(q_idx, k_idx, w_t, batch, seq, top_k)
    o_a = _dsa_attention(q_a, k_a, proj, col_va, bias, batch, seq)

    o_b = _fox_attention(proj, col_qb, g_q_b * Q_SCALE, g_k_b, wf, f_row, batch, seq)

    merged = _merge(o_a, o_b, w_up_a, w_up_b, proj, col_ga, col_gb)
    x, xg, ssq = _matmul_residual_norm(merged, w_o, x, g_ffn, name="out_proj")

    u = _swiglu(xg, ssq, w_ffn_gate, w_ffn_up, tn=FFN_TILE)
    x, xg, ssq = _matmul_residual_norm(u, w_ffn_down.astype(bf), x, g_ple, tm=512, name="ffn_down")

    return _ple(xg, ssq, w_ple_gate, p.astype(bf), w_ple, x)


def kernel(x, p, positions, g_attn, w_in, g_cq, w_uq, w_uq_idx, g_kidx, b_kidx, g_q_a, g_k_a, b_forget, g_q_b, g_k_b,
           w_up_a, w_up_b, w_o, g_ffn, w_ffn_gate, w_ffn_up, w_ffn_down, g_ple, w_ple, w_ple_gate):
    batch, seq, d = x.shape
    depth = w_in.shape[0]
    xf = x.reshape(batch * seq, d)
    pos_col = positions.reshape(batch * seq, 1)
    for i in range(depth):
        xf = _layer(xf, p[i].reshape(batch * seq, -1), pos_col, batch, seq, g_attn[i], w_in[i], g_cq[i], w_uq[i],
                    w_uq_idx[i], g_kidx[i], b_kidx[i], g_q_a[i], g_k_a[i], b_forget[i], g_q_b[i], g_k_b[i],
                    w_up_a[i], w_up_b[i], w_o[i], g_ffn[i], w_ffn_gate[i], w_ffn_up[i], w_ffn_down[i], g_ple[i],
                    w_ple[i], w_ple_gate[i])
    return xf.reshape(batch, seq, d)
```

```python
import functools

import numpy as np
import jax
import jax.numpy as jnp
from jax import lax
from jax.experimental import pallas as pl
from jax.experimental.pallas import tpu as pltpu

CHUNK = 64
CHUNK_SHIFT = CHUNK.bit_length() - 1
HEAD_DIM = 128
ROPE_THETA = 10000.0
EPS = 1e-6
N_HEADS_A = 16
N_KV_A = 2
Q_RANK = 1024
N_IDX_HEADS = 32
IDX_DIM = 64
IDX_ROPE_DIM = 32
INDEX_TOPK = 256
N_HEADS_B = 16

LANES = 128
V7X_VMEM_BYTES = 64 * 1024 * 1024
KV_A_W = N_KV_A * HEAD_DIM
W_A = N_HEADS_A * HEAD_DIM
W_B = N_HEADS_B * HEAD_DIM
GROUP_A = N_HEADS_A // N_KV_A

MISC_W_LANE = IDX_DIM
MISC_F_LANE = IDX_DIM + N_IDX_HEADS

MASK_NEG = -1e30
SCORE_NEG = -3e38
SELECT_ITERS = 32
LOG2E = 1.4426950408889634
Q_SCALE = HEAD_DIM ** -0.5 * LOG2E
KV_TILE = 512
FFN_TILE = 256
VMEM_LIMIT = V7X_VMEM_BYTES - V7X_VMEM_BYTES // 8


def _cparams(sem):
    return pltpu.CompilerParams(dimension_semantics=sem, vmem_limit_bytes=VMEM_LIMIT)


def _sigmoid(x):
    return 1.0 / (1.0 + jnp.exp(-x))


def _nt_dot(a, b):
    return lax.dot_general(a, b, (((1,), (1,)), ((), ())), preferred_element_type=jnp.float32)


def _wdot(a, w_ref):
    return jnp.dot(a, w_ref[...].astype(a.dtype), preferred_element_type=jnp.float32)


def _proj_regroup_kernel(a_ref, wt_ref, o_ref):
    a = a_ref[...]
    o_ref[...] = _nt_dot(a, wt_ref[...].astype(a.dtype)).astype(o_ref.dtype)


def _project_regrouped(a, w_t, segments, tm=1024, tn=512):
    t, k = a.shape
    lags, dst = [], 0
    for src, width in segments:
        assert width % tn == 0 and src % 16 == 0 and src + width <= w_t.shape[0]
        lags.append((dst // tn, src - dst))
        dst += width
    tm = min(tm, t)

    def src_row(j):
        lag = lags[0][1]
        for first_tile, seg_lag in lags[1:]:
            lag = jnp.where(j >= first_tile, seg_lag, lag)
        return pl.multiple_of(j * tn + lag, 16)

    return pl.pallas_call(
        _proj_regroup_kernel,
        grid=(t // tm, dst // tn),
        in_specs=[
            pl.BlockSpec((tm, k), lambda i, j: (i, 0)),
            pl.BlockSpec((pl.Element(tn), pl.Element(k)), lambda i, j: (src_row(j), 0)),
        ],
        out_specs=pl.BlockSpec((tm, tn), lambda i, j: (i, j)),
        out_shape=jax.ShapeDtypeStruct((t, dst), jnp.bfloat16),
        compiler_params=_cparams(("parallel", "parallel")),
        name="proj_main",
    )(a, w_t)


def _stage_misc_kernel(a_ref, b_ref, o_ref):
    a, b = a_ref[...], b_ref[...]
    pad = jnp.zeros((LANES - a.shape[0] - b.shape[0], a.shape[1]), a.dtype)
    o_ref[...] = jnp.concatenate([a, b, pad], axis=0).T.astype(o_ref.dtype)


def _stage_misc(w_t, start_a, n_a, start_b, n_b, tk=1024):
    k = w_t.shape[1]
    tk = min(tk, k)
    col = lambda kk: pl.multiple_of(kk * tk, tk)
    return pl.pallas_call(
        _stage_misc_kernel,
        grid=(k // tk,),
        in_specs=[
            pl.BlockSpec((pl.Element(n_a), pl.Element(tk)), lambda kk: (start_a, col(kk))),
            pl.BlockSpec((pl.Element(n_b), pl.Element(tk)), lambda kk: (start_b, col(kk))),
        ],
        out_specs=pl.BlockSpec((tk, LANES), lambda kk: (kk, 0)),
        out_shape=jax.ShapeDtypeStruct((k, LANES), jnp.bfloat16),
        compiler_params=_cparams(("parallel",)),
        name="stage_misc",
    )(w_t, w_t)


def _rmsnorm_kernel(x_ref, g_ref, o_ref):
    x = x_ref[...]
    ms = jnp.mean(x * x, axis=-1, keepdims=True)
    o_ref[...] = (x * lax.rsqrt(ms + EPS) * g_ref[...]).astype(o_ref.dtype)


def _rmsnorm(x, g, tm=512):
    t, d = x.shape
    tm = min(tm, t)
    return pl.pallas_call(
        _rmsnorm_kernel,
        grid=(t // tm,),
        in_specs=[pl.BlockSpec((tm, d), lambda i: (i, 0)), pl.BlockSpec((1, d), lambda i: (0, 0))],
        out_specs=pl.BlockSpec((tm, d), lambda i: (i, 0)),
        out_shape=jax.ShapeDtypeStruct((t, d), jnp.bfloat16),
        compiler_params=_cparams(("parallel",)),
        name="rmsnorm",
    )(x, g.reshape(1, d))


def _mm_kernel(a_ref, w_ref, o_ref):
    o_ref[...] = jnp.dot(a_ref[...], w_ref[...], preferred_element_type=jnp.float32).astype(o_ref.dtype)


def _matmul(a, w, out_dtype, tm=1024, tn=512, name="matmul"):
    t, k = a.shape
    n = w.shape[1]
    tm, tn = min(tm, t), min(tn, n)
    return pl.pallas_call(
        _mm_kernel,
        grid=(t // tm, n // tn),
        in_specs=[pl.BlockSpec((tm, k), lambda i, j: (i, 0)), pl.BlockSpec((k, tn), lambda i, j: (0, j))],
        out_specs=pl.BlockSpec((tm, tn), lambda i, j: (i, j)),
        out_shape=jax.ShapeDtypeStruct((t, n), out_dtype),
        compiler_params=_cparams(("parallel", "parallel")),
        name=name,
    )(a, w)


def _mm_res_norm_kernel(a_ref, w_ref, r_ref, g_ref, *refs, scaled_width):
    if scaled_width:
        in_ssq_ref, o_ref, xg_ref, ssq_ref = refs
    else:
        o_ref, xg_ref, ssq_ref = refs
    acc = _wdot(a_ref[...], w_ref)
    if scaled_width:
        acc = _scale_rows(acc, _row_rstd(in_ssq_ref, scaled_width))
    y = r_ref[...] + acc
    o_ref[...] = y
    xg_ref[...] = (y * g_ref[...]).astype(xg_ref.dtype)
    part = jnp.broadcast_to(jnp.sum(y * y, axis=-1, keepdims=True), ssq_ref.shape)

    @pl.when(pl.program_id(1) == 0)
    def _():
        ssq_ref[...] = part

    @pl.when(pl.program_id(1) > 0)
    def _():
        ssq_ref[...] += part


def _matmul_residual_norm(a, w, r, g, a_row_ssq=None, tm=1024, tn=512, name="matmul_residual_norm"):
    t, k = a.shape
    n = w.shape[1]
    tm, tn = min(tm, t), min(tn, n)
    tile = pl.BlockSpec((tm, tn), lambda i, j: (i, j))
    stat = pl.BlockSpec((tm, LANES), lambda i, j: (i, 0))
    scaled = a_row_ssq is not None
    return pl.pallas_call(
        functools.partial(_mm_res_norm_kernel, scaled_width=r.shape[1] if scaled else 0),
        grid=(t // tm, n // tn),
        in_specs=[
            pl.BlockSpec((tm, k), lambda i, j: (i, 0)),
            pl.BlockSpec((k, tn), lambda i, j: (0, j)),
            tile,
            pl.BlockSpec((1, tn), lambda i, j: (0, j)),
        ] + ([stat] if scaled else []),
        out_specs=[tile, tile, stat],
        out_shape=[
            jax.ShapeDtypeStruct((t, n), jnp.float32),
            jax.ShapeDtypeStruct((t, n), jnp.bfloat16),
            jax.ShapeDtypeStruct((t, LANES), jnp.float32),
        ],
        compiler_params=_cparams(("parallel", "arbitrary")),
        name=name,
    )(a, w, r, g.reshape(1, n), *([a_row_ssq] if scaled else []))


def _row_rstd(ssq_ref, width):
    return lax.rsqrt(ssq_ref[...] * (1.0 / width) + EPS)


def _scale_rows(acc, rstd):
    return jnp.concatenate([acc[:, c * LANES:(c + 1) * LANES] * rstd for c in range(acc.shape[1] // LANES)], axis=1)


def _swiglu_kernel(a_ref, ssq_ref, wg_ref, wu_ref, o_ref):
    a = a_ref[...]
    rstd = _row_rstd(ssq_ref, a.shape[1])
    g = _scale_rows(_wdot(a, wg_ref), rstd)
    u = _wdot(a, wu_ref)
    o_ref[...] = (g * _sigmoid(g) * u).astype(o_ref.dtype)


def _swiglu(xg, ssq, wg, wu, tm=1024, tn=512):
    t, k = xg.shape
    n = wg.shape[1]
    tm, tn = min(tm, t), min(tn, n)
    return pl.pallas_call(
        _swiglu_kernel,
        grid=(t // tm, n // tn),
        in_specs=[
            pl.BlockSpec((tm, k), lambda i, j: (i, 0)),
            pl.BlockSpec((tm, LANES), lambda i, j: (i, 0)),
            pl.BlockSpec((k, tn), lambda i, j: (0, j)),
            pl.BlockSpec((k, tn), lambda i, j: (0, j)),
        ],
        out_specs=pl.BlockSpec((tm, tn), lambda i, j: (i, j)),
        out_shape=jax.ShapeDtypeStruct((t, n), jnp.bfloat16),
        compiler_params=_cparams(("parallel", "parallel")),
        name="swiglu",
    )(xg, ssq, wg, wu)


def _merge_kernel(oa_ref, ob_ref, wa_ref, wb_ref, ga_ref, gb_ref, o_ref):
    a = _wdot(oa_ref[...], wa_ref)
    b = _wdot(ob_ref[...], wb_ref)
    ga = _sigmoid(ga_ref[...].astype(jnp.float32))
    gb = _sigmoid(gb_ref[...].astype(jnp.float32))
    o_ref[...] = (ga * a + gb * b).astype(o_ref.dtype)


def _merge(o_a, o_b, w_up_a, w_up_b, proj, ga_col, gb_col, tm=1024, tn=512):
    t, ka = o_a.shape
    kb = o_b.shape[1]
    n = w_up_a.shape[1]
    tm, tn = min(tm, t), min(tn, n)
    ga_blk, gb_blk = ga_col // tn, gb_col // tn
    return pl.pallas_call(
        _merge_kernel,
        grid=(t // tm, n // tn),
        in_specs=[
            pl.BlockSpec((tm, ka), lambda i, j: (i, 0)),
            pl.BlockSpec((tm, kb), lambda i, j: (i, 0)),
            pl.BlockSpec((ka, tn), lambda i, j: (0, j)),
            pl.BlockSpec((kb, tn), lambda i, j: (0, j)),
            pl.BlockSpec((tm, tn), lambda i, j: (i, ga_blk + j)),
            pl.BlockSpec((tm, tn), lambda i, j: (i, gb_blk + j)),
        ],
        out_specs=pl.BlockSpec((tm, tn), lambda i, j: (i, j)),
        out_shape=jax.ShapeDtypeStruct((t, n), jnp.bfloat16),
        compiler_params=_cparams(("parallel", "parallel")),
        name="merge",
    )(o_a, o_b, w_up_a, w_up_b, proj, proj)


def _ple_kernel(xg_ref, ssq_ref, wg_ref, p_ref, wp_ref, r_ref, o_ref):
    a = xg_ref[...]
    g = _scale_rows(_wdot(a, wg_ref), _row_rstd(ssq_ref, a.shape[1]))
    e = _wdot(p_ref[...], wp_ref)
    o_ref[...] = r_ref[...] + _sigmoid(g) * e


def _ple(xg, ssq, w_gate, p, w_ple, r, tm=1024, tn=512):
    t, k = xg.shape
    kp = p.shape[1]
    n = w_gate.shape[1]
    tm, tn = min(tm, t), min(tn, n)
    return pl.pallas_call(
        _ple_kernel,
        grid=(t // tm, n // tn),
        in_specs=[
            pl.BlockSpec((tm, k), lambda i, j: (i, 0)),
            pl.BlockSpec((tm, LANES), lambda i, j: (i, 0)),
            pl.BlockSpec((k, tn), lambda i, j: (0, j)),
            pl.BlockSpec((tm, kp), lambda i, j: (i, 0)),
            pl.BlockSpec((kp, tn), lambda i, j: (0, j)),
            pl.BlockSpec((tm, tn), lambda i, j: (i, j)),
        ],
        out_specs=pl.BlockSpec((tm, tn), lambda i, j: (i, j)),
        out_shape=jax.ShapeDtypeStruct((t, n), jnp.float32),
        compiler_params=_cparams(("parallel", "parallel")),
        name="ple",
    )(xg, ssq, w_gate, p, w_ple, r)


def _rope_consts():
    half = HEAD_DIM // 2
    inv_full = jnp.power(ROPE_THETA, -jnp.arange(half, dtype=jnp.float32) * (2.0 / HEAD_DIM))
    half_i = IDX_ROPE_DIM // 2
    inv_idx = jnp.power(ROPE_THETA, -jnp.arange(half_i, dtype=jnp.float32) * (2.0 / IDX_ROPE_DIM))
    zeros = jnp.zeros((IDX_DIM - IDX_ROPE_DIM,), jnp.float32)
    sign = np.concatenate([-np.ones(half, np.float32), np.ones(half, np.float32)])
    mask_a = np.zeros(IDX_DIM, np.float32)
    mask_a[:half_i] = -1.0
    mask_b = np.zeros(IDX_DIM, np.float32)
    mask_b[half_i:IDX_ROPE_DIM] = 1.0
    reps = LANES // IDX_DIM
    rows = [
        jnp.concatenate([inv_full, inv_full]),
        jnp.asarray(sign),
        jnp.tile(jnp.concatenate([inv_idx, inv_idx, zeros]), reps),
        jnp.asarray(np.tile(mask_a, reps)),
        jnp.asarray(np.tile(mask_b, reps)),
    ]
    rows += [jnp.zeros((LANES,), jnp.float32)] * 3
    return jnp.stack(rows)


def _rope_tables_kernel(pos_ref, c_ref, cf_ref, sf_ref, ci_ref, sa_ref, sb_ref):
    pos = pos_ref[...].astype(jnp.float32)
    ang = pos * c_ref[0:1, :]
    cf_ref[...] = jnp.cos(ang)
    sf_ref[...] = jnp.sin(ang) * c_ref[1:2, :]
    ang_i = pos * c_ref[2:3, :]
    ci_ref[...] = jnp.cos(ang_i)
    s_i = jnp.sin(ang_i)
    sa_ref[...] = s_i * c_ref[3:4, :]
    sb_ref[...] = s_i * c_ref[4:5, :]


def _rope_tables(pos_col, tm=1024):
    t = pos_col.shape[0]
    tm = min(tm, t)
    tab = jax.ShapeDtypeStruct((t, LANES), jnp.float32)
    spec = pl.BlockSpec((tm, LANES), lambda i: (i, 0))
    return pl.pallas_call(
        _rope_tables_kernel,
        grid=(t // tm,),
        in_specs=[pl.BlockSpec((tm, 1), lambda i: (i, 0)), pl.BlockSpec((8, LANES), lambda i: (0, 0))],
        out_specs=[spec] * 5,
        out_shape=[tab] * 5,
        compiler_params=_cparams(("parallel",)),
        name="rope_tables",
    )(pos_col, _rope_consts())


def _rope_full(x, cf, sf):
    return x * cf + pltpu.roll(x, HEAD_DIM // 2, axis=1) * sf


def _rope_idx(x, ci, sa, sb):
    half = IDX_ROPE_DIM // 2
    return x * ci + pltpu.roll(x, LANES - half, axis=1) * sa + pltpu.roll(x, half, axis=1) * sb


def _head_rms(x, g):
    ms = jnp.mean(x * x, axis=-1, keepdims=True)
    return x * lax.rsqrt(ms + EPS) * g


def _q_proj_kernel(cq_ref, gcq_ref, wa_ref, wi_ref, gqa_ref, cf_ref, sf_ref, ci_ref, sa_ref, sb_ref, qa_ref, qi_ref):
    cq = cq_ref[...].astype(jnp.float32)
    ms = jnp.mean(cq * cq, axis=-1, keepdims=True)
    cq = (cq * lax.rsqrt(ms + EPS) * gcq_ref[...]).astype(jnp.bfloat16)
    cf, sf = cf_ref[...], sf_ref[...]
    gqa = gqa_ref[...]
    for j in range(N_HEADS_A // 2):
        acc = jnp.dot(cq, wa_ref[:, j * 2 * HEAD_DIM:(j + 1) * 2 * HEAD_DIM], preferred_element_type=jnp.float32)
        for e in range(2):
            hq = _head_rms(acc[:, e * HEAD_DIM:(e + 1) * HEAD_DIM], gqa)
            qa_ref[2 * j + e] = (_rope_full(hq, cf, sf) * Q_SCALE).astype(qa_ref.dtype)
    ci, sa, sb = ci_ref[...], sa_ref[...], sb_ref[...]
    low_half = lax.broadcasted_iota(jnp.int32, (cq.shape[0], LANES), 1) < IDX_DIM
    per_dot = 2 * LANES // IDX_DIM
    for j in range(N_IDX_HEADS // per_dot):
        acc = jnp.dot(cq, wi_ref[:, j * 2 * LANES:(j + 1) * 2 * LANES], preferred_element_type=jnp.float32)
        for c in range(2):
            pair = _rope_idx(acc[:, c * LANES:(c + 1) * LANES], ci, sa, sb)
            for e, head in enumerate((pair, pltpu.roll(pair, IDX_DIM, axis=1))):
                qi_ref[per_dot * j + 2 * c + e] = jnp.where(low_half, head, 0.0).astype(qi_ref.dtype)


def _q_proj(proj, g_cq, w_uq, w_uq_idx, g_q_a, tabs, tm=512):
    t = proj.shape[0]
    tm = min(tm, t)
    cf, sf, ci, sa, sb = tabs
    tab_spec = pl.BlockSpec((tm, LANES), lambda i: (i, 0))
    return pl.pallas_call(
        _q_proj_kernel,
        grid=(t // tm,),
        in_specs=[
            pl.BlockSpec((tm, Q_RANK), lambda i: (i, 0)),
            pl.BlockSpec((1, Q_RANK), lambda i: (0, 0)),
            pl.BlockSpec((Q_RANK, W_A), lambda i: (0, 0)),
            pl.BlockSpec((Q_RANK, N_IDX_HEADS * IDX_DIM), lambda i: (0, 0)),
            pl.BlockSpec((1, HEAD_DIM), lambda i: (0, 0)),
            tab_spec, tab_spec, tab_spec, tab_spec, tab_spec,
        ],
        out_specs=[
            pl.BlockSpec((N_HEADS_A, tm, HEAD_DIM), lambda i: (0, i, 0)),
            pl.BlockSpec((N_IDX_HEADS, tm, LANES), lambda i: (0, i, 0)),
        ],
        out_shape=[
            jax.ShapeDtypeStruct((N_HEADS_A, t, HEAD_DIM), jnp.bfloat16),
            jax.ShapeDtypeStruct((N_IDX_HEADS, t, LANES), jnp.bfloat16),
        ],
        compiler_params=_cparams(("parallel",)),
        name="q_proj",
    )(proj, g_cq.reshape(1, Q_RANK), w_uq, w_uq_idx, g_q_a.reshape(1, HEAD_DIM), cf, sf, ci, sa, sb)


def _ka_kernel(x_ref, g_ref, cf_ref, sf_ref, o_ref):
    x = x_ref[...].astype(jnp.float32)
    g, cf, sf = g_ref[...], cf_ref[...], sf_ref[...]
    for h in range(N_KV_A):
        hk = _head_rms(x[:, h * HEAD_DIM:(h + 1) * HEAD_DIM], g)
        o_ref[:, h * HEAD_DIM:(h + 1) * HEAD_DIM] = _rope_full(hk, cf, sf).astype(o_ref.dtype)


def _ka_norm_rope(proj, col, g_k_a, cf, sf, tm=512):
    t = proj.shape[0]
    tm = min(tm, t)
    tab_spec = pl.BlockSpec((tm, LANES), lambda i: (i, 0))
    return pl.pallas_call(
        _ka_kernel,
        grid=(t // tm,),
        in_specs=[
            pl.BlockSpec((tm, KV_A_W), lambda i: (i, col // KV_A_W)),
            pl.BlockSpec((1, HEAD_DIM), lambda i: (0, 0)),
            tab_spec, tab_spec,
        ],
        out_specs=pl.BlockSpec((tm, KV_A_W), lambda i: (i, 0)),
        out_shape=jax.ShapeDtypeStruct((t, KV_A_W), jnp.bfloat16),
        compiler_params=_cparams(("parallel",)),
        name="ka_norm_rope",
    )(proj, g_k_a.reshape(1, HEAD_DIM), cf, sf)


def _split3_bf16(x):
    hi = x.astype(jnp.bfloat16)
    r1 = x - hi.astype(jnp.float32)
    mid = r1.astype(jnp.bfloat16)
    lo = (r1 - mid.astype(jnp.float32)).astype(jnp.bfloat16)
    return hi, mid, lo


def _misc_kernel(m_ref, c_ref, ci_ref, sa_ref, sb_ref, tri_ref, kidx_ref, wf_ref, wt_ref, frow_ref, carry_ref, *,
                 tiles_per_seq):
    i = pl.program_id(0)

    @pl.when(i % tiles_per_seq == 0)
    def _():
        carry_ref[...] = jnp.zeros_like(carry_ref)

    x = m_ref[...]
    lane = lax.broadcasted_iota(jnp.int32, x.shape, 1)
    is_k = lane < IDX_DIM
    xk = jnp.where(is_k, x, 0.0)
    mu = jnp.sum(xk, axis=-1, keepdims=True) * (1.0 / IDX_DIM)
    dk = jnp.where(is_k, x - mu, 0.0)
    var = jnp.sum(dk * dk, axis=-1, keepdims=True) * (1.0 / IDX_DIM)
    y = dk * lax.rsqrt(var + EPS) * c_ref[0:1, :] + c_ref[1:2, :]
    y = _rope_idx(y, ci_ref[...], sa_ref[...], sb_ref[...])
    kidx_ref[...] = jnp.where(is_k, y, 0.0).astype(kidx_ref.dtype)
    f = x + c_ref[2:3, :]
    log_f = jnp.minimum(f, 0.0) - jnp.log1p(jnp.exp(-jnp.abs(f)))
    hi, mid, lo = _split3_bf16(log_f)
    tri = tri_ref[...]
    csum = (jnp.dot(tri, hi, preferred_element_type=jnp.float32)
            + jnp.dot(tri, mid, preferred_element_type=jnp.float32)
            + jnp.dot(tri, lo, preferred_element_type=jnp.float32))
    csum = csum + carry_ref[0:1, :]
    carry_ref[...] = jnp.broadcast_to(csum[-1:, :], carry_ref.shape)
    is_w = (lane >= MISC_W_LANE) & (lane < MISC_F_LANE)
    wf = jnp.where(is_w, x * c_ref[3:4, :], csum * LOG2E)
    wf_ref[...] = wf
    wf_t = wf.T
    wt_ref[...] = wf_t[MISC_W_LANE:MISC_F_LANE, :]
    frow_ref[0] = wf_t[MISC_F_LANE:MISC_F_LANE + N_HEADS_B, :]


def _misc_post(misc, g_kidx, b_kidx, b_forget, tabs, seq):
    t = misc.shape[0]
    tm = min(KV_TILE, seq)
    idx_w_scale = (N_IDX_HEADS ** -0.5) * (IDX_DIM ** -0.5)
    pad = lambda v, off: jnp.zeros((LANES,), jnp.float32).at[off:off + v.shape[0]].set(v)
    consts = jnp.stack([
        pad(g_kidx, 0), pad(b_kidx, 0), pad(b_forget, MISC_F_LANE),
        pad(jnp.full((N_IDX_HEADS,), idx_w_scale, jnp.float32), MISC_W_LANE),
    ] + [jnp.zeros((LANES,), jnp.float32)] * 4)
    tri = jnp.asarray(np.tril(np.ones((tm, tm), np.float32)), jnp.bfloat16)
    _, _, ci, sa, sb = tabs
    tab_spec = pl.BlockSpec((tm, LANES), lambda i: (i, 0))
    return pl.pallas_call(
        functools.partial(_misc_kernel, tiles_per_seq=seq // tm),
        grid=(t // tm,),
        in_specs=[
            tab_spec,
            pl.BlockSpec((8, LANES), lambda i: (0, 0)),
            tab_spec, tab_spec, tab_spec,
            pl.BlockSpec((tm, tm), lambda i: (0, 0)),
        ],
        out_specs=[
            tab_spec, tab_spec,
            pl.BlockSpec((N_IDX_HEADS, tm), lambda i: (0, i)),
            pl.BlockSpec((1, N_HEADS_B, tm), lambda i: (i, 0, 0)),
        ],
        out_shape=[
            jax.ShapeDtypeStruct((t, LANES), jnp.bfloat16),
            jax.ShapeDtypeStruct((t, LANES), jnp.float32),
            jax.ShapeDtypeStruct((N_IDX_HEADS, t), jnp.float32),
            jax.ShapeDtypeStruct((t // tm, N_HEADS_B, tm), jnp.float32),
        ],
        scratch_shapes=[pltpu.VMEM((8, LANES), jnp.float32)],
        compiler_params=_cparams(("arbitrary",)),
        name="misc_post",
    )(misc, consts, ci, sa, sb, tri)


IDX_HEAD_GROUP = 8
IDX_SLAB = 128
COUNT_ROWS = 16
HALVINGS_PER_CHECK = 4


def _indexer_kernel(q_ref, k_ref, w_ref, o_ref, sc_ref, *, tq, tk, n_kt, top_k):
    qi = pl.program_id(1)
    t0 = qi * tq
    n_vis = (t0 + tq + tk - 1) // tk
    t_row = lax.broadcasted_iota(jnp.int32, (1, tq), 1) + t0
    vis_end = (jnp.right_shift(t_row, CHUNK_SHIFT) + 1) * CHUNK

    def score_tile(c, carry):
        mx, mn = carry
        kt = k_ref[pl.ds(pl.multiple_of(c * tk, tk), tk), :]
        for g in range(N_IDX_HEADS // IDX_HEAD_GROUP):
            qg = q_ref[g * IDX_HEAD_GROUP:(g + 1) * IDX_HEAD_GROUP].reshape(IDX_HEAD_GROUP * tq, LANES)
            st = _nt_dot(kt, qg)
            for r in range(tk // IDX_SLAB):
                rows = slice(r * IDX_SLAB, (r + 1) * IDX_SLAB)
                part = jnp.zeros((IDX_SLAB, tq), jnp.float32)
                for j in range(IDX_HEAD_GROUP):
                    h = g * IDX_HEAD_GROUP + j
                    part = part + jnp.maximum(st[rows, j * tq:(j + 1) * tq], 0.0) * w_ref[h:h + 1, :]
                if g == 0:
                    sc_ref[c, rows, :] = part
                else:
                    sc_ref[c, rows, :] += part
        s_col = lax.broadcasted_iota(jnp.int32, (tk, 1), 0) + c * tk
        adm = s_col < vis_end
        sc = sc_ref[c]
        sc_ref[c] = jnp.where(adm, sc, SCORE_NEG)
        mx = jnp.maximum(mx, jnp.max(jnp.where(adm, sc, SCORE_NEG), axis=0, keepdims=True))
        mn = jnp.minimum(mn, jnp.min(jnp.where(adm, sc, -SCORE_NEG), axis=0, keepdims=True))
        return mx, mn

    hi0, lo0 = lax.fori_loop(
        0, n_vis, score_tile,
        (jnp.full((1, tq), SCORE_NEG, jnp.float32), jnp.full((1, tq), -SCORE_NEG, jnp.float32)))

    def unsettled(state):
        it, _, _, active = state
        return jnp.logical_and(it < SELECT_ITERS, jnp.max(active) > 0.0)

    def halve(lo, hi, active):
        mid = 0.5 * (lo + hi)

        def count_tile(c, cnt):
            ge = jnp.where(sc_ref[c] >= mid, 1.0, 0.0)
            return cnt + jnp.sum(ge.reshape(tk // COUNT_ROWS, COUNT_ROWS, tq), axis=0)

        cnt = lax.fori_loop(0, n_vis, count_tile, jnp.zeros((COUNT_ROWS, tq), jnp.float32))
        cnt = jnp.sum(cnt, axis=0, keepdims=True)
        enough = cnt >= float(top_k)
        active = jnp.where(cnt == float(top_k), 0.0, active)
        return jnp.where(enough, mid, lo), jnp.where(enough, hi, mid), active

    def bisect(state):
        it, lo, hi, active = state
        for _ in range(HALVINGS_PER_CHECK):
            lo, hi, active = halve(lo, hi, active)
        return it + HALVINGS_PER_CHECK, lo, hi, active

    active0 = jnp.where(vis_end > top_k, 1.0, 0.0)
    _, thr, _, _ = lax.while_loop(unsettled, bisect, (jnp.int32(0), lo0, hi0, active0))

    def write_tile(c, carry):
        sel = jnp.where(sc_ref[c] >= thr, 0.0, MASK_NEG)
        o_ref[0, c] = sel.T.astype(o_ref.dtype)
        return carry

    lax.fori_loop(0, n_vis, write_tile, 0)

    def fill_tile(c, carry):
        o_ref[0, c] = jnp.full((tq, tk), MASK_NEG, o_ref.dtype)
        return carry

    lax.fori_loop(n_vis, n_kt, fill_tile, 0)


def _indexer_mask(q_idx, k_idx, w_t, batch, seq, top_k, tq=256):
    tq, tk = min(tq, seq), min(KV_TILE, seq)
    n_qt, n_kt = seq // tq, seq // tk
    return pl.pallas_call(
        functools.partial(_indexer_kernel, tq=tq, tk=tk, n_kt=n_kt, top_k=top_k),
        grid=(batch, n_qt),
        in_specs=[
            pl.BlockSpec((N_IDX_HEADS, tq, LANES), lambda b, i: (0, b * n_qt + i, 0)),
            pl.BlockSpec((seq, LANES), lambda b, i: (b, 0)),
            pl.BlockSpec((N_IDX_HEADS, tq), lambda b, i: (0, b * n_qt + i)),
        ],
        out_specs=pl.BlockSpec((1, n_kt, tq, tk), lambda b, i: (b, 0, i, 0)),
        out_shape=jax.ShapeDtypeStruct((batch, n_kt, seq, tk), jnp.bfloat16),
        scratch_shapes=[pltpu.VMEM((n_kt, tk, tq), jnp.float32)],
        compiler_params=_cparams(("parallel", "parallel")),
        name="indexer_mask",
    )(q_idx, k_idx, w_t)


def _softmax_update(s, row_shift, v, m_ref, l_ref, acc_ref, e):
    rows, tk = s.shape
    chunks = [s[:, c * LANES:(c + 1) * LANES] for c in range(tk // LANES)]
    mx = functools.reduce(jnp.maximum, chunks)
    mx = jnp.broadcast_to(jnp.max(mx, axis=-1, keepdims=True), (rows, LANES))
    m_prev = m_ref[e]
    m_new = jnp.maximum(m_prev, mx + row_shift)
    alpha = jnp.exp2(m_prev - m_new)
    r = m_new - row_shift
    p = jnp.concatenate([jnp.exp2(c - r) for c in chunks], axis=1).astype(v.dtype)
    v_ones = jnp.concatenate([v, jnp.ones((tk, LANES), v.dtype)], axis=1)
    pv = jnp.dot(p, v_ones, preferred_element_type=jnp.float32)
    acc_ref[e] = alpha * acc_ref[e] + pv[:, :HEAD_DIM]
    l_ref[e] = alpha * l_ref[e] + pv[:, HEAD_DIM:]
    m_ref[e] = m_new


def _softmax_init(m_ref, l_ref, acc_ref):
    m_ref[...] = jnp.full(m_ref.shape, MASK_NEG, jnp.float32)
    l_ref[...] = jnp.zeros(l_ref.shape, jnp.float32)
    acc_ref[...] = jnp.zeros(acc_ref.shape, jnp.float32)


def _dsa_attn_kernel(q_ref, k_ref, v_ref, b_ref, o_ref, m_ref, l_ref, acc_ref, *, tq, tk):
    qi = pl.program_id(1)
    n_need = ((qi + 1) * tq + tk - 1) // tk
    rows = GROUP_A * tq
    _softmax_init(m_ref, l_ref, acc_ref)

    def kv_step(j, carry):
        off = pl.multiple_of(j * tk, tk)
        bias = b_ref[0, j].astype(jnp.float32)[None]
        for g in range(N_KV_A):
            cols = slice(g * HEAD_DIM, (g + 1) * HEAD_DIM)
            q = q_ref[g * GROUP_A:(g + 1) * GROUP_A].reshape(rows, HEAD_DIM)
            s = _nt_dot(q, k_ref[pl.ds(off, tk), cols])
            s = (s.reshape(GROUP_A, tq, tk) + bias).reshape(rows, tk)
            _softmax_update(s, 0.0, v_ref[pl.ds(off, tk), cols], m_ref, l_ref, acc_ref, g)
        return carry

    lax.fori_loop(0, n_need, kv_step, 0)

    for g in range(N_KV_A):
        out = acc_ref[g] / l_ref[g]
        for h in range(GROUP_A):
            c0 = (g * GROUP_A + h) * HEAD_DIM
            o_ref[:, c0:c0 + HEAD_DIM] = out[h * tq:(h + 1) * tq].astype(o_ref.dtype)


def _dsa_attention(q_a, k_a, proj, v_col, bias, batch, seq, tq=512):
    tk = bias.shape[-1]
    tq = min(tq, seq)
    n_qt, n_kt = seq // tq, seq // tk
    rows = GROUP_A * tq
    return pl.pallas_call(
        functools.partial(_dsa_attn_kernel, tq=tq, tk=tk),
        grid=(batch, n_qt),
        in_specs=[
            pl.BlockSpec((N_HEADS_A, tq, HEAD_DIM), lambda b, i: (0, b * n_qt + i, 0)),
            pl.BlockSpec((seq, KV_A_W), lambda b, i: (b, 0)),
            pl.BlockSpec((seq, KV_A_W), lambda b, i: (b, v_col // KV_A_W)),
            pl.BlockSpec((1, n_kt, tq, tk), lambda b, i: (b, 0, i, 0)),
        ],
        out_specs=pl.BlockSpec((tq, W_A), lambda b, i: (b * n_qt + i, 0)),
        out_shape=jax.ShapeDtypeStruct((batch * seq, W_A), jnp.bfloat16),
        scratch_shapes=[
            pltpu.VMEM((N_KV_A, rows, LANES), jnp.float32),
            pltpu.VMEM((N_KV_A, rows, LANES), jnp.float32),
            pltpu.VMEM((N_KV_A, rows, HEAD_DIM), jnp.float32),
        ],
        compiler_params=_cparams(("parallel", "parallel")),
        name="dsa_attention",
    )(q_a, k_a, proj, bias)


FOX_HEADS_PER_STEP = 4


def _fox_kernel(q_ref, k_ref, v_ref, gq_ref, gk_ref, fq_ref, fk_ref, o_ref, kn_ref, s0_ref, s1_ref, m_ref, l_ref,
                acc_ref, *, tile):
    hp, qi = pl.program_id(1), pl.program_id(2)
    n_t = kn_ref.shape[0] // tile

    @pl.when(qi == 0)
    def _():
        def norm_keys(j, carry):
            off = pl.multiple_of(j * tile, tile)
            for e in range(FOX_HEADS_PER_STEP):
                cols = slice(e * HEAD_DIM, (e + 1) * HEAD_DIM)
                k = k_ref[pl.ds(off, tile), cols].astype(jnp.float32)
                kn_ref[pl.ds(off, tile), cols] = _head_rms(k, gk_ref[...]).astype(kn_ref.dtype)
            return carry

        lax.fori_loop(0, n_t, norm_keys, 0)

    lane = lax.broadcasted_iota(jnp.int32, (tile, LANES), 1)
    heads = [hp * FOX_HEADS_PER_STEP + e for e in range(FOX_HEADS_PER_STEP)]
    fq = [jnp.broadcast_to(jnp.sum(jnp.where(lane == MISC_F_LANE + h, fq_ref[...], 0.0), axis=1, keepdims=True),
                           (tile, LANES)) for h in heads]
    qn = [_head_rms(q_ref[:, e * HEAD_DIM:(e + 1) * HEAD_DIM].astype(jnp.float32), gq_ref[...]).astype(kn_ref.dtype)
          for e in range(FOX_HEADS_PER_STEP)]
    _softmax_init(m_ref, l_ref, acc_ref)

    def logits(j, s_ref):
        off = pl.multiple_of(j * tile, tile)
        for e, h in enumerate(heads):
            cols = slice(e * HEAD_DIM, (e + 1) * HEAD_DIM)
            s_ref[e] = _nt_dot(qn[e], kn_ref[pl.ds(off, tile), cols]) - fk_ref[j, pl.ds(h, 1), :]

    def attend(j, s_ref, diagonal):
        off = pl.multiple_of(j * tile, tile)
        for e in range(FOX_HEADS_PER_STEP):
            cols = slice(e * HEAD_DIM, (e + 1) * HEAD_DIM)
            s = s_ref[e]
            if diagonal:
                t_pos = lax.broadcasted_iota(jnp.int32, (tile, tile), 0)
                s_pos = lax.broadcasted_iota(jnp.int32, (tile, tile), 1)
                s = jnp.where(s_pos <= t_pos, s, MASK_NEG)
            _softmax_update(s, fq[e], v_ref[pl.ds(off, tile), cols], m_ref, l_ref, acc_ref, e)

    def tile_pair(i, carry):
        logits(2 * i + 1, s1_ref)
        attend(2 * i, s0_ref, False)
        logits(2 * i + 2, s0_ref)
        attend(2 * i + 1, s1_ref, False)
        return carry

    logits(0, s0_ref)
    lax.fori_loop(0, qi // 2, tile_pair, 0)

    @pl.when(qi % 2 == 0)
    def _():
        attend(qi, s0_ref, True)

    @pl.when(qi % 2 == 1)
    def _():
        logits(qi, s1_ref)
        attend(qi - 1, s0_ref, False)
        attend(qi, s1_ref, True)

    for e in range(FOX_HEADS_PER_STEP):
        o_ref[:, e * HEAD_DIM:(e + 1) * HEAD_DIM] = (acc_ref[e] / l_ref[e]).astype(o_ref.dtype)


def _fox_attention(proj, q_col, g_q, g_k, wf, f_row, batch, seq):
    tile = min(KV_TILE, seq)
    n_t = seq // tile
    wb = FOX_HEADS_PER_STEP * HEAD_DIM
    q_blk, k_blk, v_blk = q_col // wb, (q_col + W_B) // wb, (q_col + 2 * W_B) // wb
    gain_spec = pl.BlockSpec((1, HEAD_DIM), lambda b, h, i: (0, 0))
    return pl.pallas_call(
        functools.partial(_fox_kernel, tile=tile),
        grid=(batch, N_HEADS_B // FOX_HEADS_PER_STEP, n_t),
        in_specs=[
            pl.BlockSpec((tile, wb), lambda b, h, i: (b * n_t + i, q_blk + h)),
            pl.BlockSpec((seq, wb), lambda b, h, i: (b, k_blk + h)),
            pl.BlockSpec((seq, wb), lambda b, h, i: (b, v_blk + h)),
            gain_spec, gain_spec,
            pl.BlockSpec((tile, LANES), lambda b, h, i: (b * n_t + i, 0)),
            pl.BlockSpec((n_t, N_HEADS_B, tile), lambda b, h, i: (b, 0, 0)),
        ],
        out_specs=pl.BlockSpec((tile, wb), lambda b, h, i: (b * n_t + i, h)),
        out_shape=jax.ShapeDtypeStruct((batch * seq, W_B), jnp.bfloat16),
        scratch_shapes=[
            pltpu.VMEM((seq, wb), jnp.bfloat16),
            pltpu.VMEM((FOX_HEADS_PER_STEP, tile, tile), jnp.float32),
            pltpu.VMEM((FOX_HEADS_PER_STEP, tile, tile), jnp.float32),
            pltpu.VMEM((FOX_HEADS_PER_STEP, tile, LANES), jnp.float32),
            pltpu.VMEM((FOX_HEADS_PER_STEP, tile, LANES), jnp.float32),
            pltpu.VMEM((FOX_HEADS_PER_STEP, tile, HEAD_DIM), jnp.float32),
        ],
        compiler_params=_cparams(("parallel", "parallel", "arbitrary")),
        name="fox_attention",
    )(proj, proj, proj, g_q.reshape(1, HEAD_DIM), g_k.reshape(1, HEAD_DIM), wf, f_row)


def _layer(x, p, pos_col, batch, seq, g_attn, w_in, g_cq, w_uq, w_uq_idx, g_kidx, b_kidx, g_q_a, g_k_a, b_forget,
           g_q_b, g_k_b, w_up_a, w_up_b, w_o, g_ffn, w_ffn_gate, w_ffn_up, w_ffn_down, g_ple, w_ple, w_ple_gate):
    d = x.shape[1]
    bf = jnp.bfloat16
    top_k = min(INDEX_TOPK, seq // 4)

    o_kidx = Q_RANK + 2 * KV_A_W
    o_qb = o_kidx + IDX_DIM + N_IDX_HEADS
    o_fb = o_qb + 3 * W_B
    o_ga = o_fb + N_HEADS_B
    w_in_t = jnp.swapaxes(w_in, 0, 1)
    w_misc = _stage_misc(w_in_t, o_kidx, o_qb - o_kidx, o_fb, o_ga - o_fb)
    col_ka, col_va = Q_RANK, Q_RANK + KV_A_W
    col_qb = Q_RANK + 2 * KV_A_W
    col_vb = col_qb + 2 * W_B
    col_ga = col_vb + W_B
    col_gb = col_ga + d

    tabs = _rope_tables(pos_col)
    cf, sf = tabs[0], tabs[1]

    h = _rmsnorm(x, g_attn)
    proj = _project_regrouped(h, w_in_t, [(0, o_kidx), (o_qb, o_fb - o_qb), (o_ga, 2 * d)])
    misc = _matmul(h, w_misc, jnp.float32, tn=LANES, name="proj_misc")

    q_a, q_idx = _q_proj(proj, g_cq, w_uq.astype(bf), w_uq_idx.astype(bf), g_q_a, tabs)
    k_a = _ka_norm_rope(proj, col_ka, g_k_a, cf, sf)
    k_idx, wf, w_t, f_row = _misc_post(misc, g_kidx, b_kidx, b_forget, tabs, seq)
    bias = _indexer_mask(q_idx, k_idx, w_t, batch, seq, top_k)
    o_a = _dsa_attention(q_a, k_a, proj, col_va, bias, batch, seq)

    o_b = _fox_attention(proj, col_qb, g_q_b * Q_SCALE, g_k_b, wf, f_row, batch, seq)

    merged = _merge(o_a, o_b, w_up_a, w_up_b, proj, col_ga, col_gb)
    x, xg, ssq = _matmul_residual_norm(merged, w_o, x, g_ffn, name="out_proj")

    u = _swiglu(xg, ssq, w_ffn_gate, w_ffn_up, tn=FFN_TILE)
    x, xg, ssq = _matmul_residual_norm(u, w_ffn_down.astype(bf), x, g_ple, a_row_ssq=ssq, tm=512, name="ffn_down")

    return _ple(xg, ssq, w_ple_gate, p.astype(bf), w_ple, x)


def kernel(x, p, positions, g_attn, w_in, g_cq, w_uq, w_uq_idx, g_kidx, b_kidx, g_q_a, g_k_a, b_forget, g_q_b, g_k_b,
           w_up_a, w_up_b, w_o, g_ffn, w_ffn_gate, w_ffn_up, w_ffn_down, g_ple, w_ple, w_ple_gate):
    batch, seq, d = x.shape
    depth = w_in.shape[0]
    xf = x.reshape(batch * seq, d)
    pos_col = positions.reshape(batch * seq, 1)
    for i in range(depth):
        xf = _layer(xf, p[i].reshape(batch * seq, -1), pos_col, batch, seq, g_attn[i], w_in[i], g_cq[i], w_uq[i],
                    w_uq_idx[i], g_kidx[i], b_kidx[i], g_q_a[i], g_k_a[i], b_forget[i], g_q_b[i], g_k_b[i],
                    w_up_a[i], w_up_b[i], w_o[i], g_ffn[i], w_ffn_gate[i], w_ffn_up[i], w_ffn_down[i], g_ple[i],
                    w_ple[i], w_ple_gate[i])
    return xf.reshape(batch, seq, d)
```
